```python
import jax, jax.numpy as jnp
from jax import lax
import numpy as np


D_MODEL = 2048
BATCH = 4
SEQ = 8192
DEPTH = 1
DEC_BATCH = 8
DEC_SEQ = 64
PAST_LEN = 1024

CHUNK = 64
WINDOW = 128
WIN_CHUNKS = WINDOW // CHUNK
CONV_WIDTH = 3
HEAD_DIM = 64
N_Q_HEADS = (D_MODEL // 2) // HEAD_DIM
N_KV_HEADS = 4
GQA_GROUP = N_Q_HEADS // N_KV_HEADS
ATTN_WIDTH = N_Q_HEADS * HEAD_DIM
KV_WIDTH = N_KV_HEADS * HEAD_DIM
D_CONV = D_MODEL // 2
MIX_WIDTH = D_CONV + ATTN_WIDTH
N_IN = 3 * D_CONV + ATTN_WIDTH + 2 * KV_WIDTH
ROT_DIM = HEAD_DIM // 4
ROPE_THETA = 500000.0
N_MEM = 256
N_MEM_HEADS = 4
MEM_HEAD_DIM = D_MODEL // N_MEM_HEADS
D_FF = 11 * D_MODEL // 4
EPS = 1e-6
NEG_INF = -1e30

kernel_name = 'hybrid_streaming_encoder_step'


def rmsnorm(x, g):
    xf = x.astype(jnp.float32)
    y = xf * lax.rsqrt(jnp.mean(xf * xf, axis=-1, keepdims=True) + EPS) * g.astype(jnp.float32)
    return y.astype(x.dtype)


def rope_partial(x, pos):
    inv = ROPE_THETA ** (-jnp.arange(0, ROT_DIM, 2, dtype=jnp.float32) / ROT_DIM)
    ang = pos[:, None] * inv[None, :]
    cos = jnp.cos(ang)[:, None, :]
    sin = jnp.sin(ang)[:, None, :]
    xr = x[..., :ROT_DIM].astype(jnp.float32)
    x1, x2 = xr[..., :ROT_DIM // 2], xr[..., ROT_DIM // 2:]
    rot = jnp.concatenate([x1 * cos - x2 * sin, x2 * cos + x1 * sin], axis=-1).astype(x.dtype)
    return jnp.concatenate([rot, x[..., ROT_DIM:]], axis=-1)


def causal_dwconv(u, prev, w):
    t = u.shape[1]
    up = jnp.concatenate([prev.astype(u.dtype), u], axis=1)
    y = w[0] * up[:, 0:t]
    for i in range(1, CONV_WIDTH):
        y = y + w[i] * up[:, i:i + t]
    return y, up[:, -(CONV_WIDTH - 1):]


def sink_attention(q, k, v, sinks, mask):
    s = jnp.einsum('...qkgd,...lkd->...kgql', q, k).astype(jnp.float32) * (HEAD_DIM ** -0.5)
    if mask is not None:
        s = jnp.where(mask, s, NEG_INF)
    sink = jnp.broadcast_to(sinks.astype(jnp.float32)[:, :, None, None], s.shape[:-1] + (1,))
    p = jax.nn.softmax(jnp.concatenate([s, sink], axis=-1), axis=-1)[..., :-1]
    return jnp.einsum('...kgql,...lkd->...qkgd', p.astype(v.dtype), v)


def swa_prompt(q, k, v, sinks):
    n, t = q.shape[0], q.shape[1]
    nc = t // CHUNK
    qb = q.reshape(n, nc, CHUNK, N_KV_HEADS, GQA_GROUP, HEAD_DIM)

    def band(a):
        ab = a.reshape(n, nc, CHUNK, N_KV_HEADS, HEAD_DIM)
        ap = jnp.pad(ab, ((0, 0), (WIN_CHUNKS, 0), (0, 0), (0, 0), (0, 0)))
        return jnp.concatenate([ap[:, j:j + nc] for j in range(WIN_CHUNKS + 1)], axis=2)

    band_len = (WIN_CHUNKS + 1) * CHUNK
    key_chunk = jnp.arange(nc)[:, None] - WIN_CHUNKS + jnp.arange(band_len)[None, :] // CHUNK
    mask = (key_chunk >= 0)[:, None, None, None, :]
    o = sink_attention(qb, band(k), band(v), sinks.reshape(N_KV_HEADS, GQA_GROUP), mask)
    keep = min(WINDOW, t)
    return o.reshape(n, t, ATTN_WIDTH), k[:, -keep:], v[:, -keep:]


def swa_sample(q, k, v, sinks, cache_k, cache_v):
    n, t = q.shape[0], q.shape[1]
    kk = jnp.concatenate([cache_k.astype(k.dtype), k], axis=1)
    vv = jnp.concatenate([cache_v.astype(v.dtype), v], axis=1)
    qg = q.reshape(n, t, N_KV_HEADS, GQA_GROUP, HEAD_DIM)
    o = sink_attention(qg, kk, vv, sinks.reshape(N_KV_HEADS, GQA_GROUP), None)
    keep = cache_k.shape[1]
    return o.reshape(n, t, ATTN_WIDTH), kk[:, -keep:], vv[:, -keep:]


def memory_kv(mem, g_mem, w_xk, w_xv):
    n = mem.shape[0]
    hm = rmsnorm(mem, g_mem)
    mk = (hm @ w_xk).reshape(n, N_MEM, N_MEM_HEADS, MEM_HEAD_DIM)
    mv = (hm @ w_xv).reshape(n, N_MEM, N_MEM_HEADS, MEM_HEAD_DIM)
    return mk, mv


def trunk_layer(x, pos, mem_k, mem_v, mix_prev, ffn_prev, attend,
                g_mix_pre, w_mix_in, conv_mix_w, g_grp_conv, g_grp_attn, attn_sinks,
                w_mix_out, g_mix_post, g_x_pre, w_xq, w_xo, g_x_post,
                g_ffn_pre, w_gate, w_up, conv_ffn_w, conv_ffn_b, w_down, g_ffn_post):
    n, t, _ = x.shape
    h = rmsnorm(x, g_mix_pre)
    z = h @ w_mix_in
    cuts = [D_CONV, 2 * D_CONV, 3 * D_CONV, 3 * D_CONV + ATTN_WIDTH, 3 * D_CONV + ATTN_WIDTH + KV_WIDTH]
    b_gate, c_gate, u, q, k, v = jnp.split(z, cuts, axis=-1)
    conv_out, mix_state = causal_dwconv(c_gate * u, mix_prev, conv_mix_w)
    y_conv = b_gate * conv_out
    q = rope_partial(q.reshape(n, t, N_Q_HEADS, HEAD_DIM), pos)
    k = rope_partial(k.reshape(n, t, N_KV_HEADS, HEAD_DIM), pos)
    v = v.reshape(n, t, N_KV_HEADS, HEAD_DIM)
    y_attn, swa_k, swa_v = attend(q, k, v, attn_sinks)
    y = jnp.concatenate([rmsnorm(y_conv, g_grp_conv), rmsnorm(y_attn, g_grp_attn)], axis=-1) @ w_mix_out
    x = x + rmsnorm(y, g_mix_post)
    hq = (rmsnorm(x, g_x_pre) @ w_xq).reshape(n, t, N_MEM_HEADS, MEM_HEAD_DIM)
    s = jnp.einsum('nthd,nmhd->nhtm', hq, mem_k).astype(jnp.float32) * (MEM_HEAD_DIM ** -0.5)
    pr = jax.nn.softmax(s, axis=-1).astype(mem_v.dtype)
    o = jnp.einsum('nhtm,nmhd->nthd', pr, mem_v).reshape(n, t, N_MEM_HEADS * MEM_HEAD_DIM)
    x = x + rmsnorm(o @ w_xo, g_x_post)
    h = rmsnorm(x, g_ffn_pre)
    a, ffn_state = causal_dwconv(h @ w_gate, ffn_prev, conv_ffn_w)
    y = (jax.nn.silu(a + conv_ffn_b) * (h @ w_up)) @ w_down
    x = x + rmsnorm(y, g_ffn_post)
    return x, swa_k, swa_v, mix_state, ffn_state


def setup_inputs(seed: int = 0) -> dict:
    key = jax.random.key(seed)
    ks = jax.random.split(key, 32)
    f32 = jnp.float32

    def nrm(k, shape, scale):
        return jax.random.normal(k, shape, f32) * scale

    def gain(k, width):
        return 1.0 + 0.01 * jax.random.normal(k, (DEPTH, width), f32)

    swa_len = min(WINDOW, PAST_LEN)
    return {
        'x_prompt': nrm(ks[0], (BATCH, SEQ, D_MODEL), 1.0),
        'x_sample': nrm(ks[1], (DEC_BATCH, DEC_SEQ, D_MODEL), 1.0),
        'cache_mem_k': nrm(ks[2], (DEPTH, DEC_BATCH, N_MEM, N_MEM_HEADS, MEM_HEAD_DIM), 1.0),
        'cache_mem_v': nrm(ks[3], (DEPTH, DEC_BATCH, N_MEM, N_MEM_HEADS, MEM_HEAD_DIM), 1.0),
        'cache_swa_k': nrm(ks[4], (DEPTH, DEC_BATCH, swa_len, N_KV_HEADS, HEAD_DIM), 1.0),
        'cache_swa_v': nrm(ks[5], (DEPTH, DEC_BATCH, swa_len, N_KV_HEADS, HEAD_DIM), 1.0),
        'state_mix_conv': nrm(ks[6], (DEPTH, DEC_BATCH, CONV_WIDTH - 1, D_CONV), 1.0),
        'state_ffn_conv': nrm(ks[7], (DEPTH, DEC_BATCH, CONV_WIDTH - 1, D_FF), 1.0),
        'mem_prompt': nrm(ks[8], (BATCH, N_MEM, D_MODEL), 1.0),
        'g_mix_pre': gain(ks[9], D_MODEL),
        'w_mix_in': nrm(ks[10], (DEPTH, D_MODEL, N_IN), D_MODEL ** -0.5),
        'conv_mix_w': nrm(ks[11], (DEPTH, CONV_WIDTH, D_CONV), CONV_WIDTH ** -0.5),
        'g_grp_conv': gain(ks[12], D_CONV),
        'g_grp_attn': gain(ks[13], ATTN_WIDTH),
        'attn_sinks': nrm(ks[14], (DEPTH, N_Q_HEADS), 1.0),
        'w_mix_out': nrm(ks[15], (DEPTH, MIX_WIDTH, D_MODEL), MIX_WIDTH ** -0.5),
        'g_mix_post': gain(ks[16], D_MODEL),
        'g_mem': gain(ks[17], D_MODEL),
        'w_xk': nrm(ks[18], (DEPTH, D_MODEL, N_MEM_HEADS * MEM_HEAD_DIM), D_MODEL ** -0.5),
        'w_xv': nrm(ks[19], (DEPTH, D_MODEL, N_MEM_HEADS * MEM_HEAD_DIM), D_MODEL ** -0.5),
        'g_x_pre': gain(ks[20], D_MODEL),
        'w_xq': nrm(ks[21], (DEPTH, D_MODEL, N_MEM_HEADS * MEM_HEAD_DIM), D_MODEL ** -0.5),
        'w_xo': nrm(ks[22], (DEPTH, N_MEM_HEADS * MEM_HEAD_DIM, D_MODEL), (N_MEM_HEADS * MEM_HEAD_DIM) ** -0.5),
        'g_x_post': gain(ks[23], D_MODEL),
        'g_ffn_pre': gain(ks[24], D_MODEL),
        'w_gate': nrm(ks[25], (DEPTH, D_MODEL, D_FF), D_MODEL ** -0.5),
        'w_up': nrm(ks[26], (DEPTH, D_MODEL, D_FF), D_MODEL ** -0.5),
        'conv_ffn_w': nrm(ks[27], (DEPTH, CONV_WIDTH, D_FF), CONV_WIDTH ** -0.5),
        'conv_ffn_b': nrm(ks[28], (DEPTH, D_FF), 0.01),
        'w_down': nrm(ks[29], (DEPTH, D_FF, D_MODEL), D_FF ** -0.5),
        'g_ffn_post': gain(ks[30], D_MODEL),
    }


def reference(x_prompt, x_sample, cache_mem_k, cache_mem_v, cache_swa_k, cache_swa_v,
              state_mix_conv, state_ffn_conv, mem_prompt,
              g_mix_pre, w_mix_in, conv_mix_w, g_grp_conv, g_grp_attn, attn_sinks,
              w_mix_out, g_mix_post, g_mem, w_xk, w_xv, g_x_pre, w_xq, w_xo, g_x_post,
              g_ffn_pre, w_gate, w_up, conv_ffn_w, conv_ffn_b, w_down, g_ffn_post):
    n_p, s_p, _ = x_prompt.shape
    n_s, s_s, _ = x_sample.shape
    pos_p = jnp.arange(s_p, dtype=jnp.float32)
    pos_s = PAST_LEN + jnp.arange(s_s, dtype=jnp.float32)
    zero_mix = jnp.zeros((n_p, CONV_WIDTH - 1, D_CONV), x_prompt.dtype)
    zero_ffn = jnp.zeros((n_p, CONV_WIDTH - 1, D_FF), x_prompt.dtype)

    yp, ys = x_prompt, x_sample
    mk_p, mv_p, sk_p, sv_p, mc_p, fc_p = [], [], [], [], [], []
    sk_s, sv_s, mc_s, fc_s = [], [], [], []
    for l in range(DEPTH):
        w = (g_mix_pre[l], w_mix_in[l], conv_mix_w[l], g_grp_conv[l], g_grp_attn[l], attn_sinks[l],
             w_mix_out[l], g_mix_post[l], g_x_pre[l], w_xq[l], w_xo[l], g_x_post[l],
             g_ffn_pre[l], w_gate[l], w_up[l], conv_ffn_w[l], conv_ffn_b[l], w_down[l], g_ffn_post[l])
        mk, mv = memory_kv(mem_prompt, g_mem[l], w_xk[l], w_xv[l])
        yp, a_k, a_v, a_mc, a_fc = trunk_layer(yp, pos_p, mk, mv, zero_mix, zero_ffn, swa_prompt, *w)
        mk_p.append(mk)
        mv_p.append(mv)
        sk_p.append(a_k)
        sv_p.append(a_v)
        mc_p.append(a_mc)
        fc_p.append(a_fc)
        attend_s = (lambda q, k, v, s, ck=cache_swa_k[l], cv=cache_swa_v[l]:
                    swa_sample(q, k, v, s, ck, cv))
        ys, b_k, b_v, b_mc, b_fc = trunk_layer(ys, pos_s, cache_mem_k[l], cache_mem_v[l],
                                               state_mix_conv[l], state_ffn_conv[l], attend_s, *w)
        sk_s.append(b_k)
        sv_s.append(b_v)
        mc_s.append(b_mc)
        fc_s.append(b_fc)

    return (yp, ys,
            jnp.stack(mk_p), jnp.stack(mv_p), jnp.stack(sk_p), jnp.stack(sv_p),
            jnp.stack(mc_p), jnp.stack(fc_p),
            jnp.stack(sk_s), jnp.stack(sv_s), jnp.stack(mc_s), jnp.stack(fc_s))
```

```python
import functools

import jax
import jax.numpy as jnp
from jax import lax
from jax.experimental import pallas as pl
from jax.experimental.pallas import tpu as pltpu

D_MODEL = 2048
CHUNK = 64
WINDOW = 128
CONV_WIDTH = 3
HEAD_DIM = 64
N_Q_HEADS = 16
N_KV_HEADS = 4
ATTN_WIDTH = N_Q_HEADS * HEAD_DIM
KV_WIDTH = N_KV_HEADS * HEAD_DIM
D_CONV = D_MODEL // 2
ROT_DIM = HEAD_DIM // 4
ROPE_THETA = 500000.0
N_MEM = 256
N_MEM_HEADS = 4
MEM_HEAD_DIM = D_MODEL // N_MEM_HEADS
D_FF = 11 * D_MODEL // 4
PAST_LEN = 1024
EPS = 1e-6
NEG_INF = -1e30

BF16 = jnp.bfloat16
F32 = jnp.float32

LANES = 128
SUBLANES = 8
HALF = LANES // 2
BAND = WINDOW + CHUNK
VMEM_LIMIT_BYTES = 60 * 1024 * 1024
FFN_CHUNK = 512

_NT = (((1,), (1,)), ((), ()))


def _dot(a, b):
    return jnp.dot(a, b, preferred_element_type=F32)


def _dot_nt(a, b):
    return lax.dot_general(a, b, _NT, preferred_element_type=F32)


def _rms_scale(x):
    return lax.rsqrt(jnp.mean(x * x, axis=-1, keepdims=True) + EPS)


def _resident(shape):
    nd = len(shape)
    return pl.BlockSpec(shape, lambda *_: (0,) * nd, pipeline_mode=pl.Buffered(1))


def _causal_conv(cs_ref, prev8, cur, w_ref, c0, width):
    rows = cur.shape[0]
    cs_ref[0:SUBLANES, :] = prev8
    cs_ref[SUBLANES:SUBLANES + rows, :] = cur
    x1 = cs_ref[pl.ds(SUBLANES - 1, rows), :]
    x2 = cs_ref[pl.ds(SUBLANES - 2, rows), :]
    w0 = w_ref[0:1, c0:c0 + width]
    w1 = w_ref[1:2, c0:c0 + width]
    w2 = w_ref[2:3, c0:c0 + width]
    return w0 * x2 + w1 * x1 + w2 * cur


def _memkv_kernel(mem_ref, g_ref, wk_ref, wv_ref, k32_ref, v32_ref, kbf_ref, vbf_ref, hm_ref):
    @pl.when(pl.program_id(0) == 0)
    def _():
        m = mem_ref[...]
        hm_ref[...] = (m * _rms_scale(m) * g_ref[...]).astype(BF16)

    hm = hm_ref[...]
    k = _dot(hm, wk_ref[...])
    v = _dot(hm, wv_ref[...])
    k32_ref[...] = k
    v32_ref[...] = v
    kbf_ref[...] = k.astype(BF16)
    vbf_ref[...] = v.astype(BF16)


def _memory_kv(mem2d, g_mem, wk, wv):
    rows = mem2d.shape[0]
    nc = 512
    out_w = wk.shape[1]
    col = lambda j: (0, j)
    return pl.pallas_call(
        _memkv_kernel,
        grid=(out_w // nc,),
        in_specs=[
            _resident((rows, D_MODEL)),
            _resident((1, D_MODEL)),
            pl.BlockSpec((D_MODEL, nc), col),
            pl.BlockSpec((D_MODEL, nc), col),
        ],
        out_specs=[pl.BlockSpec((rows, nc), col)] * 4,
        out_shape=[
            jax.ShapeDtypeStruct((rows, out_w), F32),
            jax.ShapeDtypeStruct((rows, out_w), F32),
            jax.ShapeDtypeStruct((rows, out_w), BF16),
            jax.ShapeDtypeStruct((rows, out_w), BF16),
        ],
        scratch_shapes=[pltpu.VMEM((rows, D_MODEL), BF16)],
        compiler_params=pltpu.CompilerParams(
            dimension_semantics=("arbitrary",), vmem_limit_bytes=VMEM_LIMIT_BYTES),
        name="memory_kv",
    )(mem2d, g_mem, wk, wv)


def _mixer_kernel(x_ref, cos_ref, s1_ref, s2_ref, kinit_ref, vinit_ref, cinit_ref,
                  gpre_ref, win_ref, convw_ref, gconv_ref, gattn_ref, sink_ref,
                  wout_ref, gpost_ref,
                  out_ref, kout_ref, vout_ref, cstate_ref,
                  yc_ref, q_ref, o_ref, ke_ref, ko_ref, ve_ref, vo_ref, cs_ref,
                  *, tm, n_seg, carry, keep, conv_cb):
    i = pl.program_id(1)
    seg = tm // n_seg
    nch = seg // CHUNK
    lo = lax.broadcasted_iota(jnp.int32, (1, LANES), 1) < HALF

    def rope(blk, cos_t, s1_t, s2_t):
        return (blk * cos_t + pltpu.roll(blk, LANES - ROT_DIM // 2, 1) * s1_t
                + pltpu.roll(blk, ROT_DIM // 2, 1) * s2_t)

    def store_kv(kf, vf, s, row0):
        rows = kf.shape[0]
        for hp in range(N_KV_HEADS // 2):
            for src, e_ref, o_ref_ in ((kf, ke_ref, ko_ref), (vf, ve_ref, vo_ref)):
                own = src[:, LANES * hp:LANES * (hp + 1)]
                swp = pltpu.roll(own, HALF, 1)
                for par in range(2):
                    h = 2 * hp + par
                    low, high = (own, swp) if par == 0 else (swp, own)
                    e_ref[s, h, row0:row0 + rows, :] = jnp.where(lo, low, 0.0).astype(BF16)
                    o_ref_[s, h, row0:row0 + rows, :] = jnp.where(lo, 0.0, high).astype(BF16)

    @pl.when(i == 0)
    def _():
        for s in range(n_seg):
            store_kv(kinit_ref[s], vinit_ref[s], s, 0)
            cstate_ref[s] = cinit_ref[s]

    x = x_ref[...]
    hb = (x * _rms_scale(x) * gpre_ref[...]).astype(BF16)

    ssq_c = jnp.zeros((tm, 1), F32)
    for cb in range(D_CONV // conv_cb):
        c0 = cb * conv_cb
        bg = _dot(hb, win_ref[:, c0:c0 + conv_cb])
        cg = _dot(hb, win_ref[:, D_CONV + c0:D_CONV + c0 + conv_cb])
        ug = _dot(hb, win_ref[:, 2 * D_CONV + c0:2 * D_CONV + c0 + conv_cb])
        cu = cg * ug
        ssq_parts = []
        for s in range(n_seg):
            r0 = s * seg
            cur = cu[r0:r0 + seg]
            conv = _causal_conv(cs_ref, cstate_ref[s, :, c0:c0 + conv_cb], cur,
                                convw_ref, c0, conv_cb)
            y = bg[r0:r0 + seg] * conv
            ssq_parts.append(jnp.sum(y * y, axis=-1, keepdims=True))
            yc_ref[r0:r0 + seg, c0:c0 + conv_cb] = (y * gconv_ref[:, c0:c0 + conv_cb]).astype(BF16)
            cstate_ref[s, :, c0:c0 + conv_cb] = cur[seg - SUBLANES:seg]
        ssq_c = ssq_c + (ssq_parts[0] if n_seg == 1 else jnp.concatenate(ssq_parts, axis=0))

    cos_t = cos_ref[...]
    s1_t = s1_ref[...]
    s2_t = s2_ref[...]
    q0 = 3 * D_CONV
    qf = _dot(hb, win_ref[:, q0:q0 + ATTN_WIDTH])
    for jb in range(ATTN_WIDTH // LANES):
        blk = rope(qf[:, LANES * jb:LANES * (jb + 1)], cos_t, s1_t, s2_t)
        q_ref[:, LANES * jb:LANES * (jb + 1)] = (blk * (HEAD_DIM ** -0.5)).astype(BF16)
    k0 = q0 + ATTN_WIDTH
    kv = _dot(hb, win_ref[:, k0:k0 + 2 * KV_WIDTH])
    kf = jnp.concatenate(
        [rope(kv[:, LANES * jb:LANES * (jb + 1)], cos_t, s1_t, s2_t)
         for jb in range(KV_WIDTH // LANES)], axis=1)
    vf = kv[:, KV_WIDTH:2 * KV_WIDTH]
    for s in range(n_seg):
        r0 = s * seg
        store_kv(kf[r0:r0 + seg], vf[r0:r0 + seg], s, WINDOW)
        kout_ref[s] = kf[r0 + seg - keep:r0 + seg]
        vout_ref[s] = vf[r0 + seg - keep:r0 + seg]

    key_pos = lax.broadcasted_iota(jnp.int32, (1, BAND), 1)
    for s in range(n_seg):
        for c in range(nch):
            r0 = s * seg + c * CHUNK
            for h in range(N_KV_HEADS):
                qa = jnp.concatenate(
                    [q_ref[r0:r0 + CHUNK, 2 * LANES * h:2 * LANES * h + LANES],
                     q_ref[r0:r0 + CHUNK, 2 * LANES * h + LANES:2 * LANES * (h + 1)]], axis=0)
                band = pl.ds(c * CHUNK, BAND)
                s_e = _dot_nt(qa, ke_ref[s, h, band, :])
                s_o = _dot_nt(qa, ko_ref[s, h, band, :])
                if carry and c < WINDOW // CHUNK:
                    first_valid = (WINDOW // CHUNK - (i * nch + c)) * CHUNK
                    valid = key_pos >= first_valid
                    s_e = jnp.where(valid, s_e, NEG_INF)
                    s_o = jnp.where(valid, s_o, NEG_INF)
                sink_e = sink_ref[h, :, 0:1]
                sink_o = sink_ref[h, :, HALF:HALF + 1]
                m_e = jnp.maximum(jnp.max(s_e, axis=-1, keepdims=True), sink_e)
                m_o = jnp.maximum(jnp.max(s_o, axis=-1, keepdims=True), sink_o)
                p_e = jnp.exp(s_e - m_e)
                p_o = jnp.exp(s_o - m_o)
                den_e = jnp.sum(p_e, axis=-1, keepdims=True) + jnp.exp(sink_e - m_e)
                den_o = jnp.sum(p_o, axis=-1, keepdims=True) + jnp.exp(sink_o - m_o)
                acc = (_dot(p_e.astype(BF16), ve_ref[s, h, band, :])
                       + _dot(p_o.astype(BF16), vo_ref[s, h, band, :]))
                acc = acc * jnp.where(lo, 1.0 / den_e, 1.0 / den_o)
                o_ref[r0:r0 + CHUNK, 2 * LANES * h:2 * LANES * h + LANES] = acc[0:CHUNK]
                o_ref[r0:r0 + CHUNK, 2 * LANES * h + LANES:2 * LANES * (h + 1)] = acc[CHUNK:2 * CHUNK]

    if carry:
        for h in range(N_KV_HEADS):
            for ref in (ke_ref, ko_ref, ve_ref, vo_ref):
                ref[0, h, 0:WINDOW, :] = ref[0, h, seg:seg + WINDOW, :]

    o = o_ref[...]
    rs_a = _rms_scale(o)
    rs_c = lax.rsqrt(ssq_c * (1.0 / D_CONV) + EPS)
    ob = (o * gattn_ref[...]).astype(BF16)
    y = (rs_c * _dot(yc_ref[...], wout_ref[0:D_CONV, :])
         + rs_a * _dot(ob, wout_ref[D_CONV:D_CONV + ATTN_WIDTH, :]))
    out_ref[...] = x + y * _rms_scale(y) * gpost_ref[...]


def _mixer(x2d, tabs, kinit, vinit, cinit, gpre, win, convw, gconv, gattn, sink_tab, wout, gpost,
           *, n_batch, tm, n_seg, carry):
    rows = x2d.shape[0]
    nt = rows // (n_batch * tm)
    seg = tm // n_seg
    keep = min(WINDOW, seg)
    conv_cb = 256
    n_in = win.shape[1]
    row_blk = lambda b, i: (b * nt + i, 0)
    tab_blk = lambda b, i: (i, 0)
    seg_blk = lambda b, i: (b, 0, 0)
    kern = functools.partial(_mixer_kernel, tm=tm, n_seg=n_seg, carry=carry, keep=keep,
                             conv_cb=conv_cb)
    n_segs_total = n_batch * n_seg
    return pl.pallas_call(
        kern,
        grid=(n_batch, nt),
        in_specs=[
            pl.BlockSpec((tm, D_MODEL), row_blk),
            pl.BlockSpec((tm, LANES), tab_blk),
            pl.BlockSpec((tm, LANES), tab_blk),
            pl.BlockSpec((tm, LANES), tab_blk),
            pl.BlockSpec((n_seg, WINDOW, KV_WIDTH), seg_blk),
            pl.BlockSpec((n_seg, WINDOW, KV_WIDTH), seg_blk),
            pl.BlockSpec((n_seg, SUBLANES, D_CONV), seg_blk),
            _resident((1, D_MODEL)),
            _resident((D_MODEL, n_in)),
            _resident((CONV_WIDTH, D_CONV)),
            _resident((1, D_CONV)),
            _resident((1, ATTN_WIDTH)),
            _resident((N_KV_HEADS, 2 * CHUNK, LANES)),
            _resident((D_CONV + ATTN_WIDTH, D_MODEL)),
            _resident((1, D_MODEL)),
        ],
        out_specs=[
            pl.BlockSpec((tm, D_MODEL), row_blk),
            pl.BlockSpec((n_seg, keep, KV_WIDTH), seg_blk),
            pl.BlockSpec((n_seg, keep, KV_WIDTH), seg_blk),
            pl.BlockSpec((n_seg, SUBLANES, D_CONV), seg_blk),
        ],
        out_shape=[
            jax.ShapeDtypeStruct((rows, D_MODEL), F32),
            jax.ShapeDtypeStruct((n_segs_total, keep, KV_WIDTH), F32),
            jax.ShapeDtypeStruct((n_segs_total, keep, KV_WIDTH), F32),
            jax.ShapeDtypeStruct((n_segs_total, SUBLANES, D_CONV), F32),
        ],
        scratch_shapes=[
            pltpu.VMEM((tm, D_CONV), BF16),
            pltpu.VMEM((tm, ATTN_WIDTH), BF16),
            pltpu.VMEM((tm, ATTN_WIDTH), F32),
            pltpu.VMEM((n_seg, N_KV_HEADS, WINDOW + seg, LANES), BF16),
            pltpu.VMEM((n_seg, N_KV_HEADS, WINDOW + seg, LANES), BF16),
            pltpu.VMEM((n_seg, N_KV_HEADS, WINDOW + seg, LANES), BF16),
            pltpu.VMEM((n_seg, N_KV_HEADS, WINDOW + seg, LANES), BF16),
            pltpu.VMEM((SUBLANES + seg, conv_cb), F32),
        ],
        compiler_params=pltpu.CompilerParams(
            dimension_semantics=("arbitrary", "arbitrary"), vmem_limit_bytes=VMEM_LIMIT_BYTES),
        name="mixer",
    )(x2d, *tabs, kinit, vinit, cinit, gpre, win, convw, gconv, gattn, sink_tab, wout, gpost)


def _cross_kernel(x_ref, mk_ref, mv_ref, gpre_ref, wq_ref, wo_ref, gpost_ref, out_ref, o_ref,
                  *, tm, n_seg):
    seg = tm // n_seg
    x = x_ref[...]
    hb = (x * _rms_scale(x) * gpre_ref[...]).astype(BF16)
    for hd in range(N_MEM_HEADS):
        c0 = hd * MEM_HEAD_DIM
        q = _dot(hb, wq_ref[:, c0:c0 + MEM_HEAD_DIM]).astype(BF16)
        for s in range(n_seg):
            r0 = s * seg
            sc = _dot_nt(q[r0:r0 + seg], mk_ref[s, :, c0:c0 + MEM_HEAD_DIM]) * (MEM_HEAD_DIM ** -0.5)
            p = jnp.exp(sc - jnp.max(sc, axis=-1, keepdims=True))
            den = jnp.sum(p, axis=-1, keepdims=True)
            o = _dot(p.astype(BF16), mv_ref[s, :, c0:c0 + MEM_HEAD_DIM]) * (1.0 / den)
            o_ref[r0:r0 + seg, c0:c0 + MEM_HEAD_DIM] = o.astype(BF16)
    y = _dot(o_ref[...], wo_ref[...])
    out_ref[...] = x + y * _rms_scale(y) * gpost_ref[...]


def _cross(x2d, mk, mv, gpre, wq, wo, gpost, *, n_batch, tm, n_seg):
    rows = x2d.shape[0]
    nt = rows // (n_batch * tm)
    row_blk = lambda b, i: (b * nt + i, 0)
    if n_batch == 1:
        mem_blk = _resident((n_seg, N_MEM, D_MODEL))
    else:
        mem_blk = pl.BlockSpec((n_seg, N_MEM, D_MODEL), lambda b, i: (b, 0, 0))
    kern = functools.partial(_cross_kernel, tm=tm, n_seg=n_seg)
    return pl.pallas_call(
        kern,
        grid=(n_batch, nt),
        in_specs=[
            pl.BlockSpec((tm, D_MODEL), row_blk),
            mem_blk, mem_blk,
            _resident((1, D_MODEL)),
            _resident((D_MODEL, D_MODEL)),
            _resident((D_MODEL, D_MODEL)),
            _resident((1, D_MODEL)),
        ],
        out_specs=pl.BlockSpec((tm, D_MODEL), row_blk),
        out_shape=jax.ShapeDtypeStruct((rows, D_MODEL), F32),
        scratch_shapes=[pltpu.VMEM((tm, D_MODEL), BF16)],
        compiler_params=pltpu.CompilerParams(
            dimension_semantics=("arbitrary", "arbitrary"), vmem_limit_bytes=VMEM_LIMIT_BYTES),
        name="cross_attn",
    )(x2d, mk, mv, gpre, wq, wo, gpost)


def _ffn_kernel(x_ref, ginit_ref, gpre_ref, wg_ref, wu_ref, cw_ref, cb_ref, wd_ref, gpost_ref,
                out_ref, gstate_ref,
                h_ref, acc_ref, gc_ref, cs_ref, *, tm, n_seg, fc):
    i = pl.program_id(1)
    f = pl.program_id(2)
    nf = pl.num_programs(2)
    seg = tm // n_seg

    @pl.when(f == 0)
    def _():
        x = x_ref[...]
        h_ref[...] = (x * _rms_scale(x) * gpre_ref[...]).astype(BF16)

    @pl.when(i == 0)
    def _():
        gc_ref[f] = ginit_ref[...]

    hb = h_ref[...]
    g = _dot(hb, wg_ref[...])
    u = _dot(hb, wu_ref[...])
    acts = []
    for s in range(n_seg):
        r0 = s * seg
        cur = g[r0:r0 + seg]
        a = _causal_conv(cs_ref, gc_ref[f, s], cur, cw_ref, 0, fc) + cb_ref[...]
        acts.append((a * (1.0 / (1.0 + jnp.exp(-a))) * u[r0:r0 + seg]).astype(BF16))
        last = cur[seg - SUBLANES:seg]
        gc_ref[f, s] = last
        gstate_ref[s, f] = last
    act = acts[0] if n_seg == 1 else jnp.concatenate(acts, axis=0)
    contrib = _dot(act, wd_ref[...])

    @pl.when(f == 0)
    def _():
        acc_ref[...] = contrib

    @pl.when(f != 0)
    def _():
        acc_ref[...] += contrib

    @pl.when(f == nf - 1)
    def _():
        y = acc_ref[...]
        out_ref[...] = x_ref[...] + y * _rms_scale(y) * gpost_ref[...]


def _ffn(x2d, ginit, gpre, wg, wu, cw, cb, wd, gpost, *, n_batch, tm, n_seg, fc):
    rows = x2d.shape[0]
    nt = rows // (n_batch * tm)
    nf = D_FF // fc
    seg = tm // n_seg
    row_blk = lambda b, i, f: (b * nt + i, 0)
    st_blk = pl.BlockSpec((n_seg, SUBLANES, fc), lambda b, i, f: (b, 0, f))
    kern = functools.partial(_ffn_kernel, tm=tm, n_seg=n_seg, fc=fc)
    return pl.pallas_call(
        kern,
        grid=(n_batch, nt, nf),
        in_specs=[
            pl.BlockSpec((tm, D_MODEL), row_blk),
            st_blk,
            _resident((1, D_MODEL)),
            pl.BlockSpec((D_MODEL, fc), lambda b, i, f: (0, f)),
            pl.BlockSpec((D_MODEL, fc), lambda b, i, f: (0, f)),
            pl.BlockSpec((CONV_WIDTH, fc), lambda b, i, f: (0, f)),
            pl.BlockSpec((1, fc), lambda b, i, f: (0, f)),
            pl.BlockSpec((fc, D_MODEL), lambda b, i, f: (f, 0)),
            _resident((1, D_MODEL)),
        ],
        out_specs=[
            pl.BlockSpec((tm, D_MODEL), row_blk),
            pl.BlockSpec((n_seg, nf, SUBLANES, fc), lambda b, i, f: (b, 0, 0, 0)),
        ],
        out_shape=[
            jax.ShapeDtypeStruct((rows, D_MODEL), F32),
            jax.ShapeDtypeStruct((n_batch * n_seg, nf, SUBLANES, fc), F32),
        ],
        scratch_shapes=[
            pltpu.VMEM((tm, D_MODEL), BF16),
            pltpu.VMEM((tm, D_MODEL), F32),
            pltpu.VMEM((nf, n_seg, SUBLANES, fc), F32),
            pltpu.VMEM((SUBLANES + seg, fc), F32),
        ],
        compiler_params=pltpu.CompilerParams(
            dimension_semantics=("arbitrary", "arbitrary", "arbitrary"),
            vmem_limit_bytes=VMEM_LIMIT_BYTES),
        name="conv_ffn",
    )(x2d, ginit, gpre, wg, wu, cw, cb, wd, gpost)


def _rope_tables(pos):
    half = ROT_DIM // 2
    inv = ROPE_THETA ** (-jnp.arange(0, ROT_DIM, 2, dtype=F32) / ROT_DIM)
    ang = pos[:, None] * inv[None, :]
    cos, sin = jnp.cos(ang), jnp.sin(ang)
    t = pos.shape[0]
    zeros = jnp.zeros((t, half), F32)
    rest = HEAD_DIM - ROT_DIM
    c64 = jnp.concatenate([cos, cos, jnp.ones((t, rest), F32)], axis=1)
    s1_64 = jnp.concatenate([-sin, zeros, jnp.zeros((t, rest), F32)], axis=1)
    s2_64 = jnp.concatenate([zeros, sin, jnp.zeros((t, rest), F32)], axis=1)
    rep = LANES // HEAD_DIM
    return tuple(jnp.tile(a, (1, rep)) for a in (c64, s1_64, s2_64))


def _pad_state(state):
    return jnp.pad(state, ((0, 0), (SUBLANES - (CONV_WIDTH - 1), 0), (0, 0)))


def _layer(x2d, tabs, kinit, vinit, cinit, finit, mk, mv, w, *, mixer_cfg, cross_cfg, ffn_cfg,
           carry):
    tile = lambda cfg: dict(n_batch=cfg[0], tm=cfg[1], n_seg=cfg[2])
    x1, k_new, v_new, cstate = _mixer(
        x2d, tabs, kinit, vinit, cinit, w["g_mix_pre"], w["w_mix_in"], w["conv_mix_w"],
        w["g_grp_conv"], w["g_grp_attn"], w["sink_tab"], w["w_mix_out"], w["g_mix_post"],
        carry=carry, **tile(mixer_cfg))
    x2 = _cross(x1, mk, mv, w["g_x_pre"], w["w_xq"], w["w_xo"], w["g_x_post"], **tile(cross_cfg))
    x3, fstate = _ffn(x2, finit, w["g_ffn_pre"], w["w_gate"], w["w_up"], w["conv_ffn_w"],
                      w["conv_ffn_b"], w["w_down"], w["g_ffn_post"], fc=FFN_CHUNK,
                      **tile(ffn_cfg))
    fstate = jnp.swapaxes(fstate, 1, 2).reshape(fstate.shape[0], SUBLANES, D_FF)
    return x3, k_new, v_new, cstate, fstate


def kernel(x_prompt, x_sample, cache_mem_k, cache_mem_v, cache_swa_k, cache_swa_v,
           state_mix_conv, state_ffn_conv, mem_prompt,
           g_mix_pre, w_mix_in, conv_mix_w, g_grp_conv, g_grp_attn, attn_sinks,
           w_mix_out, g_mix_post, g_mem, w_xk, w_xv, g_x_pre, w_xq, w_xo, g_x_post,
           g_ffn_pre, w_gate, w_up, conv_ffn_w, conv_ffn_b, w_down, g_ffn_post):
    n_p, s_p, _ = x_prompt.shape
    n_s, s_s, _ = x_sample.shape
    depth = w_mix_in.shape[0]
    swa_len = cache_swa_k.shape[2]
    tabs_p = _rope_tables(jnp.arange(s_p, dtype=F32))
    tabs_s = tuple(jnp.tile(a, (n_s // 2, 1))
                   for a in _rope_tables(PAST_LEN + jnp.arange(s_s, dtype=F32)))

    yp = x_prompt.reshape(n_p * s_p, D_MODEL)
    ys = x_sample.reshape(n_s * s_s, D_MODEL)
    outs = [[] for _ in range(10)]
    for l in range(depth):
        row = lambda a: a[l][None, :]
        sinks = attn_sinks[l].reshape(N_KV_HEADS, 2, 1, 2, 1)
        w = {
            "g_mix_pre": row(g_mix_pre), "w_mix_in": w_mix_in[l].astype(BF16),
            "conv_mix_w": conv_mix_w[l], "g_grp_conv": row(g_grp_conv),
            "g_grp_attn": row(g_grp_attn),
            "sink_tab": jnp.broadcast_to(sinks, (N_KV_HEADS, 2, CHUNK, 2, HALF)).reshape(
                N_KV_HEADS, 2 * CHUNK, LANES),
            "w_mix_out": w_mix_out[l].astype(BF16), "g_mix_post": row(g_mix_post),
            "g_x_pre": row(g_x_pre), "w_xq": w_xq[l].astype(BF16), "w_xo": w_xo[l].astype(BF16),
            "g_x_post": row(g_x_post), "g_ffn_pre": row(g_ffn_pre),
            "w_gate": w_gate[l].astype(BF16), "w_up": w_up[l].astype(BF16),
            "conv_ffn_w": conv_ffn_w[l], "conv_ffn_b": row(conv_ffn_b),
            "w_down": w_down[l].astype(BF16), "g_ffn_post": row(g_ffn_post),
        }
        mk32, mv32, mkb, mvb = _memory_kv(
            mem_prompt.reshape(n_p * N_MEM, D_MODEL), row(g_mem),
            w_xk[l].astype(BF16), w_xv[l].astype(BF16))
        yp, kp, vp, cp, fp = _layer(
            yp, tabs_p,
            jnp.zeros((n_p, WINDOW, KV_WIDTH), F32), jnp.zeros((n_p, WINDOW, KV_WIDTH), F32),
            jnp.zeros((n_p, SUBLANES, D_CONV), F32), jnp.zeros((n_p, SUBLANES, D_FF), F32),
            mkb.reshape(n_p, N_MEM, D_MODEL), mvb.reshape(n_p, N_MEM, D_MODEL), w,
            mixer_cfg=(n_p, 256, 1), cross_cfg=(n_p, 512, 1), ffn_cfg=(n_p, 512, 1), carry=True)
        ys, ks, vs, cs, fs = _layer(
            ys, tabs_s,
            cache_swa_k[l].reshape(n_s, swa_len, KV_WIDTH),
            cache_swa_v[l].reshape(n_s, swa_len, KV_WIDTH),
            _pad_state(state_mix_conv[l]), _pad_state(state_ffn_conv[l]),
            cache_mem_k[l].reshape(n_s, N_MEM, D_MODEL).astype(BF16),
            cache_mem_v[l].reshape(n_s, N_MEM, D_MODEL).astype(BF16), w,
            mixer_cfg=(2, n_s * s_s // 2, n_s // 2), cross_cfg=(1, n_s * s_s, n_s),
            ffn_cfg=(1, n_s * s_s, n_s), carry=False)
        keep_p = min(WINDOW, s_p)
        tail = CONV_WIDTH - 1
        new_k = ks.reshape(n_s, s_s, N_KV_HEADS, HEAD_DIM)
        new_v = vs.reshape(n_s, s_s, N_KV_HEADS, HEAD_DIM)
        layer_out = (
            mk32.reshape(n_p, N_MEM, N_MEM_HEADS, MEM_HEAD_DIM),
            mv32.reshape(n_p, N_MEM, N_MEM_HEADS, MEM_HEAD_DIM),
            kp.reshape(n_p, keep_p, N_KV_HEADS, HEAD_DIM),
            vp.reshape(n_p, keep_p, N_KV_HEADS, HEAD_DIM),
            cp[:, SUBLANES - tail:], fp[:, SUBLANES - tail:],
            jnp.concatenate([cache_swa_k[l], new_k], axis=1)[:, -swa_len:],
            jnp.concatenate([cache_swa_v[l], new_v], axis=1)[:, -swa_len:],
            cs[:, SUBLANES - tail:], fs[:, SUBLANES - tail:],
        )
        for acc, o in zip(outs, layer_out):
            acc.append(o)

    return (yp.reshape(n_p, s_p, D_MODEL), ys.reshape(n_s, s_s, D_MODEL),
            *[jnp.stack(o) for o in outs])
```

```python
import functools

import jax
import jax.numpy as jnp
from jax import lax
from jax.experimental import pallas as pl
from jax.experimental.pallas import tpu as pltpu

D_MODEL = 2048
CHUNK = 64
WINDOW = 128
CONV_WIDTH = 3
HEAD_DIM = 64
N_Q_HEADS = 16
N_KV_HEADS = 4
ATTN_WIDTH = N_Q_HEADS * HEAD_DIM
KV_WIDTH = N_KV_HEADS * HEAD_DIM
D_CONV = D_MODEL // 2
ROT_DIM = HEAD_DIM // 4
ROPE_THETA = 500000.0
N_MEM = 256
N_MEM_HEADS = 4
MEM_HEAD_DIM = D_MODEL // N_MEM_HEADS
D_FF = 11 * D_MODEL // 4
PAST_LEN = 1024
EPS = 1e-6
NEG_INF = -1e30

BF16 = jnp.bfloat16
F32 = jnp.float32

LANES = 128
SUBLANES = 8
HALF = LANES // 2
BAND = WINDOW + CHUNK
VMEM_LIMIT_BYTES = 60 * 1024 * 1024
FFN_CHUNK = 512
ATTN_BLOCK = 128

_NT = (((1,), (1,)), ((), ()))


def _dot(a, b):
    return jnp.dot(a, b, preferred_element_type=F32)


def _dot_nt(a, b):
    return lax.dot_general(a, b, _NT, preferred_element_type=F32)


def _rms_scale(x):
    return lax.rsqrt(jnp.mean(x * x, axis=-1, keepdims=True) + EPS)


def _resident(shape):
    nd = len(shape)
    return pl.BlockSpec(shape, lambda *_: (0,) * nd, pipeline_mode=pl.Buffered(1))


def _causal_conv(cs_ref, prev8, cur, w_ref, c0, width):
    rows = cur.shape[0]
    cs_ref[0:SUBLANES, :] = prev8
    cs_ref[SUBLANES:SUBLANES + rows, :] = cur
    x1 = cs_ref[pl.ds(SUBLANES - 1, rows), :]
    x2 = cs_ref[pl.ds(SUBLANES - 2, rows), :]
    w0 = w_ref[0:1, c0:c0 + width]
    w1 = w_ref[1:2, c0:c0 + width]
    w2 = w_ref[2:3, c0:c0 + width]
    return w0 * x2 + w1 * x1 + w2 * cur


def _memkv_kernel(mem_ref, g_ref, wk_ref, wv_ref, k32_ref, v32_ref, kbf_ref, vbf_ref, hm_ref):
    @pl.when(pl.program_id(0) == 0)
    def _():
        m = mem_ref[...]
        hm_ref[...] = (m * _rms_scale(m) * g_ref[...]).astype(BF16)

    hm = hm_ref[...]
    k = _dot(hm, wk_ref[...])
    v = _dot(hm, wv_ref[...])
    k32_ref[...] = k
    v32_ref[...] = v
    kbf_ref[...] = k.astype(BF16)
    vbf_ref[...] = v.astype(BF16)


def _memory_kv(mem2d, g_mem, wk, wv):
    rows = mem2d.shape[0]
    nc = 512
    out_w = wk.shape[1]
    col = lambda j: (0, j)
    return pl.pallas_call(
        _memkv_kernel,
        grid=(out_w // nc,),
        in_specs=[
            _resident((rows, D_MODEL)),
            _resident((1, D_MODEL)),
            pl.BlockSpec((D_MODEL, nc), col),
            pl.BlockSpec((D_MODEL, nc), col),
        ],
        out_specs=[pl.BlockSpec((rows, nc), col)] * 4,
        out_shape=[
            jax.ShapeDtypeStruct((rows, out_w), F32),
            jax.ShapeDtypeStruct((rows, out_w), F32),
            jax.ShapeDtypeStruct((rows, out_w), BF16),
            jax.ShapeDtypeStruct((rows, out_w), BF16),
        ],
        scratch_shapes=[pltpu.VMEM((rows, D_MODEL), BF16)],
        compiler_params=pltpu.CompilerParams(
            dimension_semantics=("arbitrary",), vmem_limit_bytes=VMEM_LIMIT_BYTES),
        name="memory_kv",
    )(mem2d, g_mem, wk, wv)


def _mixer_kernel(x_ref, cos_ref, s1_ref, s2_ref, kinit_ref, vinit_ref, cinit_ref,
                  gpre_ref, win_ref, convw_ref, gconv_ref, gattn_ref, sink_ref,
                  mix_ref, kout_ref, vout_ref, cstate_ref,
                  yc_ref, q_ref, o_ref, ke_ref, ko_ref, ve_ref, vo_ref, cs_ref,
                  *, tm, n_seg, carry, keep, conv_cb, ab):
    i = pl.program_id(1)
    seg = tm // n_seg
    nblk = seg // ab
    nkeys = ab + WINDOW
    lo = lax.broadcasted_iota(jnp.int32, (1, LANES), 1) < HALF

    def rope(blk, cos_t, s1_t, s2_t):
        return (blk * cos_t + pltpu.roll(blk, LANES - ROT_DIM // 2, 1) * s1_t
                + pltpu.roll(blk, ROT_DIM // 2, 1) * s2_t)

    def store_kv(kf, vf, s, row0):
        rows = kf.shape[0]
        for hp in range(N_KV_HEADS // 2):
            for src, e_ref, o_ref_ in ((kf, ke_ref, ko_ref), (vf, ve_ref, vo_ref)):
                own = src[:, LANES * hp:LANES * (hp + 1)]
                swp = pltpu.roll(own, HALF, 1)
                for par in range(2):
                    h = 2 * hp + par
                    low, high = (own, swp) if par == 0 else (swp, own)
                    e_ref[s, h, row0:row0 + rows, :] = jnp.where(lo, low, 0.0).astype(BF16)
                    o_ref_[s, h, row0:row0 + rows, :] = jnp.where(lo, 0.0, high).astype(BF16)

    @pl.when(i == 0)
    def _():
        for s in range(n_seg):
            store_kv(kinit_ref[s], vinit_ref[s], s, 0)
            cstate_ref[s] = cinit_ref[s]

    x = x_ref[...]
    hb = (x * _rms_scale(x) * gpre_ref[...]).astype(BF16)

    ssq_c = jnp.zeros((tm, 1), F32)
    for cb in range(D_CONV // conv_cb):
        c0 = cb * conv_cb
        bg = _dot(hb, win_ref[:, c0:c0 + conv_cb])
        cg = _dot(hb, win_ref[:, D_CONV + c0:D_CONV + c0 + conv_cb])
        ug = _dot(hb, win_ref[:, 2 * D_CONV + c0:2 * D_CONV + c0 + conv_cb])
        cu = cg * ug
        ssq_parts = []
        for s in range(n_seg):
            r0 = s * seg
            cur = cu[r0:r0 + seg]
            conv = _causal_conv(cs_ref, cstate_ref[s, :, c0:c0 + conv_cb], cur,
                                convw_ref, c0, conv_cb)
            y = bg[r0:r0 + seg] * conv
            ssq_parts.append(jnp.sum(y * y, axis=-1, keepdims=True))
            yc_ref[r0:r0 + seg, c0:c0 + conv_cb] = y
            cstate_ref[s, :, c0:c0 + conv_cb] = cur[seg - SUBLANES:seg]
        ssq_c = ssq_c + (ssq_parts[0] if n_seg == 1 else jnp.concatenate(ssq_parts, axis=0))

    cos_t = cos_ref[...]
    s1_t = s1_ref[...]
    s2_t = s2_ref[...]
    q0 = 3 * D_CONV
    qf = _dot(hb, win_ref[:, q0:q0 + ATTN_WIDTH])
    for jb in range(ATTN_WIDTH // LANES):
        blk = rope(qf[:, LANES * jb:LANES * (jb + 1)], cos_t, s1_t, s2_t)
        q_ref[:, LANES * jb:LANES * (jb + 1)] = (blk * (HEAD_DIM ** -0.5)).astype(BF16)
    k0 = q0 + ATTN_WIDTH
    kv = _dot(hb, win_ref[:, k0:k0 + 2 * KV_WIDTH])
    kf = jnp.concatenate(
        [rope(kv[:, LANES * jb:LANES * (jb + 1)], cos_t, s1_t, s2_t)
         for jb in range(KV_WIDTH // LANES)], axis=1)
    vf = kv[:, KV_WIDTH:2 * KV_WIDTH]
    for s in range(n_seg):
        r0 = s * seg
        store_kv(kf[r0:r0 + seg], vf[r0:r0 + seg], s, WINDOW)
        kout_ref[s] = kf[r0 + seg - keep:r0 + seg]
        vout_ref[s] = vf[r0 + seg - keep:r0 + seg]

    key_pos = lax.broadcasted_iota(jnp.int32, (2 * ab, nkeys), 1)
    if ab > CHUNK:
        tok = lax.broadcasted_iota(jnp.int32, (2 * ab, nkeys), 0) % ab
        first_key = (tok // CHUNK) * CHUNK
        band_bias = jnp.where((key_pos >= first_key) & (key_pos < first_key + BAND), 0.0, NEG_INF)
    else:
        band_bias = None
    def attend_block(idx, _):
        s = idx // nblk
        j = idx % nblk
        rows = pl.ds(pl.multiple_of(idx * ab, ab), ab)
        band = pl.ds(pl.multiple_of(j * ab, ab), nkeys)
        bias = band_bias
        if carry:
            first_valid = WINDOW - (i * seg + j * ab)
            start_bias = jnp.where(key_pos >= first_valid, 0.0, NEG_INF)
            bias = start_bias if bias is None else bias + start_bias
        for h in range(N_KV_HEADS):
            qa = jnp.concatenate(
                [q_ref[rows, 2 * LANES * h:2 * LANES * h + LANES],
                 q_ref[rows, 2 * LANES * h + LANES:2 * LANES * (h + 1)]], axis=0)
            s_e = _dot_nt(qa, ke_ref[s, h, band, :])
            s_o = _dot_nt(qa, ko_ref[s, h, band, :])
            if bias is not None:
                s_e = s_e + bias
                s_o = s_o + bias
            sink_e = sink_ref[h, :, 0:1]
            sink_o = sink_ref[h, :, HALF:HALF + 1]
            m_e = jnp.maximum(jnp.max(s_e, axis=-1, keepdims=True), sink_e)
            m_o = jnp.maximum(jnp.max(s_o, axis=-1, keepdims=True), sink_o)
            p_e = jnp.exp(s_e - m_e)
            p_o = jnp.exp(s_o - m_o)
            den_e = jnp.sum(p_e, axis=-1, keepdims=True) + jnp.exp(sink_e - m_e)
            den_o = jnp.sum(p_o, axis=-1, keepdims=True) + jnp.exp(sink_o - m_o)
            acc = (_dot(p_e.astype(BF16), ve_ref[s, h, band, :])
                   + _dot(p_o.astype(BF16), vo_ref[s, h, band, :]))
            acc = acc * jnp.where(lo, 1.0 / den_e, 1.0 / den_o)
            o_ref[rows, 2 * LANES * h:2 * LANES * h + LANES] = acc[0:ab]
            o_ref[rows, 2 * LANES * h + LANES:2 * LANES * (h + 1)] = acc[ab:2 * ab]
        return 0

    lax.fori_loop(0, n_seg * nblk, attend_block, 0)

    if carry:
        for h in range(N_KV_HEADS):
            for ref in (ke_ref, ko_ref, ve_ref, vo_ref):
                ref[0, h, 0:WINDOW, :] = ref[0, h, seg:seg + WINDOW, :]

    rs_c = lax.rsqrt(ssq_c * (1.0 / D_CONV) + EPS)
    mix_ref[:, 0:D_CONV] = (yc_ref[...] * rs_c * gconv_ref[...]).astype(BF16)
    o = o_ref[...]
    mix_ref[:, D_CONV:D_CONV + ATTN_WIDTH] = (o * _rms_scale(o) * gattn_ref[...]).astype(BF16)


def _mixer(x2d, tabs, kinit, vinit, cinit, gpre, win, convw, gconv, gattn, sinks,
           *, n_batch, tm, n_seg, carry):
    rows = x2d.shape[0]
    nt = rows // (n_batch * tm)
    seg = tm // n_seg
    keep = min(WINDOW, seg)
    conv_cb = 256
    ab = min(ATTN_BLOCK, seg)
    n_in = win.shape[1]
    row_blk = lambda b, i: (b * nt + i, 0)
    tab_blk = lambda b, i: (i, 0)
    seg_blk = lambda b, i: (b, 0, 0)
    kern = functools.partial(_mixer_kernel, tm=tm, n_seg=n_seg, carry=carry, keep=keep,
                             conv_cb=conv_cb, ab=ab)
    n_segs_total = n_batch * n_seg
    sink_tab = jnp.broadcast_to(sinks.reshape(N_KV_HEADS, 2, 1, 2, 1),
                                (N_KV_HEADS, 2, ab, 2, HALF)).reshape(N_KV_HEADS, 2 * ab, LANES)
    return pl.pallas_call(
        kern,
        grid=(n_batch, nt),
        in_specs=[
            pl.BlockSpec((tm, D_MODEL), row_blk),
            pl.BlockSpec((tm, LANES), tab_blk),
            pl.BlockSpec((tm, LANES), tab_blk),
            pl.BlockSpec((tm, LANES), tab_blk),
            pl.BlockSpec((n_seg, WINDOW, KV_WIDTH), seg_blk),
            pl.BlockSpec((n_seg, WINDOW, KV_WIDTH), seg_blk),
            pl.BlockSpec((n_seg, SUBLANES, D_CONV), seg_blk),
            _resident((1, D_MODEL)),
            _resident((D_MODEL, n_in)),
            _resident((CONV_WIDTH, D_CONV)),
            _resident((1, D_CONV)),
            _resident((1, ATTN_WIDTH)),
            _resident((N_KV_HEADS, 2 * ab, LANES)),
        ],
        out_specs=[
            pl.BlockSpec((tm, D_CONV + ATTN_WIDTH), row_blk),
            pl.BlockSpec((n_seg, keep, KV_WIDTH), seg_blk),
            pl.BlockSpec((n_seg, keep, KV_WIDTH), seg_blk),
            pl.BlockSpec((n_seg, SUBLANES, D_CONV), seg_blk),
        ],
        out_shape=[
            jax.ShapeDtypeStruct((rows, D_CONV + ATTN_WIDTH), BF16),
            jax.ShapeDtypeStruct((n_segs_total, keep, KV_WIDTH), F32),
            jax.ShapeDtypeStruct((n_segs_total, keep, KV_WIDTH), F32),
            jax.ShapeDtypeStruct((n_segs_total, SUBLANES, D_CONV), F32),
        ],
        scratch_shapes=[
            pltpu.VMEM((tm, D_CONV), F32),
            pltpu.VMEM((tm, ATTN_WIDTH), BF16),
            pltpu.VMEM((tm, ATTN_WIDTH), F32),
            pltpu.VMEM((n_seg, N_KV_HEADS, WINDOW + seg, LANES), BF16),
            pltpu.VMEM((n_seg, N_KV_HEADS, WINDOW + seg, LANES), BF16),
            pltpu.VMEM((n_seg, N_KV_HEADS, WINDOW + seg, LANES), BF16),
            pltpu.VMEM((n_seg, N_KV_HEADS, WINDOW + seg, LANES), BF16),
            pltpu.VMEM((SUBLANES + seg, conv_cb), F32),
        ],
        compiler_params=pltpu.CompilerParams(
            dimension_semantics=("arbitrary", "arbitrary"), vmem_limit_bytes=VMEM_LIMIT_BYTES),
        name="mixer",
    )(x2d, *tabs, kinit, vinit, cinit, gpre, win, convw, gconv, gattn, sink_tab)


def _mix_out_kernel(x_ref, mix_ref, wout_ref, gpost_ref, out_ref):
    y = _dot(mix_ref[...], wout_ref[...])
    out_ref[...] = x_ref[...] + y * _rms_scale(y) * gpost_ref[...]


def _mix_out(x2d, mix, wout, gpost, *, tm):
    rows = x2d.shape[0]
    row_blk = lambda i: (i, 0)
    width = mix.shape[1]
    return pl.pallas_call(
        _mix_out_kernel,
        grid=(rows // tm,),
        in_specs=[
            pl.BlockSpec((tm, D_MODEL), row_blk),
            pl.BlockSpec((tm, width), row_blk),
            _resident((width, D_MODEL)),
            _resident((1, D_MODEL)),
        ],
        out_specs=pl.BlockSpec((tm, D_MODEL), row_blk),
        out_shape=jax.ShapeDtypeStruct((rows, D_MODEL), F32),
        compiler_params=pltpu.CompilerParams(
            dimension_semantics=("arbitrary",), vmem_limit_bytes=VMEM_LIMIT_BYTES),
        name="mix_out",
    )(x2d, mix, wout, gpost)


def _cross_kernel(x_ref, mk_ref, mv_ref, gpre_ref, wq_ref, wo_ref, gpost_ref, out_ref, o_ref,
                  *, tm, n_seg):
    seg = tm // n_seg
    x = x_ref[...]
    hb = (x * _rms_scale(x) * gpre_ref[...]).astype(BF16)
    for hd in range(N_MEM_HEADS):
        c0 = hd * MEM_HEAD_DIM
        q = _dot(hb, wq_ref[:, c0:c0 + MEM_HEAD_DIM]).astype(BF16)
        for s in range(n_seg):
            r0 = s * seg
            sc = _dot_nt(q[r0:r0 + seg], mk_ref[s, :, c0:c0 + MEM_HEAD_DIM]) * (MEM_HEAD_DIM ** -0.5)
            p = jnp.exp(sc - jnp.max(sc, axis=-1, keepdims=True))
            den = jnp.sum(p, axis=-1, keepdims=True)
            o = _dot(p.astype(BF16), mv_ref[s, :, c0:c0 + MEM_HEAD_DIM]) * (1.0 / den)
            o_ref[r0:r0 + seg, c0:c0 + MEM_HEAD_DIM] = o.astype(BF16)
    y = _dot(o_ref[...], wo_ref[...])
    out_ref[...] = x + y * _rms_scale(y) * gpost_ref[...]


def _cross(x2d, mk, mv, gpre, wq, wo, gpost, *, n_batch, tm, n_seg):
    rows = x2d.shape[0]
    nt = rows // (n_batch * tm)
    row_blk = lambda b, i: (b * nt + i, 0)
    if n_batch == 1:
        mem_blk = _resident((n_seg, N_MEM, D_MODEL))
    else:
        mem_blk = pl.BlockSpec((n_seg, N_MEM, D_MODEL), lambda b, i: (b, 0, 0))
    kern = functools.partial(_cross_kernel, tm=tm, n_seg=n_seg)
    return pl.pallas_call(
        kern,
        grid=(n_batch, nt),
        in_specs=[
            pl.BlockSpec((tm, D_MODEL), row_blk),
            mem_blk, mem_blk,
            _resident((1, D_MODEL)),
            _resident((D_MODEL, D_MODEL)),
            _resident((D_MODEL, D_MODEL)),
            _resident((1, D_MODEL)),
        ],
        out_specs=pl.BlockSpec((tm, D_MODEL), row_blk),
        out_shape=jax.ShapeDtypeStruct((rows, D_MODEL), F32),
        scratch_shapes=[pltpu.VMEM((tm, D_MODEL), BF16)],
        compiler_params=pltpu.CompilerParams(
            dimension_semantics=("arbitrary", "arbitrary"), vmem_limit_bytes=VMEM_LIMIT_BYTES),
        name="cross_attn",
    )(x2d, mk, mv, gpre, wq, wo, gpost)


def _ffn_kernel(x_ref, ginit_ref, gpre_ref, wg_ref, wu_ref, cw_ref, cb_ref, wd_ref, gpost_ref,
                out_ref, gstate_ref,
                h_ref, acc_ref, gc_ref, cs_ref, *, tm, n_seg, fc):
    i = pl.program_id(1)
    f = pl.program_id(2)
    nf = pl.num_programs(2)
    seg = tm // n_seg

    @pl.when(f == 0)
    def _():
        x = x_ref[...]
        h_ref[...] = (x * _rms_scale(x) * gpre_ref[...]).astype(BF16)
        acc_ref[...] = jnp.zeros_like(acc_ref)

    @pl.when(i == 0)
    def _():
        gc_ref[f] = ginit_ref[...]

    hb = h_ref[...]
    g = _dot(hb, wg_ref[...])
    u = _dot(hb, wu_ref[...])
    acts = []
    for s in range(n_seg):
        r0 = s * seg
        cur = g[r0:r0 + seg]
        a = _causal_conv(cs_ref, gc_ref[f, s], cur, cw_ref, 0, fc) + cb_ref[...]
        acts.append((a * (1.0 / (1.0 + jnp.exp(-a))) * u[r0:r0 + seg]).astype(BF16))
        last = cur[seg - SUBLANES:seg]
        gc_ref[f, s] = last
        gstate_ref[s, f] = last
    act = acts[0] if n_seg == 1 else jnp.concatenate(acts, axis=0)
    acc_ref[...] += _dot(act, wd_ref[...])

    @pl.when(f == nf - 1)
    def _():
        y = acc_ref[...]
        out_ref[...] = x_ref[...] + y * _rms_scale(y) * gpost_ref[...]


def _ffn(x2d, ginit, gpre, wg, wu, cw, cb, wd, gpost, *, n_batch, tm, n_seg, fc):
    rows = x2d.shape[0]
    nt = rows // (n_batch * tm)
    nf = D_FF // fc
    seg = tm // n_seg
    row_blk = lambda b, i, f: (b * nt + i, 0)
    st_blk = pl.BlockSpec((n_seg, SUBLANES, fc), lambda b, i, f: (b, 0, f))
    kern = functools.partial(_ffn_kernel, tm=tm, n_seg=n_seg, fc=fc)
    return pl.pallas_call(
        kern,
        grid=(n_batch, nt, nf),
        in_specs=[
            pl.BlockSpec((tm, D_MODEL), row_blk),
            st_blk,
            _resident((1, D_MODEL)),
            pl.BlockSpec((D_MODEL, fc), lambda b, i, f: (0, f)),
            pl.BlockSpec((D_MODEL, fc), lambda b, i, f: (0, f)),
            pl.BlockSpec((CONV_WIDTH, fc), lambda b, i, f: (0, f)),
            pl.BlockSpec((1, fc), lambda b, i, f: (0, f)),
            pl.BlockSpec((fc, D_MODEL), lambda b, i, f: (f, 0)),
            _resident((1, D_MODEL)),
        ],
        out_specs=[
            pl.BlockSpec((tm, D_MODEL), row_blk),
            pl.BlockSpec((n_seg, nf, SUBLANES, fc), lambda b, i, f: (b, 0, 0, 0)),
        ],
        out_shape=[
            jax.ShapeDtypeStruct((rows, D_MODEL), F32),
            jax.ShapeDtypeStruct((n_batch * n_seg, nf, SUBLANES, fc), F32),
        ],
        scratch_shapes=[
            pltpu.VMEM((tm, D_MODEL), BF16),
            pltpu.VMEM((tm, D_MODEL), F32),
            pltpu.VMEM((nf, n_seg, SUBLANES, fc), F32),
            pltpu.VMEM((SUBLANES + seg, fc), F32),
        ],
        compiler_params=pltpu.CompilerParams(
            dimension_semantics=("arbitrary", "arbitrary", "arbitrary"),
            vmem_limit_bytes=VMEM_LIMIT_BYTES),
        name="conv_ffn",
    )(x2d, ginit, gpre, wg, wu, cw, cb, wd, gpost)


def _rope_tables(pos):
    half = ROT_DIM // 2
    inv = ROPE_THETA ** (-jnp.arange(0, ROT_DIM, 2, dtype=F32) / ROT_DIM)
    ang = pos[:, None] * inv[None, :]
    cos, sin = jnp.cos(ang), jnp.sin(ang)
    t = pos.shape[0]
    zeros = jnp.zeros((t, half), F32)
    rest = HEAD_DIM - ROT_DIM
    c64 = jnp.concatenate([cos, cos, jnp.ones((t, rest), F32)], axis=1)
    s1_64 = jnp.concatenate([-sin, zeros, jnp.zeros((t, rest), F32)], axis=1)
    s2_64 = jnp.concatenate([zeros, sin, jnp.zeros((t, rest), F32)], axis=1)
    rep = LANES // HEAD_DIM
    return tuple(jnp.tile(a, (1, rep)) for a in (c64, s1_64, s2_64))


def _pad_state(state):
    return jnp.pad(state, ((0, 0), (SUBLANES - (CONV_WIDTH - 1), 0), (0, 0)))


def _layer(x2d, tabs, kinit, vinit, cinit, finit, mk, mv, w, *, mixer_cfg, mix_out_tm, cross_cfg,
           ffn_cfg, carry):
    tile = lambda cfg: dict(n_batch=cfg[0], tm=cfg[1], n_seg=cfg[2])
    mix, k_new, v_new, cstate = _mixer(
        x2d, tabs, kinit, vinit, cinit, w["g_mix_pre"], w["w_mix_in"], w["conv_mix_w"],
        w["g_grp_conv"], w["g_grp_attn"], w["sinks"], carry=carry, **tile(mixer_cfg))
    x1 = _mix_out(x2d, mix, w["w_mix_out"], w["g_mix_post"], tm=mix_out_tm)
    x2 = _cross(x1, mk, mv, w["g_x_pre"], w["w_xq"], w["w_xo"], w["g_x_post"], **tile(cross_cfg))
    x3, fstate = _ffn(x2, finit, w["g_ffn_pre"], w["w_gate"], w["w_up"], w["conv_ffn_w"],
                      w["conv_ffn_b"], w["w_down"], w["g_ffn_post"], fc=FFN_CHUNK,
                      **tile(ffn_cfg))
    fstate = jnp.swapaxes(fstate, 1, 2).reshape(fstate.shape[0], SUBLANES, D_FF)
    return x3, k_new, v_new, cstate, fstate


def kernel(x_prompt, x_sample, cache_mem_k, cache_mem_v, cache_swa_k, cache_swa_v,
           state_mix_conv, state_ffn_conv, mem_prompt,
           g_mix_pre, w_mix_in, conv_mix_w, g_grp_conv, g_grp_attn, attn_sinks,
           w_mix_out, g_mix_post, g_mem, w_xk, w_xv, g_x_pre, w_xq, w_xo, g_x_post,
           g_ffn_pre, w_gate, w_up, conv_ffn_w, conv_ffn_b, w_down, g_ffn_post):
    n_p, s_p, _ = x_prompt.shape
    n_s, s_s, _ = x_sample.shape
    depth = w_mix_in.shape[0]
    swa_len = cache_swa_k.shape[2]
    tabs_p = _rope_tables(jnp.arange(s_p, dtype=F32))
    tabs_s = tuple(jnp.tile(a, (n_s // 2, 1))
                   for a in _rope_tables(PAST_LEN + jnp.arange(s_s, dtype=F32)))

    yp = x_prompt.reshape(n_p * s_p, D_MODEL)
    ys = x_sample.reshape(n_s * s_s, D_MODEL)
    outs = [[] for _ in range(10)]
    for l in range(depth):
        row = lambda a: a[l][None, :]
        w = {
            "sinks": attn_sinks[l],
            "g_mix_pre": row(g_mix_pre), "w_mix_in": w_mix_in[l].astype(BF16),
            "conv_mix_w": conv_mix_w[l], "g_grp_conv": row(g_grp_conv),
            "g_grp_attn": row(g_grp_attn),
            "w_mix_out": w_mix_out[l].astype(BF16), "g_mix_post": row(g_mix_post),
            "g_x_pre": row(g_x_pre), "w_xq": w_xq[l].astype(BF16), "w_xo": w_xo[l].astype(BF16),
            "g_x_post": row(g_x_post), "g_ffn_pre": row(g_ffn_pre),
            "w_gate": w_gate[l].astype(BF16), "w_up": w_up[l].astype(BF16),
            "conv_ffn_w": conv_ffn_w[l], "conv_ffn_b": row(conv_ffn_b),
            "w_down": w_down[l].astype(BF16), "g_ffn_post": row(g_ffn_post),
        }
        mk32, mv32, mkb, mvb = _memory_kv(
            mem_prompt.reshape(n_p * N_MEM, D_MODEL), row(g_mem),
            w_xk[l].astype(BF16), w_xv[l].astype(BF16))
        yp, kp, vp, cp, fp = _layer(
            yp, tabs_p,
            jnp.zeros((n_p, WINDOW, KV_WIDTH), F32), jnp.zeros((n_p, WINDOW, KV_WIDTH), F32),
            jnp.zeros((n_p, SUBLANES, D_CONV), F32), jnp.zeros((n_p, SUBLANES, D_FF), F32),
            mkb.reshape(n_p, N_MEM, D_MODEL), mvb.reshape(n_p, N_MEM, D_MODEL), w,
            mixer_cfg=(n_p, 512, 1), mix_out_tm=1024, cross_cfg=(n_p, 512, 1),
            ffn_cfg=(n_p, 512, 1), carry=True)
        ys, ks, vs, cs, fs = _layer(
            ys, tabs_s,
            cache_swa_k[l].reshape(n_s, swa_len, KV_WIDTH),
            cache_swa_v[l].reshape(n_s, swa_len, KV_WIDTH),
            _pad_state(state_mix_conv[l]), _pad_state(state_ffn_conv[l]),
            cache_mem_k[l].reshape(n_s, N_MEM, D_MODEL).astype(BF16),
            cache_mem_v[l].reshape(n_s, N_MEM, D_MODEL).astype(BF16), w,
            mixer_cfg=(2, n_s * s_s // 2, n_s // 2), mix_out_tm=n_s * s_s,
            cross_cfg=(1, n_s * s_s, n_s),
            ffn_cfg=(1, n_s * s_s, n_s), carry=False)
        keep_p = min(WINDOW, s_p)
        tail = CONV_WIDTH - 1
        new_k = ks.reshape(n_s, s_s, N_KV_HEADS, HEAD_DIM)
        new_v = vs.reshape(n_s, s_s, N_KV_HEADS, HEAD_DIM)
        layer_out = (
            mk32.reshape(n_p, N_MEM, N_MEM_HEADS, MEM_HEAD_DIM),
            mv32.reshape(n_p, N_MEM, N_MEM_HEADS, MEM_HEAD_DIM),
            kp.reshape(n_p, keep_p, N_KV_HEADS, HEAD_DIM),
            vp.reshape(n_p, keep_p, N_KV_HEADS, HEAD_DIM),
            cp[:, SUBLANES - tail:], fp[:, SUBLANES - tail:],
            jnp.concatenate([cache_swa_k[l], new_k], axis=1)[:, -swa_len:],
            jnp.concatenate([cache_swa_v[l], new_v], axis=1)[:, -swa_len:],
            cs[:, SUBLANES - tail:], fs[:, SUBLANES - tail:],
        )
        for acc, o in zip(outs, layer_out):
            acc.append(o)

    return (yp.reshape(n_p, s_p, D_MODEL), ys.reshape(n_s, s_s, D_MODEL),
            *[jnp.stack(o) for o in outs])
```

```python
import functools

import jax
import jax.numpy as jnp
from jax import lax
from jax.experimental import pallas as pl
from jax.experimental.pallas import tpu as pltpu

D_MODEL = 2048
CHUNK = 64
WINDOW = 128
CONV_WIDTH = 3
HEAD_DIM = 64
N_Q_HEADS = 16
N_KV_HEADS = 4
ATTN_WIDTH = N_Q_HEADS * HEAD_DIM
KV_WIDTH = N_KV_HEADS * HEAD_DIM
D_CONV = D_MODEL // 2
ROT_DIM = HEAD_DIM // 4
ROPE_THETA = 500000.0
N_MEM = 256
N_MEM_HEADS = 4
MEM_HEAD_DIM = D_MODEL // N_MEM_HEADS
D_FF = 11 * D_MODEL // 4
PAST_LEN = 1024
EPS = 1e-6
NEG_INF = -1e30

BF16 = jnp.bfloat16
F32 = jnp.float32

LANES = 128
SUBLANES = 8
HALF = LANES // 2
BAND = WINDOW + CHUNK
VMEM_LIMIT_BYTES = 60 * 1024 * 1024
FFN_CHUNK = 512
ATTN_BLOCK = 128

_NT = (((1,), (1,)), ((), ()))


def _dot(a, b):
    return jnp.dot(a, b, preferred_element_type=F32)


def _dot_nt(a, b):
    return lax.dot_general(a, b, _NT, preferred_element_type=F32)


def _rms_scale(x):
    return lax.rsqrt(jnp.mean(x * x, axis=-1, keepdims=True) + EPS)


def _aligned(index, multiple):
    return index if isinstance(index, int) else pl.multiple_of(index, multiple)


def _resident(shape):
    nd = len(shape)
    return pl.BlockSpec(shape, lambda *_: (0,) * nd, pipeline_mode=pl.Buffered(1))


def _causal_conv(cs_ref, prev8, cur, w_ref, c0, width):
    rows = cur.shape[0]
    cs_ref[0:SUBLANES, :] = prev8
    cs_ref[SUBLANES:SUBLANES + rows, :] = cur
    x1 = cs_ref[pl.ds(SUBLANES - 1, rows), :]
    x2 = cs_ref[pl.ds(SUBLANES - 2, rows), :]
    w0 = w_ref[0:1, c0:c0 + width]
    w1 = w_ref[1:2, c0:c0 + width]
    w2 = w_ref[2:3, c0:c0 + width]
    return w0 * x2 + w1 * x1 + w2 * cur


def _memkv_kernel(mem_ref, g_ref, wk_ref, wv_ref, k32_ref, v32_ref, kbf_ref, vbf_ref, hm_ref):
    @pl.when(pl.program_id(0) == 0)
    def _():
        m = mem_ref[...]
        hm_ref[...] = (m * _rms_scale(m) * g_ref[...]).astype(BF16)

    hm = hm_ref[...]
    k = _dot(hm, wk_ref[...])
    v = _dot(hm, wv_ref[...])
    k32_ref[...] = k
    v32_ref[...] = v
    kbf_ref[...] = k.astype(BF16)
    vbf_ref[...] = v.astype(BF16)


def _memory_kv(mem2d, g_mem, wk, wv):
    rows = mem2d.shape[0]
    nc = 512
    out_w = wk.shape[1]
    col = lambda j: (0, j)
    return pl.pallas_call(
        _memkv_kernel,
        grid=(out_w // nc,),
        in_specs=[
            _resident((rows, D_MODEL)),
            _resident((1, D_MODEL)),
            pl.BlockSpec((D_MODEL, nc), col),
            pl.BlockSpec((D_MODEL, nc), col),
        ],
        out_specs=[pl.BlockSpec((rows, nc), col)] * 4,
        out_shape=[
            jax.ShapeDtypeStruct((rows, out_w), F32),
            jax.ShapeDtypeStruct((rows, out_w), F32),
            jax.ShapeDtypeStruct((rows, out_w), BF16),
            jax.ShapeDtypeStruct((rows, out_w), BF16),
        ],
        scratch_shapes=[pltpu.VMEM((rows, D_MODEL), BF16)],
        compiler_params=pltpu.CompilerParams(
            dimension_semantics=("arbitrary",), vmem_limit_bytes=VMEM_LIMIT_BYTES),
        name="memory_kv",
    )(mem2d, g_mem, wk, wv)


def _mixer_kernel(x_ref, cos_ref, s1_ref, s2_ref, kinit_ref, vinit_ref, cinit_ref,
                  gpre_ref, win_ref, convw_ref, gconv_ref, gattn_ref, sink_ref,
                  mix_ref, kout_ref, vout_ref, cstate_ref,
                  yc_ref, q_ref, o_ref, ke_ref, ko_ref, ve_ref, vo_ref, cs_ref,
                  *, tm, n_seg, carry, keep, conv_cb, ab):
    i = pl.program_id(1)
    seg = tm // n_seg
    nblk = seg // ab
    nkeys = ab + WINDOW
    lo = lax.broadcasted_iota(jnp.int32, (1, LANES), 1) < HALF

    def rope(blk, cos_t, s1_t, s2_t):
        return (blk * cos_t + pltpu.roll(blk, LANES - ROT_DIM // 2, 1) * s1_t
                + pltpu.roll(blk, ROT_DIM // 2, 1) * s2_t)

    def store_kv(kf, vf, s, row0):
        rows = kf.shape[0]
        for hp in range(N_KV_HEADS // 2):
            own = kf[:, LANES * hp:LANES * (hp + 1)]
            swp = pltpu.roll(own, HALF, 1)
            for par in range(2):
                h = 2 * hp + par
                low, high = (own, swp) if par == 0 else (swp, own)
                ke_ref[s, h, row0:row0 + rows, :] = jnp.where(lo, low, 0.0).astype(BF16)
                ko_ref[s, h, row0:row0 + rows, :] = jnp.where(lo, 0.0, high).astype(BF16)
        vt = vf.T.astype(BF16)
        zeros = jnp.zeros((HALF, rows), BF16)
        for h in range(N_KV_HEADS):
            vh = vt[HEAD_DIM * h:HEAD_DIM * (h + 1)]
            ve_ref[s, h, :, row0:row0 + rows] = jnp.concatenate([vh, zeros], axis=0)
            vo_ref[s, h, :, row0:row0 + rows] = jnp.concatenate([zeros, vh], axis=0)

    @pl.when(i == 0)
    def _():
        for s in range(n_seg):
            store_kv(kinit_ref[s], vinit_ref[s], s, 0)
            cstate_ref[s] = cinit_ref[s]

    x = x_ref[...]
    hb = (x * _rms_scale(x) * gpre_ref[...]).astype(BF16)

    ssq_c = jnp.zeros((tm, 1), F32)
    for cb in range(D_CONV // conv_cb):
        c0 = cb * conv_cb
        bg = _dot(hb, win_ref[:, c0:c0 + conv_cb])
        cg = _dot(hb, win_ref[:, D_CONV + c0:D_CONV + c0 + conv_cb])
        ug = _dot(hb, win_ref[:, 2 * D_CONV + c0:2 * D_CONV + c0 + conv_cb])
        cu = cg * ug
        ssq_parts = []
        for s in range(n_seg):
            r0 = s * seg
            cur = cu[r0:r0 + seg]
            conv = _causal_conv(cs_ref, cstate_ref[s, :, c0:c0 + conv_cb], cur,
                                convw_ref, c0, conv_cb)
            y = bg[r0:r0 + seg] * conv
            ssq_parts.append(jnp.sum(y * y, axis=-1, keepdims=True))
            yc_ref[r0:r0 + seg, c0:c0 + conv_cb] = y
            cstate_ref[s, :, c0:c0 + conv_cb] = cur[seg - SUBLANES:seg]
        ssq_c = ssq_c + (ssq_parts[0] if n_seg == 1 else jnp.concatenate(ssq_parts, axis=0))

    cos_t = cos_ref[...]
    s1_t = s1_ref[...]
    s2_t = s2_ref[...]
    q0 = 3 * D_CONV
    qf = _dot(hb, win_ref[:, q0:q0 + ATTN_WIDTH])
    for jb in range(ATTN_WIDTH // LANES):
        blk = rope(qf[:, LANES * jb:LANES * (jb + 1)], cos_t, s1_t, s2_t)
        q_ref[:, LANES * jb:LANES * (jb + 1)] = (blk * (HEAD_DIM ** -0.5)).astype(BF16)
    k0 = q0 + ATTN_WIDTH
    kv = _dot(hb, win_ref[:, k0:k0 + 2 * KV_WIDTH])
    kf = jnp.concatenate(
        [rope(kv[:, LANES * jb:LANES * (jb + 1)], cos_t, s1_t, s2_t)
         for jb in range(KV_WIDTH // LANES)], axis=1)
    vf = kv[:, KV_WIDTH:2 * KV_WIDTH]
    for s in range(n_seg):
        r0 = s * seg
        store_kv(kf[r0:r0 + seg], vf[r0:r0 + seg], s, WINDOW)
        kout_ref[s] = kf[r0 + seg - keep:r0 + seg]
        vout_ref[s] = vf[r0 + seg - keep:r0 + seg]

    key_pos = lax.broadcasted_iota(jnp.int32, (nkeys, 2 * ab), 0)
    if ab > CHUNK:
        tok = lax.broadcasted_iota(jnp.int32, (nkeys, 2 * ab), 1) % ab
        first_key = (tok // CHUNK) * CHUNK
        band_bias = jnp.where((key_pos >= first_key) & (key_pos < first_key + BAND), 0.0, NEG_INF)
    else:
        band_bias = None
    feat_lo = lax.broadcasted_iota(jnp.int32, (LANES, 1), 0) < HALF

    def attend_block(s, j):
        rows = pl.ds(_aligned((s * nblk + j) * ab, ab), ab)
        band = pl.ds(_aligned(j * ab, ab), nkeys)
        bias = band_bias
        if carry:
            first_valid = WINDOW - (i * seg + j * ab)
            start_bias = jnp.where(key_pos >= first_valid, 0.0, NEG_INF)
            bias = start_bias if bias is None else bias + start_bias
        heads = range(N_KV_HEADS)
        scores = []
        for h in heads:
            qa = jnp.concatenate(
                [q_ref[rows, 2 * LANES * h:2 * LANES * h + LANES],
                 q_ref[rows, 2 * LANES * h + LANES:2 * LANES * (h + 1)]], axis=0)
            st_e = _dot_nt(ke_ref[s, h, band, :], qa)
            st_o = _dot_nt(ko_ref[s, h, band, :], qa)
            if bias is not None:
                st_e = st_e + bias
                st_o = st_o + bias
            scores.append((st_e, st_o))
        stats = []
        for h in heads:
            sink_e = sink_ref[h, 0:1, :]
            sink_o = sink_ref[h, 1:2, :]
            m_e = jnp.maximum(jnp.max(scores[h][0], axis=0, keepdims=True), sink_e)
            m_o = jnp.maximum(jnp.max(scores[h][1], axis=0, keepdims=True), sink_o)
            stats.append((m_e, m_o, jnp.exp(sink_e - m_e), jnp.exp(sink_o - m_o)))
        probs = [(jnp.exp(scores[h][0] - stats[h][0]), jnp.exp(scores[h][1] - stats[h][1]))
                 for h in heads]
        for h in heads:
            p_e, p_o = probs[h]
            den_e = jnp.sum(p_e, axis=0, keepdims=True) + stats[h][2]
            den_o = jnp.sum(p_o, axis=0, keepdims=True) + stats[h][3]
            acc_t = (_dot(ve_ref[s, h, :, band], p_e.astype(BF16))
                     + _dot(vo_ref[s, h, :, band], p_o.astype(BF16)))
            acc = (acc_t * jnp.where(feat_lo, 1.0 / den_e, 1.0 / den_o)).T
            o_ref[rows, 2 * LANES * h:2 * LANES * h + LANES] = acc[0:ab]
            o_ref[rows, 2 * LANES * h + LANES:2 * LANES * (h + 1)] = acc[ab:2 * ab]

    if nblk == 1:
        for s in range(n_seg):
            attend_block(s, 0)
    else:
        def trip(idx, _):
            attend_block(idx // nblk, idx % nblk)
            return 0
        lax.fori_loop(0, n_seg * nblk, trip, 0)

    if carry:
        for h in range(N_KV_HEADS):
            for ref in (ke_ref, ko_ref):
                ref[0, h, 0:WINDOW, :] = ref[0, h, seg:seg + WINDOW, :]
            for ref in (ve_ref, vo_ref):
                ref[0, h, :, 0:WINDOW] = ref[0, h, :, seg:seg + WINDOW]

    rs_c = lax.rsqrt(ssq_c * (1.0 / D_CONV) + EPS)
    mix_ref[:, 0:D_CONV] = (yc_ref[...] * rs_c * gconv_ref[...]).astype(BF16)
    o = o_ref[...]
    mix_ref[:, D_CONV:D_CONV + ATTN_WIDTH] = (o * _rms_scale(o) * gattn_ref[...]).astype(BF16)


def _mixer(x2d, tabs, kinit, vinit, cinit, gpre, win, convw, gconv, gattn, sinks,
           *, n_batch, tm, n_seg, carry):
    rows = x2d.shape[0]
    nt = rows // (n_batch * tm)
    seg = tm // n_seg
    keep = min(WINDOW, seg)
    conv_cb = 256
    ab = min(ATTN_BLOCK, seg)
    n_in = win.shape[1]
    row_blk = lambda b, i: (b * nt + i, 0)
    tab_blk = lambda b, i: (i, 0)
    seg_blk = lambda b, i: (b, 0, 0)
    kern = functools.partial(_mixer_kernel, tm=tm, n_seg=n_seg, carry=carry, keep=keep,
                             conv_cb=conv_cb, ab=ab)
    n_segs_total = n_batch * n_seg
    by_parity = jnp.swapaxes(sinks.reshape(N_KV_HEADS, 2, 2), 1, 2)[:, :, :, None]
    sink_tab = jnp.broadcast_to(by_parity, (N_KV_HEADS, 2, 2, ab)).reshape(N_KV_HEADS, 2, 2 * ab)
    sink_tab = jnp.pad(sink_tab, ((0, 0), (0, SUBLANES - 2), (0, 0)))
    return pl.pallas_call(
        kern,
        grid=(n_batch, nt),
        in_specs=[
            pl.BlockSpec((tm, D_MODEL), row_blk),
            pl.BlockSpec((tm, LANES), tab_blk),
            pl.BlockSpec((tm, LANES), tab_blk),
            pl.BlockSpec((tm, LANES), tab_blk),
            pl.BlockSpec((n_seg, WINDOW, KV_WIDTH), seg_blk),
            pl.BlockSpec((n_seg, WINDOW, KV_WIDTH), seg_blk),
            pl.BlockSpec((n_seg, SUBLANES, D_CONV), seg_blk),
            _resident((1, D_MODEL)),
            _resident((D_MODEL, n_in)),
            _resident((CONV_WIDTH, D_CONV)),
            _resident((1, D_CONV)),
            _resident((1, ATTN_WIDTH)),
            _resident((N_KV_HEADS, SUBLANES, 2 * ab)),
        ],
        out_specs=[
            pl.BlockSpec((tm, D_CONV + ATTN_WIDTH), row_blk),
            pl.BlockSpec((n_seg, keep, KV_WIDTH), seg_blk),
            pl.BlockSpec((n_seg, keep, KV_WIDTH), seg_blk),
            pl.BlockSpec((n_seg, SUBLANES, D_CONV), seg_blk),
        ],
        out_shape=[
            jax.ShapeDtypeStruct((rows, D_CONV + ATTN_WIDTH), BF16),
            jax.ShapeDtypeStruct((n_segs_total, keep, KV_WIDTH), F32),
            jax.ShapeDtypeStruct((n_segs_total, keep, KV_WIDTH), F32),
            jax.ShapeDtypeStruct((n_segs_total, SUBLANES, D_CONV), F32),
        ],
        scratch_shapes=[
            pltpu.VMEM((tm, D_CONV), F32),
            pltpu.VMEM((tm, ATTN_WIDTH), BF16),
            pltpu.VMEM((tm, ATTN_WIDTH), F32),
            pltpu.VMEM((n_seg, N_KV_HEADS, WINDOW + seg, LANES), BF16),
            pltpu.VMEM((n_seg, N_KV_HEADS, WINDOW + seg, LANES), BF16),
            pltpu.VMEM((n_seg, N_KV_HEADS, LANES, WINDOW + seg), BF16),
            pltpu.VMEM((n_seg, N_KV_HEADS, LANES, WINDOW + seg), BF16),
            pltpu.VMEM((SUBLANES + seg, conv_cb), F32),
        ],
        compiler_params=pltpu.CompilerParams(
            dimension_semantics=("arbitrary", "arbitrary"), vmem_limit_bytes=VMEM_LIMIT_BYTES),
        name="mixer",
    )(x2d, *tabs, kinit, vinit, cinit, gpre, win, convw, gconv, gattn, sink_tab)


def _mix_out_kernel(x_ref, mix_ref, wout_ref, gpost_ref, out_ref):
    y = _dot(mix_ref[...], wout_ref[...])
    out_ref[...] = x_ref[...] + y * _rms_scale(y) * gpost_ref[...]


def _mix_out(x2d, mix, wout, gpost, *, tm):
    rows = x2d.shape[0]
    row_blk = lambda i: (i, 0)
    width = mix.shape[1]
    return pl.pallas_call(
        _mix_out_kernel,
        grid=(rows // tm,),
        in_specs=[
            pl.BlockSpec((tm, D_MODEL), row_blk),
            pl.BlockSpec((tm, width), row_blk),
            _resident((width, D_MODEL)),
            _resident((1, D_MODEL)),
        ],
        out_specs=pl.BlockSpec((tm, D_MODEL), row_blk),
        out_shape=jax.ShapeDtypeStruct((rows, D_MODEL), F32),
        compiler_params=pltpu.CompilerParams(
            dimension_semantics=("arbitrary",), vmem_limit_bytes=VMEM_LIMIT_BYTES),
        name="mix_out",
    )(x2d, mix, wout, gpost)


def _cross_kernel(x_ref, mk_ref, mv_ref, gpre_ref, wq_ref, wo_ref, gpost_ref, out_ref, o_ref,
                  *, tm, n_seg):
    seg = tm // n_seg
    x = x_ref[...]
    hb = (x * _rms_scale(x) * gpre_ref[...]).astype(BF16)
    for hd in range(N_MEM_HEADS):
        c0 = hd * MEM_HEAD_DIM
        q = _dot(hb, wq_ref[:, c0:c0 + MEM_HEAD_DIM]).astype(BF16)
        for s in range(n_seg):
            r0 = s * seg
            sc = _dot_nt(q[r0:r0 + seg], mk_ref[s, :, c0:c0 + MEM_HEAD_DIM]) * (MEM_HEAD_DIM ** -0.5)
            p = jnp.exp(sc - jnp.max(sc, axis=-1, keepdims=True))
            den = jnp.sum(p, axis=-1, keepdims=True)
            o = _dot(p.astype(BF16), mv_ref[s, :, c0:c0 + MEM_HEAD_DIM]) * (1.0 / den)
            o_ref[r0:r0 + seg, c0:c0 + MEM_HEAD_DIM] = o.astype(BF16)
    y = _dot(o_ref[...], wo_ref[...])
    out_ref[...] = x + y * _rms_scale(y) * gpost_ref[...]


def _cross(x2d, mk, mv, gpre, wq, wo, gpost, *, n_batch, tm, n_seg):
    rows = x2d.shape[0]
    nt = rows // (n_batch * tm)
    row_blk = lambda b, i: (b * nt + i, 0)
    if n_batch == 1:
        mem_blk = _resident((n_seg, N_MEM, D_MODEL))
    else:
        mem_blk = pl.BlockSpec((n_seg, N_MEM, D_MODEL), lambda b, i: (b, 0, 0))
    kern = functools.partial(_cross_kernel, tm=tm, n_seg=n_seg)
    return pl.pallas_call(
        kern,
        grid=(n_batch, nt),
        in_specs=[
            pl.BlockSpec((tm, D_MODEL), row_blk),
            mem_blk, mem_blk,
            _resident((1, D_MODEL)),
            _resident((D_MODEL, D_MODEL)),
            _resident((D_MODEL, D_MODEL)),
            _resident((1, D_MODEL)),
        ],
        out_specs=pl.BlockSpec((tm, D_MODEL), row_blk),
        out_shape=jax.ShapeDtypeStruct((rows, D_MODEL), F32),
        scratch_shapes=[pltpu.VMEM((tm, D_MODEL), BF16)],
        compiler_params=pltpu.CompilerParams(
            dimension_semantics=("arbitrary", "arbitrary"), vmem_limit_bytes=VMEM_LIMIT_BYTES),
        name="cross_attn",
    )(x2d, mk, mv, gpre, wq, wo, gpost)


def _ffn_kernel(x_ref, ginit_ref, gpre_ref, wg_ref, wu_ref, cw_ref, cb_ref, wd_ref, gpost_ref,
                out_ref, gstate_ref,
                h_ref, acc_ref, gc_ref, cs_ref, *, tm, n_seg, fc):
    i = pl.program_id(1)
    f = pl.program_id(2)
    nf = pl.num_programs(2)
    seg = tm // n_seg

    @pl.when(f == 0)
    def _():
        x = x_ref[...]
        h_ref[...] = (x * _rms_scale(x) * gpre_ref[...]).astype(BF16)
        acc_ref[...] = jnp.zeros_like(acc_ref)

    @pl.when(i == 0)
    def _():
        gc_ref[f] = ginit_ref[...]

    hb = h_ref[...]
    g = _dot(hb, wg_ref[...])
    u = _dot(hb, wu_ref[...])
    acts = []
    for s in range(n_seg):
        r0 = s * seg
        cur = g[r0:r0 + seg]
        a = _causal_conv(cs_ref, gc_ref[f, s], cur, cw_ref, 0, fc) + cb_ref[...]
        acts.append((a * (1.0 / (1.0 + jnp.exp(-a))) * u[r0:r0 + seg]).astype(BF16))
        last = cur[seg - SUBLANES:seg]
        gc_ref[f, s] = last
        gstate_ref[s, f] = last
    act = acts[0] if n_seg == 1 else jnp.concatenate(acts, axis=0)
    acc_ref[...] += _dot(act, wd_ref[...])

    @pl.when(f == nf - 1)
    def _():
        y = acc_ref[...]
        out_ref[...] = x_ref[...] + y * _rms_scale(y) * gpost_ref[...]


def _ffn(x2d, ginit, gpre, wg, wu, cw, cb, wd, gpost, *, n_batch, tm, n_seg, fc):
    rows = x2d.shape[0]
    nt = rows // (n_batch * tm)
    nf = D_FF // fc
    seg = tm // n_seg
    row_blk = lambda b, i, f: (b * nt + i, 0)
    st_blk = pl.BlockSpec((n_seg, SUBLANES, fc), lambda b, i, f: (b, 0, f))
    kern = functools.partial(_ffn_kernel, tm=tm, n_seg=n_seg, fc=fc)
    return pl.pallas_call(
        kern,
        grid=(n_batch, nt, nf),
        in_specs=[
            pl.BlockSpec((tm, D_MODEL), row_blk),
            st_blk,
            _resident((1, D_MODEL)),
            pl.BlockSpec((D_MODEL, fc), lambda b, i, f: (0, f)),
            pl.BlockSpec((D_MODEL, fc), lambda b, i, f: (0, f)),
            pl.BlockSpec((CONV_WIDTH, fc), lambda b, i, f: (0, f)),
            pl.BlockSpec((1, fc), lambda b, i, f: (0, f)),
            pl.BlockSpec((fc, D_MODEL), lambda b, i, f: (f, 0)),
            _resident((1, D_MODEL)),
        ],
        out_specs=[
            pl.BlockSpec((tm, D_MODEL), row_blk),
            pl.BlockSpec((n_seg, nf, SUBLANES, fc), lambda b, i, f: (b, 0, 0, 0)),
        ],
        out_shape=[
            jax.ShapeDtypeStruct((rows, D_MODEL), F32),
            jax.ShapeDtypeStruct((n_batch * n_seg, nf, SUBLANES, fc), F32),
        ],
        scratch_shapes=[
            pltpu.VMEM((tm, D_MODEL), BF16),
            pltpu.VMEM((tm, D_MODEL), F32),
            pltpu.VMEM((nf, n_seg, SUBLANES, fc), F32),
            pltpu.VMEM((SUBLANES + seg, fc), F32),
        ],
        compiler_params=pltpu.CompilerParams(
            dimension_semantics=("arbitrary", "arbitrary", "arbitrary"),
            vmem_limit_bytes=VMEM_LIMIT_BYTES),
        name="conv_ffn",
    )(x2d, ginit, gpre, wg, wu, cw, cb, wd, gpost)


def _rope_tables(pos):
    half = ROT_DIM // 2
    inv = ROPE_THETA ** (-jnp.arange(0, ROT_DIM, 2, dtype=F32) / ROT_DIM)
    ang = pos[:, None] * inv[None, :]
    cos, sin = jnp.cos(ang), jnp.sin(ang)
    t = pos.shape[0]
    zeros = jnp.zeros((t, half), F32)
    rest = HEAD_DIM - ROT_DIM
    c64 = jnp.concatenate([cos, cos, jnp.ones((t, rest), F32)], axis=1)
    s1_64 = jnp.concatenate([-sin, zeros, jnp.zeros((t, rest), F32)], axis=1)
    s2_64 = jnp.concatenate([zeros, sin, jnp.zeros((t, rest), F32)], axis=1)
    rep = LANES // HEAD_DIM
    return tuple(jnp.tile(a, (1, rep)) for a in (c64, s1_64, s2_64))


def _pad_state(state):
    return jnp.pad(state, ((0, 0), (SUBLANES - (CONV_WIDTH - 1), 0), (0, 0)))


def _layer(x2d, tabs, kinit, vinit, cinit, finit, mk, mv, w, *, mixer_cfg, mix_out_tm, cross_cfg,
           ffn_cfg, carry):
    tile = lambda cfg: dict(n_batch=cfg[0], tm=cfg[1], n_seg=cfg[2])
    mix, k_new, v_new, cstate = _mixer(
        x2d, tabs, kinit, vinit, cinit, w["g_mix_pre"], w["w_mix_in"], w["conv_mix_w"],
        w["g_grp_conv"], w["g_grp_attn"], w["sinks"], carry=carry, **tile(mixer_cfg))
    x1 = _mix_out(x2d, mix, w["w_mix_out"], w["g_mix_post"], tm=mix_out_tm)
    x2 = _cross(x1, mk, mv, w["g_x_pre"], w["w_xq"], w["w_xo"], w["g_x_post"], **tile(cross_cfg))
    x3, fstate = _ffn(x2, finit, w["g_ffn_pre"], w["w_gate"], w["w_up"], w["conv_ffn_w"],
                      w["conv_ffn_b"], w["w_down"], w["g_ffn_post"], fc=FFN_CHUNK,
                      **tile(ffn_cfg))
    fstate = jnp.swapaxes(fstate, 1, 2).reshape(fstate.shape[0], SUBLANES, D_FF)
    return x3, k_new, v_new, cstate, fstate


def kernel(x_prompt, x_sample, cache_mem_k, cache_mem_v, cache_swa_k, cache_swa_v,
           state_mix_conv, state_ffn_conv, mem_prompt,
           g_mix_pre, w_mix_in, conv_mix_w, g_grp_conv, g_grp_attn, attn_sinks,
           w_mix_out, g_mix_post, g_mem, w_xk, w_xv, g_x_pre, w_xq, w_xo, g_x_post,
           g_ffn_pre, w_gate, w_up, conv_ffn_w, conv_ffn_b, w_down, g_ffn_post):
    n_p, s_p, _ = x_prompt.shape
    n_s, s_s, _ = x_sample.shape
    depth = w_mix_in.shape[0]
    swa_len = cache_swa_k.shape[2]
    tabs_p = _rope_tables(jnp.arange(s_p, dtype=F32))
    tabs_s = tuple(jnp.tile(a, (n_s // 2, 1))
                   for a in _rope_tables(PAST_LEN + jnp.arange(s_s, dtype=F32)))

    yp = x_prompt.reshape(n_p * s_p, D_MODEL)
    ys = x_sample.reshape(n_s * s_s, D_MODEL)
    outs = [[] for _ in range(10)]
    for l in range(depth):
        row = lambda a: a[l][None, :]
        w = {
            "sinks": attn_sinks[l],
            "g_mix_pre": row(g_mix_pre), "w_mix_in": w_mix_in[l].astype(BF16),
            "conv_mix_w": conv_mix_w[l], "g_grp_conv": row(g_grp_conv),
            "g_grp_attn": row(g_grp_attn),
            "w_mix_out": w_mix_out[l].astype(BF16), "g_mix_post": row(g_mix_post),
            "g_x_pre": row(g_x_pre), "w_xq": w_xq[l].astype(BF16), "w_xo": w_xo[l].astype(BF16),
            "g_x_post": row(g_x_post), "g_ffn_pre": row(g_ffn_pre),
            "w_gate": w_gate[l].astype(BF16), "w_up": w_up[l].astype(BF16),
            "conv_ffn_w": conv_ffn_w[l], "conv_ffn_b": row(conv_ffn_b),
            "w_down": w_down[l].astype(BF16), "g_ffn_post": row(g_ffn_post),
        }
        mk32, mv32, mkb, mvb = _memory_kv(
            mem_prompt.reshape(n_p * N_MEM, D_MODEL), row(g_mem),
            w_xk[l].astype(BF16), w_xv[l].astype(BF16))
        yp, kp, vp, cp, fp = _layer(
            yp, tabs_p,
            jnp.zeros((n_p, WINDOW, KV_WIDTH), F32), jnp.zeros((n_p, WINDOW, KV_WIDTH), F32),
            jnp.zeros((n_p, SUBLANES, D_CONV), F32), jnp.zeros((n_p, SUBLANES, D_FF), F32),
            mkb.reshape(n_p, N_MEM, D_MODEL), mvb.reshape(n_p, N_MEM, D_MODEL), w,
            mixer_cfg=(n_p, 512, 1), mix_out_tm=1024, cross_cfg=(n_p, 512, 1),
            ffn_cfg=(n_p, 512, 1), carry=True)
        ys, ks, vs, cs, fs = _layer(
            ys, tabs_s,
            cache_swa_k[l].reshape(n_s, swa_len, KV_WIDTH),
            cache_swa_v[l].reshape(n_s, swa_len, KV_WIDTH),
            _pad_state(state_mix_conv[l]), _pad_state(state_ffn_conv[l]),
            cache_mem_k[l].reshape(n_s, N_MEM, D_MODEL).astype(BF16),
            cache_mem_v[l].reshape(n_s, N_MEM, D_MODEL).astype(BF16), w,
            mixer_cfg=(2, n_s * s_s // 2, n_s // 2), mix_out_tm=n_s * s_s,
            cross_cfg=(1, n_s * s_s, n_s),
            ffn_cfg=(1, n_s * s_s, n_s), carry=False)
        keep_p = min(WINDOW, s_p)
        tail = CONV_WIDTH - 1
        new_k = ks.reshape(n_s, s_s, N_KV_HEADS, HEAD_DIM)
        new_v = vs.reshape(n_s, s_s, N_KV_HEADS, HEAD_DIM)
        layer_out = (
            mk32.reshape(n_p, N_MEM, N_MEM_HEADS, MEM_HEAD_DIM),
            mv32.reshape(n_p, N_MEM, N_MEM_HEADS, MEM_HEAD_DIM),
            kp.reshape(n_p, keep_p, N_KV_HEADS, HEAD_DIM),
            vp.reshape(n_p, keep_p, N_KV_HEADS, HEAD_DIM),
            cp[:, SUBLANES - tail:], fp[:, SUBLANES - tail:],
            jnp.concatenate([cache_swa_k[l], new_k], axis=1)[:, -swa_len:],
            jnp.concatenate([cache_swa_v[l], new_v], axis=1)[:, -swa_len:],
            cs[:, SUBLANES - tail:], fs[:, SUBLANES - tail:],
        )
        for acc, o in zip(outs, layer_out):
            acc.append(o)

    return (yp.reshape(n_p, s_p, D_MODEL), ys.reshape(n_s, s_s, D_MODEL),
            *[jnp.stack(o) for o in outs])
```

```python
import functools

import jax
import jax.numpy as jnp
from jax import lax
from jax.experimental import pallas as pl
from jax.experimental.pallas import tpu as pltpu

D_MODEL = 2048
CHUNK = 64
WINDOW = 128
CONV_WIDTH = 3
HEAD_DIM = 64
N_Q_HEADS = 16
N_KV_HEADS = 4
ATTN_WIDTH = N_Q_HEADS * HEAD_DIM
KV_WIDTH = N_KV_HEADS * HEAD_DIM
D_CONV = D_MODEL // 2
ROT_DIM = HEAD_DIM // 4
ROPE_THETA = 500000.0
N_MEM = 256
N_MEM_HEADS = 4
MEM_HEAD_DIM = D_MODEL // N_MEM_HEADS
D_FF = 11 * D_MODEL // 4
PAST_LEN = 1024
EPS = 1e-6
NEG_INF = -1e30

BF16 = jnp.bfloat16
F32 = jnp.float32

LANES = 128
SUBLANES = 8
HALF = LANES // 2
BAND = WINDOW + CHUNK
VMEM_LIMIT_BYTES = 60 * 1024 * 1024
FFN_CHUNK = 512
ATTN_BLOCK = 128

_NT = (((1,), (1,)), ((), ()))


def _dot(a, b):
    return jnp.dot(a, b, preferred_element_type=F32)


def _dot_nt(a, b):
    return lax.dot_general(a, b, _NT, preferred_element_type=F32)


def _rms_scale(x):
    return lax.rsqrt(jnp.mean(x * x, axis=-1, keepdims=True) + EPS)


def _aligned(index, multiple):
    return index if isinstance(index, int) else pl.multiple_of(index, multiple)


def _resident(shape):
    nd = len(shape)
    return pl.BlockSpec(shape, lambda *_: (0,) * nd, pipeline_mode=pl.Buffered(1))


def _causal_conv(cs_ref, prev8, cur, w_ref, c0, width):
    rows = cur.shape[0]
    cs_ref[0:SUBLANES, :] = prev8
    cs_ref[SUBLANES:SUBLANES + rows, :] = cur
    x1 = cs_ref[pl.ds(SUBLANES - 1, rows), :]
    x2 = cs_ref[pl.ds(SUBLANES - 2, rows), :]
    w0 = w_ref[0:1, c0:c0 + width]
    w1 = w_ref[1:2, c0:c0 + width]
    w2 = w_ref[2:3, c0:c0 + width]
    return w0 * x2 + w1 * x1 + w2 * cur


def _memkv_kernel(mem_ref, g_ref, wk_ref, wv_ref, k32_ref, v32_ref, kbf_ref, vbf_ref, hm_ref):
    @pl.when(pl.program_id(0) == 0)
    def _():
        m = mem_ref[...]
        hm_ref[...] = (m * _rms_scale(m) * g_ref[...]).astype(BF16)

    hm = hm_ref[...]
    k = _dot(hm, wk_ref[...])
    v = _dot(hm, wv_ref[...])
    k32_ref[...] = k
    v32_ref[...] = v
    kbf_ref[...] = k.astype(BF16)
    vbf_ref[...] = v.astype(BF16)


def _memory_kv(mem2d, g_mem, wk, wv):
    rows = mem2d.shape[0]
    nc = 512
    out_w = wk.shape[1]
    col = lambda j: (0, j)
    return pl.pallas_call(
        _memkv_kernel,
        grid=(out_w // nc,),
        in_specs=[
            _resident((rows, D_MODEL)),
            _resident((1, D_MODEL)),
            pl.BlockSpec((D_MODEL, nc), col),
            pl.BlockSpec((D_MODEL, nc), col),
        ],
        out_specs=[pl.BlockSpec((rows, nc), col)] * 4,
        out_shape=[
            jax.ShapeDtypeStruct((rows, out_w), F32),
            jax.ShapeDtypeStruct((rows, out_w), F32),
            jax.ShapeDtypeStruct((rows, out_w), BF16),
            jax.ShapeDtypeStruct((rows, out_w), BF16),
        ],
        scratch_shapes=[pltpu.VMEM((rows, D_MODEL), BF16)],
        compiler_params=pltpu.CompilerParams(
            dimension_semantics=("arbitrary",), vmem_limit_bytes=VMEM_LIMIT_BYTES),
        name="memory_kv",
    )(mem2d, g_mem, wk, wv)


def _mixer_kernel(x_ref, cos_ref, s1_ref, s2_ref, kinit_ref, vinit_ref, cinit_ref,
                  gpre_ref, win_ref, convw_ref, gconv_ref, gattn_ref, sink_ref,
                  mix_ref, kout_ref, vout_ref, cstate_ref,
                  yc_ref, q_ref, o_ref, ke_ref, ko_ref, ve_ref, vo_ref, cs_ref,
                  *, tm, n_seg, carry, keep, conv_cb, ab):
    i = pl.program_id(1)
    seg = tm // n_seg
    nblk = seg // ab
    nkeys = ab + WINDOW
    lo = lax.broadcasted_iota(jnp.int32, (1, LANES), 1) < HALF

    def rope(blk, cos_t, s1_t, s2_t):
        return (blk * cos_t + pltpu.roll(blk, LANES - ROT_DIM // 2, 1) * s1_t
                + pltpu.roll(blk, ROT_DIM // 2, 1) * s2_t)

    def store_kv(kf, vf, s, row0):
        rows = kf.shape[0]
        for hp in range(N_KV_HEADS // 2):
            own = kf[:, LANES * hp:LANES * (hp + 1)]
            swp = pltpu.roll(own, HALF, 1)
            for par in range(2):
                h = 2 * hp + par
                low, high = (own, swp) if par == 0 else (swp, own)
                ke_ref[s, h, row0:row0 + rows, :] = jnp.where(lo, low, 0.0).astype(BF16)
                ko_ref[s, h, row0:row0 + rows, :] = jnp.where(lo, 0.0, high).astype(BF16)
        vt = vf.T.astype(BF16)
        zeros = jnp.zeros((HALF, rows), BF16)
        for h in range(N_KV_HEADS):
            vh = vt[HEAD_DIM * h:HEAD_DIM * (h + 1)]
            ve_ref[s, h, :, row0:row0 + rows] = jnp.concatenate([vh, zeros], axis=0)
            vo_ref[s, h, :, row0:row0 + rows] = jnp.concatenate([zeros, vh], axis=0)

    @pl.when(i == 0)
    def _():
        for s in range(n_seg):
            store_kv(kinit_ref[s], vinit_ref[s], s, 0)
            cstate_ref[s] = cinit_ref[s]

    x = x_ref[...]
    hb = (x * _rms_scale(x) * gpre_ref[...]).astype(BF16)

    ssq_c = jnp.zeros((tm, 1), F32)
    for cb in range(D_CONV // conv_cb):
        c0 = cb * conv_cb
        bg = _dot(hb, win_ref[:, c0:c0 + conv_cb])
        cg = _dot(hb, win_ref[:, D_CONV + c0:D_CONV + c0 + conv_cb])
        ug = _dot(hb, win_ref[:, 2 * D_CONV + c0:2 * D_CONV + c0 + conv_cb])
        cu = cg * ug
        ssq_parts = []
        for s in range(n_seg):
            r0 = s * seg
            cur = cu[r0:r0 + seg]
            conv = _causal_conv(cs_ref, cstate_ref[s, :, c0:c0 + conv_cb], cur,
                                convw_ref, c0, conv_cb)
            y = bg[r0:r0 + seg] * conv
            ssq_parts.append(jnp.sum(y * y, axis=-1, keepdims=True))
            yc_ref[r0:r0 + seg, c0:c0 + conv_cb] = y
            cstate_ref[s, :, c0:c0 + conv_cb] = cur[seg - SUBLANES:seg]
        ssq_c = ssq_c + (ssq_parts[0] if n_seg == 1 else jnp.concatenate(ssq_parts, axis=0))

    cos_t = cos_ref[...]
    s1_t = s1_ref[...]
    s2_t = s2_ref[...]
    q0 = 3 * D_CONV
    qf = _dot(hb, win_ref[:, q0:q0 + ATTN_WIDTH])
    for jb in range(ATTN_WIDTH // LANES):
        blk = rope(qf[:, LANES * jb:LANES * (jb + 1)], cos_t, s1_t, s2_t)
        q_ref[:, LANES * jb:LANES * (jb + 1)] = (blk * (HEAD_DIM ** -0.5)).astype(BF16)
    k0 = q0 + ATTN_WIDTH
    kv = _dot(hb, win_ref[:, k0:k0 + 2 * KV_WIDTH])
    kf = jnp.concatenate(
        [rope(kv[:, LANES * jb:LANES * (jb + 1)], cos_t, s1_t, s2_t)
         for jb in range(KV_WIDTH // LANES)], axis=1)
    vf = kv[:, KV_WIDTH:2 * KV_WIDTH]
    for s in range(n_seg):
        r0 = s * seg
        store_kv(kf[r0:r0 + seg], vf[r0:r0 + seg], s, WINDOW)
        kout_ref[s] = kf[r0 + seg - keep:r0 + seg]
        vout_ref[s] = vf[r0 + seg - keep:r0 + seg]

    key_pos = lax.broadcasted_iota(jnp.int32, (nkeys, 2 * ab), 0)
    if ab > CHUNK:
        tok = lax.broadcasted_iota(jnp.int32, (nkeys, 2 * ab), 1) % ab
        first_key = (tok // CHUNK) * CHUNK
        band_bias = jnp.where((key_pos >= first_key) & (key_pos < first_key + BAND), 0.0, NEG_INF)
    else:
        band_bias = None
    feat_lo = lax.broadcasted_iota(jnp.int32, (LANES, 1), 0) < HALF

    def attend_block(s, j):
        rows = pl.ds(_aligned((s * nblk + j) * ab, ab), ab)
        band = pl.ds(_aligned(j * ab, ab), nkeys)
        bias = band_bias
        if carry:
            first_valid = WINDOW - (i * seg + j * ab)
            start_bias = jnp.where(key_pos >= first_valid, 0.0, NEG_INF)
            bias = start_bias if bias is None else bias + start_bias
        heads = range(N_KV_HEADS)
        scores = []
        for h in heads:
            qa = jnp.concatenate(
                [q_ref[rows, 2 * LANES * h:2 * LANES * h + LANES],
                 q_ref[rows, 2 * LANES * h + LANES:2 * LANES * (h + 1)]], axis=0)
            st_e = _dot_nt(ke_ref[s, h, band, :], qa)
            st_o = _dot_nt(ko_ref[s, h, band, :], qa)
            if bias is not None:
                st_e = st_e + bias
                st_o = st_o + bias
            scores.append((st_e, st_o))
        stats = []
        for h in heads:
            sink_e = sink_ref[h, 0:1, :]
            sink_o = sink_ref[h, 1:2, :]
            m_e = jnp.maximum(jnp.max(scores[h][0], axis=0, keepdims=True), sink_e)
            m_o = jnp.maximum(jnp.max(scores[h][1], axis=0, keepdims=True), sink_o)
            stats.append((m_e, m_o, jnp.exp(sink_e - m_e), jnp.exp(sink_o - m_o)))
        probs = [(jnp.exp(scores[h][0] - stats[h][0]), jnp.exp(scores[h][1] - stats[h][1]))
                 for h in heads]
        for h in heads:
            p_e, p_o = probs[h]
            den_e = jnp.sum(p_e, axis=0, keepdims=True) + stats[h][2]
            den_o = jnp.sum(p_o, axis=0, keepdims=True) + stats[h][3]
            acc_t = (_dot(ve_ref[s, h, :, band], p_e.astype(BF16))
                     + _dot(vo_ref[s, h, :, band], p_o.astype(BF16)))
            acc = (acc_t * jnp.where(feat_lo, 1.0 / den_e, 1.0 / den_o)).T
            o_ref[rows, 2 * LANES * h:2 * LANES * h + LANES] = acc[0:ab]
            o_ref[rows, 2 * LANES * h + LANES:2 * LANES * (h + 1)] = acc[ab:2 * ab]

    if nblk == 1:
        for s in range(n_seg):
            attend_block(s, 0)
    else:
        def trip(idx, _):
            attend_block(idx // nblk, idx % nblk)
            return 0
        lax.fori_loop(0, n_seg * nblk, trip, 0)

    if carry:
        for h in range(N_KV_HEADS):
            for ref in (ke_ref, ko_ref):
                ref[0, h, 0:WINDOW, :] = ref[0, h, seg:seg + WINDOW, :]
            for ref in (ve_ref, vo_ref):
                ref[0, h, :, 0:WINDOW] = ref[0, h, :, seg:seg + WINDOW]

    rs_c = lax.rsqrt(ssq_c * (1.0 / D_CONV) + EPS)
    mix_ref[:, 0:D_CONV] = (yc_ref[...] * rs_c * gconv_ref[...]).astype(BF16)
    o = o_ref[...]
    mix_ref[:, D_CONV:D_CONV + ATTN_WIDTH] = (o * _rms_scale(o) * gattn_ref[...]).astype(BF16)


def _mixer(x2d, tabs, kinit, vinit, cinit, gpre, win, convw, gconv, gattn, sinks,
           *, n_batch, tm, n_seg, carry):
    rows = x2d.shape[0]
    nt = rows // (n_batch * tm)
    seg = tm // n_seg
    keep = min(WINDOW, seg)
    conv_cb = 256
    ab = min(ATTN_BLOCK, seg)
    n_in = win.shape[1]
    row_blk = lambda b, i: (b * nt + i, 0)
    tab_blk = lambda b, i: (i, 0)
    seg_blk = lambda b, i: (b, 0, 0)
    kern = functools.partial(_mixer_kernel, tm=tm, n_seg=n_seg, carry=carry, keep=keep,
                             conv_cb=conv_cb, ab=ab)
    n_segs_total = n_batch * n_seg
    by_parity = jnp.swapaxes(sinks.reshape(N_KV_HEADS, 2, 2), 1, 2)[:, :, :, None]
    sink_tab = jnp.broadcast_to(by_parity, (N_KV_HEADS, 2, 2, ab)).reshape(N_KV_HEADS, 2, 2 * ab)
    sink_tab = jnp.pad(sink_tab, ((0, 0), (0, SUBLANES - 2), (0, 0)))
    return pl.pallas_call(
        kern,
        grid=(n_batch, nt),
        in_specs=[
            pl.BlockSpec((tm, D_MODEL), row_blk),
            pl.BlockSpec((tm, LANES), tab_blk),
            pl.BlockSpec((tm, LANES), tab_blk),
            pl.BlockSpec((tm, LANES), tab_blk),
            pl.BlockSpec((n_seg, WINDOW, KV_WIDTH), seg_blk),
            pl.BlockSpec((n_seg, WINDOW, KV_WIDTH), seg_blk),
            pl.BlockSpec((n_seg, SUBLANES, D_CONV), seg_blk),
            _resident((1, D_MODEL)),
            _resident((D_MODEL, n_in)),
            _resident((CONV_WIDTH, D_CONV)),
            _resident((1, D_CONV)),
            _resident((1, ATTN_WIDTH)),
            _resident((N_KV_HEADS, SUBLANES, 2 * ab)),
        ],
        out_specs=[
            pl.BlockSpec((tm, D_CONV + ATTN_WIDTH), row_blk),
            pl.BlockSpec((n_seg, keep, KV_WIDTH), seg_blk),
            pl.BlockSpec((n_seg, keep, KV_WIDTH), seg_blk),
            pl.BlockSpec((n_seg, SUBLANES, D_CONV), seg_blk),
        ],
        out_shape=[
            jax.ShapeDtypeStruct((rows, D_CONV + ATTN_WIDTH), BF16),
            jax.ShapeDtypeStruct((n_segs_total, keep, KV_WIDTH), F32),
            jax.ShapeDtypeStruct((n_segs_total, keep, KV_WIDTH), F32),
            jax.ShapeDtypeStruct((n_segs_total, SUBLANES, D_CONV), F32),
        ],
        scratch_shapes=[
            pltpu.VMEM((tm, D_CONV), F32),
            pltpu.VMEM((tm, ATTN_WIDTH), BF16),
            pltpu.VMEM((tm, ATTN_WIDTH), F32),
            pltpu.VMEM((n_seg, N_KV_HEADS, WINDOW + seg, LANES), BF16),
            pltpu.VMEM((n_seg, N_KV_HEADS, WINDOW + seg, LANES), BF16),
            pltpu.VMEM((n_seg, N_KV_HEADS, LANES, WINDOW + seg), BF16),
            pltpu.VMEM((n_seg, N_KV_HEADS, LANES, WINDOW + seg), BF16),
            pltpu.VMEM((SUBLANES + seg, conv_cb), F32),
        ],
        compiler_params=pltpu.CompilerParams(
            dimension_semantics=("arbitrary", "arbitrary"), vmem_limit_bytes=VMEM_LIMIT_BYTES),
        name="mixer",
    )(x2d, *tabs, kinit, vinit, cinit, gpre, win, convw, gconv, gattn, sink_tab)


def _mix_out_kernel(x_ref, mix_ref, wout_ref, gpost_ref, out_ref):
    y = _dot(mix_ref[...], wout_ref[...])
    out_ref[...] = x_ref[...] + y * _rms_scale(y) * gpost_ref[...]


def _mix_out(x2d, mix, wout, gpost, *, tm):
    rows = x2d.shape[0]
    row_blk = lambda i: (i, 0)
    width = mix.shape[1]
    return pl.pallas_call(
        _mix_out_kernel,
        grid=(rows // tm,),
        in_specs=[
            pl.BlockSpec((tm, D_MODEL), row_blk),
            pl.BlockSpec((tm, width), row_blk),
            _resident((width, D_MODEL)),
            _resident((1, D_MODEL)),
        ],
        out_specs=pl.BlockSpec((tm, D_MODEL), row_blk),
        out_shape=jax.ShapeDtypeStruct((rows, D_MODEL), F32),
        compiler_params=pltpu.CompilerParams(
            dimension_semantics=("arbitrary",), vmem_limit_bytes=VMEM_LIMIT_BYTES),
        name="mix_out",
    )(x2d, mix, wout, gpost)


def _cross_kernel(x_ref, mk_ref, mv_ref, gpre_ref, wq_ref, wo_ref, gpost_ref, gnext_ref,
                  out_ref, hnext_ref, o_ref, *, tm, n_seg):
    seg = tm // n_seg
    x = x_ref[...]
    hb = (x * _rms_scale(x) * gpre_ref[...]).astype(BF16)
    for hd in range(N_MEM_HEADS):
        c0 = hd * MEM_HEAD_DIM
        q = _dot(hb, wq_ref[:, c0:c0 + MEM_HEAD_DIM]).astype(BF16)
        for s in range(n_seg):
            r0 = s * seg
            sc = _dot_nt(q[r0:r0 + seg], mk_ref[s, :, c0:c0 + MEM_HEAD_DIM]) * (MEM_HEAD_DIM ** -0.5)
            p = jnp.exp(sc - jnp.max(sc, axis=-1, keepdims=True))
            den = jnp.sum(p, axis=-1, keepdims=True)
            o = _dot(p.astype(BF16), mv_ref[s, :, c0:c0 + MEM_HEAD_DIM]) * (1.0 / den)
            o_ref[r0:r0 + seg, c0:c0 + MEM_HEAD_DIM] = o.astype(BF16)
    y = _dot(o_ref[...], wo_ref[...])
    x_out = x + y * _rms_scale(y) * gpost_ref[...]
    out_ref[...] = x_out
    hnext_ref[...] = (x_out * _rms_scale(x_out) * gnext_ref[...]).astype(BF16)


def _cross(x2d, mk, mv, gpre, wq, wo, gpost, gnext, *, n_batch, tm, n_seg):
    rows = x2d.shape[0]
    nt = rows // (n_batch * tm)
    row_blk = lambda b, i: (b * nt + i, 0)
    if n_batch == 1:
        mem_blk = _resident((n_seg, N_MEM, D_MODEL))
    else:
        mem_blk = pl.BlockSpec((n_seg, N_MEM, D_MODEL), lambda b, i: (b, 0, 0))
    kern = functools.partial(_cross_kernel, tm=tm, n_seg=n_seg)
    return pl.pallas_call(
        kern,
        grid=(n_batch, nt),
        in_specs=[
            pl.BlockSpec((tm, D_MODEL), row_blk),
            mem_blk, mem_blk,
            _resident((1, D_MODEL)),
            _resident((D_MODEL, D_MODEL)),
            _resident((D_MODEL, D_MODEL)),
            _resident((1, D_MODEL)),
            _resident((1, D_MODEL)),
        ],
        out_specs=[pl.BlockSpec((tm, D_MODEL), row_blk), pl.BlockSpec((tm, D_MODEL), row_blk)],
        out_shape=[jax.ShapeDtypeStruct((rows, D_MODEL), F32),
                   jax.ShapeDtypeStruct((rows, D_MODEL), BF16)],
        scratch_shapes=[pltpu.VMEM((tm, D_MODEL), BF16)],
        compiler_params=pltpu.CompilerParams(
            dimension_semantics=("arbitrary", "arbitrary"), vmem_limit_bytes=VMEM_LIMIT_BYTES),
        name="cross_attn",
    )(x2d, mk, mv, gpre, wq, wo, gpost, gnext)


def _ffn_kernel(h_ref, x_hbm, ginit_ref, wg_ref, wu_ref, cw_ref, cb_ref, wd_ref, gpost_ref,
                out_ref, gstate_ref,
                gc_ref, cs_ref, xbuf_ref, xsem, *, tm, n_seg, fc, nt, nf):
    b = pl.program_id(0)
    i = pl.program_id(1)
    f = pl.program_id(2)
    seg = tm // n_seg

    def residual_copy():
        row0 = pl.multiple_of((b * nt + i) * tm, tm)
        return pltpu.make_async_copy(x_hbm.at[pl.ds(row0, tm)], xbuf_ref, xsem)

    @pl.when(f == 0)
    def _():
        out_ref[...] = jnp.zeros_like(out_ref)

    @pl.when(f == max(nf - 3, 0))
    def _():
        residual_copy().start()

    @pl.when(i == 0)
    def _():
        gc_ref[f] = ginit_ref[...]

    hb = h_ref[...]
    g = _dot(hb, wg_ref[...])
    u = _dot(hb, wu_ref[...])
    acts = []
    for s in range(n_seg):
        r0 = s * seg
        cur = g[r0:r0 + seg]
        a = _causal_conv(cs_ref, gc_ref[f, s], cur, cw_ref, 0, fc) + cb_ref[...]
        acts.append((a * (1.0 / (1.0 + jnp.exp(-a))) * u[r0:r0 + seg]).astype(BF16))
        last = cur[seg - SUBLANES:seg]
        gc_ref[f, s] = last
        gstate_ref[s, f] = last
    act = acts[0] if n_seg == 1 else jnp.concatenate(acts, axis=0)
    out_ref[...] += _dot(act, wd_ref[...])

    @pl.when(f == nf - 1)
    def _():
        residual_copy().wait()
        y = out_ref[...]
        out_ref[...] = xbuf_ref[...] + y * _rms_scale(y) * gpost_ref[...]


def _ffn(h2d, x2d, ginit, wg, wu, cw, cb, wd, gpost, *, n_batch, tm, n_seg, fc):
    rows = x2d.shape[0]
    nt = rows // (n_batch * tm)
    nf = D_FF // fc
    seg = tm // n_seg
    row_blk = lambda b, i, f: (b * nt + i, 0)
    st_blk = pl.BlockSpec((n_seg, SUBLANES, fc), lambda b, i, f: (b, 0, f))
    kern = functools.partial(_ffn_kernel, tm=tm, n_seg=n_seg, fc=fc, nt=nt, nf=nf)
    return pl.pallas_call(
        kern,
        grid=(n_batch, nt, nf),
        in_specs=[
            pl.BlockSpec((tm, D_MODEL), row_blk),
            pl.BlockSpec(memory_space=pl.ANY),
            st_blk,
            pl.BlockSpec((D_MODEL, fc), lambda b, i, f: (0, f)),
            pl.BlockSpec((D_MODEL, fc), lambda b, i, f: (0, f)),
            pl.BlockSpec((CONV_WIDTH, fc), lambda b, i, f: (0, f)),
            pl.BlockSpec((1, fc), lambda b, i, f: (0, f)),
            pl.BlockSpec((fc, D_MODEL), lambda b, i, f: (f, 0)),
            _resident((1, D_MODEL)),
        ],
        out_specs=[
            pl.BlockSpec((tm, D_MODEL), row_blk),
            pl.BlockSpec((n_seg, nf, SUBLANES, fc), lambda b, i, f: (b, 0, 0, 0)),
        ],
        out_shape=[
            jax.ShapeDtypeStruct((rows, D_MODEL), F32),
            jax.ShapeDtypeStruct((n_batch * n_seg, nf, SUBLANES, fc), F32),
        ],
        scratch_shapes=[
            pltpu.VMEM((nf, n_seg, SUBLANES, fc), F32),
            pltpu.VMEM((SUBLANES + seg, fc), F32),
            pltpu.VMEM((tm, D_MODEL), F32),
            pltpu.SemaphoreType.DMA(()),
        ],
        compiler_params=pltpu.CompilerParams(
            dimension_semantics=("arbitrary", "arbitrary", "arbitrary"),
            vmem_limit_bytes=VMEM_LIMIT_BYTES),
        name="conv_ffn",
    )(h2d, x2d, ginit, wg, wu, cw, cb, wd, gpost)


def _rope_tables(pos):
    half = ROT_DIM // 2
    inv = ROPE_THETA ** (-jnp.arange(0, ROT_DIM, 2, dtype=F32) / ROT_DIM)
    ang = pos[:, None] * inv[None, :]
    cos, sin = jnp.cos(ang), jnp.sin(ang)
    t = pos.shape[0]
    zeros = jnp.zeros((t, half), F32)
    rest = HEAD_DIM - ROT_DIM
    c64 = jnp.concatenate([cos, cos, jnp.ones((t, rest), F32)], axis=1)
    s1_64 = jnp.concatenate([-sin, zeros, jnp.zeros((t, rest), F32)], axis=1)
    s2_64 = jnp.concatenate([zeros, sin, jnp.zeros((t, rest), F32)], axis=1)
    rep = LANES // HEAD_DIM
    return tuple(jnp.tile(a, (1, rep)) for a in (c64, s1_64, s2_64))


def _pad_state(state):
    return jnp.pad(state, ((0, 0), (SUBLANES - (CONV_WIDTH - 1), 0), (0, 0)))


def _layer(x2d, tabs, kinit, vinit, cinit, finit, mk, mv, w, *, mixer_cfg, mix_out_tm, cross_cfg,
           ffn_cfg, carry):
    tile = lambda cfg: dict(n_batch=cfg[0], tm=cfg[1], n_seg=cfg[2])
    mix, k_new, v_new, cstate = _mixer(
        x2d, tabs, kinit, vinit, cinit, w["g_mix_pre"], w["w_mix_in"], w["conv_mix_w"],
        w["g_grp_conv"], w["g_grp_attn"], w["sinks"], carry=carry, **tile(mixer_cfg))
    x1 = _mix_out(x2d, mix, w["w_mix_out"], w["g_mix_post"], tm=mix_out_tm)
    x2, h_ffn = _cross(x1, mk, mv, w["g_x_pre"], w["w_xq"], w["w_xo"], w["g_x_post"],
                       w["g_ffn_pre"], **tile(cross_cfg))
    x3, fstate = _ffn(h_ffn, x2, finit, w["w_gate"], w["w_up"], w["conv_ffn_w"],
                      w["conv_ffn_b"], w["w_down"], w["g_ffn_post"], fc=FFN_CHUNK,
                      **tile(ffn_cfg))
    fstate = jnp.swapaxes(fstate, 1, 2).reshape(fstate.shape[0], SUBLANES, D_FF)
    return x3, k_new, v_new, cstate, fstate


def kernel(x_prompt, x_sample, cache_mem_k, cache_mem_v, cache_swa_k, cache_swa_v,
           state_mix_conv, state_ffn_conv, mem_prompt,
           g_mix_pre, w_mix_in, conv_mix_w, g_grp_conv, g_grp_attn, attn_sinks,
           w_mix_out, g_mix_post, g_mem, w_xk, w_xv, g_x_pre, w_xq, w_xo, g_x_post,
           g_ffn_pre, w_gate, w_up, conv_ffn_w, conv_ffn_b, w_down, g_ffn_post):
    n_p, s_p, _ = x_prompt.shape
    n_s, s_s, _ = x_sample.shape
    depth = w_mix_in.shape[0]
    swa_len = cache_swa_k.shape[2]
    tabs_p = _rope_tables(jnp.arange(s_p, dtype=F32))
    tabs_s = tuple(jnp.tile(a, (n_s // 2, 1))
                   for a in _rope_tables(PAST_LEN + jnp.arange(s_s, dtype=F32)))

    yp = x_prompt.reshape(n_p * s_p, D_MODEL)
    ys = x_sample.reshape(n_s * s_s, D_MODEL)
    outs = [[] for _ in range(10)]
    for l in range(depth):
        row = lambda a: a[l][None, :]
        w = {
            "sinks": attn_sinks[l],
            "g_mix_pre": row(g_mix_pre), "w_mix_in": w_mix_in[l].astype(BF16),
            "conv_mix_w": conv_mix_w[l], "g_grp_conv": row(g_grp_conv),
            "g_grp_attn": row(g_grp_attn),
            "w_mix_out": w_mix_out[l].astype(BF16), "g_mix_post": row(g_mix_post),
            "g_x_pre": row(g_x_pre), "w_xq": w_xq[l].astype(BF16), "w_xo": w_xo[l].astype(BF16),
            "g_x_post": row(g_x_post), "g_ffn_pre": row(g_ffn_pre),
            "w_gate": w_gate[l].astype(BF16), "w_up": w_up[l].astype(BF16),
            "conv_ffn_w": conv_ffn_w[l], "conv_ffn_b": row(conv_ffn_b),
            "w_down": w_down[l].astype(BF16), "g_ffn_post": row(g_ffn_post),
        }
        mk32, mv32, mkb, mvb = _memory_kv(
            mem_prompt.reshape(n_p * N_MEM, D_MODEL), row(g_mem),
            w_xk[l].astype(BF16), w_xv[l].astype(BF16))
        yp, kp, vp, cp, fp = _layer(
            yp, tabs_p,
            jnp.zeros((n_p, WINDOW, KV_WIDTH), F32), jnp.zeros((n_p, WINDOW, KV_WIDTH), F32),
            jnp.zeros((n_p, SUBLANES, D_CONV), F32), jnp.zeros((n_p, SUBLANES, D_FF), F32),
            mkb.reshape(n_p, N_MEM, D_MODEL), mvb.reshape(n_p, N_MEM, D_MODEL), w,
            mixer_cfg=(n_p, 512, 1), mix_out_tm=1024, cross_cfg=(n_p, 512, 1),
            ffn_cfg=(n_p, 1024, 1), carry=True)
        ys, ks, vs, cs, fs = _layer(
            ys, tabs_s,
            cache_swa_k[l].reshape(n_s, swa_len, KV_WIDTH),
            cache_swa_v[l].reshape(n_s, swa_len, KV_WIDTH),
            _pad_state(state_mix_conv[l]), _pad_state(state_ffn_conv[l]),
            cache_mem_k[l].reshape(n_s, N_MEM, D_MODEL).astype(BF16),
            cache_mem_v[l].reshape(n_s, N_MEM, D_MODEL).astype(BF16), w,
            mixer_cfg=(2, n_s * s_s // 2, n_s // 2), mix_out_tm=n_s * s_s,
            cross_cfg=(1, n_s * s_s, n_s),
            ffn_cfg=(1, n_s * s_s, n_s), carry=False)
        keep_p = min(WINDOW, s_p)
        tail = CONV_WIDTH - 1
        new_k = ks.reshape(n_s, s_s, N_KV_HEADS, HEAD_DIM)
        new_v = vs.reshape(n_s, s_s, N_KV_HEADS, HEAD_DIM)
        layer_out = (
            mk32.reshape(n_p, N_MEM, N_MEM_HEADS, MEM_HEAD_DIM),
            mv32.reshape(n_p, N_MEM, N_MEM_HEADS, MEM_HEAD_DIM),
            kp.reshape(n_p, keep_p, N_KV_HEADS, HEAD_DIM),
            vp.reshape(n_p, keep_p, N_KV_HEADS, HEAD_DIM),
            cp[:, SUBLANES - tail:], fp[:, SUBLANES - tail:],
            jnp.concatenate([cache_swa_k[l], new_k], axis=1)[:, -swa_len:],
            jnp.concatenate([cache_swa_v[l], new_v], axis=1)[:, -swa_len:],
            cs[:, SUBLANES - tail:], fs[:, SUBLANES - tail:],
        )
        for acc, o in zip(outs, layer_out):
            acc.append(o)

    return (yp.reshape(n_p, s_p, D_MODEL), ys.reshape(n_s, s_s, D_MODEL),
            *[jnp.stack(o) for o in outs])
```

```python
import functools

import jax
import jax.numpy as jnp
from jax import lax
from jax.experimental import pallas as pl
from jax.experimental.pallas import tpu as pltpu

D_MODEL = 2048
CHUNK = 64
WINDOW = 128
CONV_WIDTH = 3
HEAD_DIM = 64
N_Q_HEADS = 16
N_KV_HEADS = 4
ATTN_WIDTH = N_Q_HEADS * HEAD_DIM
KV_WIDTH = N_KV_HEADS * HEAD_DIM
D_CONV = D_MODEL // 2
ROT_DIM = HEAD_DIM // 4
ROPE_THETA = 500000.0
N_MEM = 256
N_MEM_HEADS = 4
MEM_HEAD_DIM = D_MODEL // N_MEM_HEADS
D_FF = 11 * D_MODEL // 4
PAST_LEN = 1024
EPS = 1e-6
NEG_INF = -1e30

BF16 = jnp.bfloat16
F32 = jnp.float32

LANES = 128
SUBLANES = 8
HALF = LANES // 2
BAND = WINDOW + CHUNK
VMEM_LIMIT_BYTES = 60 * 1024 * 1024
FFN_CHUNK = 512
ATTN_BLOCK = 128

_NT = (((1,), (1,)), ((), ()))


def _dot(a, b):
    return jnp.dot(a, b, preferred_element_type=F32)


def _dot_nt(a, b):
    return lax.dot_general(a, b, _NT, preferred_element_type=F32)


def _rms_scale(x):
    return lax.rsqrt(jnp.mean(x * x, axis=-1, keepdims=True) + EPS)


def _aligned(index, multiple):
    return index if isinstance(index, int) else pl.multiple_of(index, multiple)


def _resident(shape):
    nd = len(shape)
    return pl.BlockSpec(shape, lambda *_: (0,) * nd, pipeline_mode=pl.Buffered(1))


def _causal_conv(cs_ref, prev8, cur, taps):
    rows = cur.shape[0]
    cs_ref[0:SUBLANES, :] = prev8
    cs_ref[SUBLANES:SUBLANES + rows, :] = cur
    x1 = cs_ref[pl.ds(SUBLANES - 1, rows), :]
    x2 = cs_ref[pl.ds(SUBLANES - 2, rows), :]
    return taps[0:1] * x2 + taps[1:2] * x1 + taps[2:3] * cur


def _memkv_kernel(mem_ref, g_ref, wk_ref, wv_ref, k32_ref, v32_ref, kbf_ref, vbf_ref, hm_ref):
    @pl.when(pl.program_id(0) == 0)
    def _():
        m = mem_ref[...]
        hm_ref[...] = (m * _rms_scale(m) * g_ref[...]).astype(BF16)

    hm = hm_ref[...]
    k = _dot(hm, wk_ref[...])
    v = _dot(hm, wv_ref[...])
    k32_ref[...] = k
    v32_ref[...] = v
    kbf_ref[...] = k.astype(BF16)
    vbf_ref[...] = v.astype(BF16)


def _memory_kv(mem2d, g_mem, wk, wv):
    rows = mem2d.shape[0]
    nc = 512
    out_w = wk.shape[1]
    col = lambda j: (0, j)
    return pl.pallas_call(
        _memkv_kernel,
        grid=(out_w // nc,),
        in_specs=[
            _resident((rows, D_MODEL)),
            _resident((1, D_MODEL)),
            pl.BlockSpec((D_MODEL, nc), col),
            pl.BlockSpec((D_MODEL, nc), col),
        ],
        out_specs=[pl.BlockSpec((rows, nc), col)] * 4,
        out_shape=[
            jax.ShapeDtypeStruct((rows, out_w), F32),
            jax.ShapeDtypeStruct((rows, out_w), F32),
            jax.ShapeDtypeStruct((rows, out_w), BF16),
            jax.ShapeDtypeStruct((rows, out_w), BF16),
        ],
        scratch_shapes=[pltpu.VMEM((rows, D_MODEL), BF16)],
        compiler_params=pltpu.CompilerParams(
            dimension_semantics=("arbitrary",), vmem_limit_bytes=VMEM_LIMIT_BYTES),
        name="memory_kv",
    )(mem2d, g_mem, wk, wv)


def _mixer_kernel(x_ref, cos_ref, s1_ref, s2_ref, kinit_ref, vinit_ref, cinit_ref,
                  gpre_ref, wconv_ref, wqkv_ref, convw_ref, gconv_ref, gattn_ref, sink_ref,
                  mix_ref, kout_ref, vout_ref, cstate_ref,
                  h_ref, yc_ref, q_ref, o_ref, ke_ref, ko_ref, ve_ref, vo_ref, cs_ref,
                  *, tm, n_seg, carry, keep, conv_cb, ab):
    i = pl.program_id(1)
    seg = tm // n_seg
    nblk = seg // ab
    nkeys = ab + WINDOW
    lo = lax.broadcasted_iota(jnp.int32, (1, LANES), 1) < HALF

    def rope(blk, cos_t, s1_t, s2_t):
        return (blk * cos_t + pltpu.roll(blk, LANES - ROT_DIM // 2, 1) * s1_t
                + pltpu.roll(blk, ROT_DIM // 2, 1) * s2_t)

    def store_kv(kf, vf, s, row0):
        rows = kf.shape[0]
        for hp in range(N_KV_HEADS // 2):
            own = kf[:, LANES * hp:LANES * (hp + 1)]
            swp = pltpu.roll(own, HALF, 1)
            for par in range(2):
                h = 2 * hp + par
                low, high = (own, swp) if par == 0 else (swp, own)
                ke_ref[s, h, row0:row0 + rows, :] = jnp.where(lo, low, 0.0).astype(BF16)
                ko_ref[s, h, row0:row0 + rows, :] = jnp.where(lo, 0.0, high).astype(BF16)
        vt = vf.T.astype(BF16)
        zeros = jnp.zeros((HALF, rows), BF16)
        for h in range(N_KV_HEADS):
            vh = vt[HEAD_DIM * h:HEAD_DIM * (h + 1)]
            ve_ref[s, h, :, row0:row0 + rows] = jnp.concatenate([vh, zeros], axis=0)
            vo_ref[s, h, :, row0:row0 + rows] = jnp.concatenate([zeros, vh], axis=0)

    @pl.when(i == 0)
    def _():
        for s in range(n_seg):
            store_kv(kinit_ref[s], vinit_ref[s], s, 0)
            cstate_ref[s] = cinit_ref[s]

    x = x_ref[...]
    h_ref[...] = (x * _rms_scale(x) * gpre_ref[...]).astype(BF16)
    hb = h_ref[...]

    def conv_block(cb):
        hbl = h_ref[...]
        bg = _dot(hbl, wconv_ref[cb, :, 0:conv_cb])
        cg = _dot(hbl, wconv_ref[cb, :, conv_cb:2 * conv_cb])
        ug = _dot(hbl, wconv_ref[cb, :, 2 * conv_cb:3 * conv_cb])
        cu = cg * ug
        taps = convw_ref[cb]
        ssq_parts = []
        for s in range(n_seg):
            r0 = s * seg
            cur = cu[r0:r0 + seg]
            y = bg[r0:r0 + seg] * _causal_conv(cs_ref, cstate_ref[s, cb], cur, taps)
            ssq_parts.append(jnp.sum(y * y, axis=-1, keepdims=True))
            yc_ref[cb, r0:r0 + seg, :] = y
            cstate_ref[s, cb] = cur[seg - SUBLANES:seg]
        return ssq_parts[0] if n_seg == 1 else jnp.concatenate(ssq_parts, axis=0)

    cos_t = cos_ref[...]
    s1_t = s1_ref[...]
    s2_t = s2_ref[...]
    qf = _dot(hb, wqkv_ref[:, 0:ATTN_WIDTH])
    for jb in range(ATTN_WIDTH // LANES):
        blk = rope(qf[:, LANES * jb:LANES * (jb + 1)], cos_t, s1_t, s2_t)
        q_ref[:, LANES * jb:LANES * (jb + 1)] = (blk * (HEAD_DIM ** -0.5)).astype(BF16)
    kv = _dot(hb, wqkv_ref[:, ATTN_WIDTH:ATTN_WIDTH + 2 * KV_WIDTH])
    kf = jnp.concatenate(
        [rope(kv[:, LANES * jb:LANES * (jb + 1)], cos_t, s1_t, s2_t)
         for jb in range(KV_WIDTH // LANES)], axis=1)
    vf = kv[:, KV_WIDTH:2 * KV_WIDTH]
    for s in range(n_seg):
        r0 = s * seg
        store_kv(kf[r0:r0 + seg], vf[r0:r0 + seg], s, WINDOW)
        kout_ref[s] = kf[r0 + seg - keep:r0 + seg]
        vout_ref[s] = vf[r0 + seg - keep:r0 + seg]

    key_pos = lax.broadcasted_iota(jnp.int32, (nkeys, 2 * ab), 0)
    if ab > CHUNK:
        tok = lax.broadcasted_iota(jnp.int32, (nkeys, 2 * ab), 1) % ab
        first_key = (tok // CHUNK) * CHUNK
        band_bias = jnp.where((key_pos >= first_key) & (key_pos < first_key + BAND), 0.0, NEG_INF)
    else:
        band_bias = None
    feat_lo = lax.broadcasted_iota(jnp.int32, (LANES, 1), 0) < HALF

    def attend_block(idx):
        s = 0 if n_seg == 1 else idx // nblk
        j = 0 if nblk == 1 else idx % nblk
        rows = pl.ds(pl.multiple_of(idx * ab, ab), ab)
        band = pl.ds(_aligned(j * ab, ab), nkeys)
        bias = band_bias
        if carry:
            first_valid = WINDOW - (i * seg + j * ab)
            start_bias = jnp.where(key_pos >= first_valid, 0.0, NEG_INF)
            bias = start_bias if bias is None else bias + start_bias
        heads = range(N_KV_HEADS)
        scores = []
        for h in heads:
            qa = jnp.concatenate(
                [q_ref[rows, 2 * LANES * h:2 * LANES * h + LANES],
                 q_ref[rows, 2 * LANES * h + LANES:2 * LANES * (h + 1)]], axis=0)
            st_e = _dot_nt(ke_ref[s, h, band, :], qa)
            st_o = _dot_nt(ko_ref[s, h, band, :], qa)
            if bias is not None:
                st_e = st_e + bias
                st_o = st_o + bias
            scores.append((st_e, st_o))
        stats = []
        for h in heads:
            sink_e = sink_ref[h, 0:1, :]
            sink_o = sink_ref[h, 1:2, :]
            m_e = jnp.maximum(jnp.max(scores[h][0], axis=0, keepdims=True), sink_e)
            m_o = jnp.maximum(jnp.max(scores[h][1], axis=0, keepdims=True), sink_o)
            stats.append((m_e, m_o, jnp.exp(sink_e - m_e), jnp.exp(sink_o - m_o)))
        probs = [(jnp.exp(scores[h][0] - stats[h][0]), jnp.exp(scores[h][1] - stats[h][1]))
                 for h in heads]
        for h in heads:
            p_e, p_o = probs[h]
            den_e = jnp.sum(p_e, axis=0, keepdims=True) + stats[h][2]
            den_o = jnp.sum(p_o, axis=0, keepdims=True) + stats[h][3]
            acc_t = (_dot(ve_ref[s, h, :, band], p_e.astype(BF16))
                     + _dot(vo_ref[s, h, :, band], p_o.astype(BF16)))
            acc = (acc_t * jnp.where(feat_lo, 1.0 / den_e, 1.0 / den_o)).T
            o_ref[rows, 2 * LANES * h:2 * LANES * h + LANES] = acc[0:ab]
            o_ref[rows, 2 * LANES * h + LANES:2 * LANES * (h + 1)] = acc[ab:2 * ab]

    def trip(idx, ssq):
        ssq = ssq + conv_block(idx)
        attend_block(idx)
        return ssq

    ssq_c = lax.fori_loop(0, n_seg * nblk, trip, jnp.zeros((tm, 1), F32))

    if carry:
        for h in range(N_KV_HEADS):
            for ref in (ke_ref, ko_ref):
                ref[0, h, 0:WINDOW, :] = ref[0, h, seg:seg + WINDOW, :]
            for ref in (ve_ref, vo_ref):
                ref[0, h, :, 0:WINDOW] = ref[0, h, :, seg:seg + WINDOW]

    rs_c = lax.rsqrt(ssq_c * (1.0 / D_CONV) + EPS)
    for cb in range(D_CONV // conv_cb):
        mix_ref[:, cb * conv_cb:(cb + 1) * conv_cb] = (
            yc_ref[cb] * rs_c * gconv_ref[cb]).astype(BF16)
    o = o_ref[...]
    mix_ref[:, D_CONV:D_CONV + ATTN_WIDTH] = (o * _rms_scale(o) * gattn_ref[...]).astype(BF16)


def _mixer(x2d, tabs, kinit, vinit, cinit, gpre, win, convw, gconv, gattn, sinks,
           *, n_batch, tm, n_seg, carry):
    rows = x2d.shape[0]
    nt = rows // (n_batch * tm)
    seg = tm // n_seg
    keep = min(WINDOW, seg)
    ab = min(ATTN_BLOCK, seg)
    n_trips = tm // ab
    conv_cb = D_CONV // n_trips
    row_blk = lambda b, i: (b * nt + i, 0)
    tab_blk = lambda b, i: (i, 0)
    seg_blk = lambda b, i: (b, 0, 0)
    cst_blk = pl.BlockSpec((n_seg, n_trips, SUBLANES, conv_cb), lambda b, i: (b, 0, 0, 0))
    kern = functools.partial(_mixer_kernel, tm=tm, n_seg=n_seg, carry=carry, keep=keep,
                             conv_cb=conv_cb, ab=ab)
    n_segs_total = n_batch * n_seg
    wconv = jnp.moveaxis(win[:, :3 * D_CONV].reshape(D_MODEL, 3, n_trips, conv_cb), 2, 0)
    wconv = wconv.reshape(n_trips, D_MODEL, 3 * conv_cb)
    wqkv = win[:, 3 * D_CONV:]
    taps = jnp.swapaxes(convw.reshape(CONV_WIDTH, n_trips, conv_cb), 0, 1)
    gconv_b = gconv.reshape(n_trips, 1, conv_cb)
    cinit_b = jnp.swapaxes(cinit.reshape(n_segs_total, SUBLANES, n_trips, conv_cb), 1, 2)
    by_parity = jnp.swapaxes(sinks.reshape(N_KV_HEADS, 2, 2), 1, 2)[:, :, :, None]
    sink_tab = jnp.broadcast_to(by_parity, (N_KV_HEADS, 2, 2, ab)).reshape(N_KV_HEADS, 2, 2 * ab)
    sink_tab = jnp.pad(sink_tab, ((0, 0), (0, SUBLANES - 2), (0, 0)))
    mix, k_tail, v_tail, cstate = pl.pallas_call(
        kern,
        grid=(n_batch, nt),
        in_specs=[
            pl.BlockSpec((tm, D_MODEL), row_blk),
            pl.BlockSpec((tm, LANES), tab_blk),
            pl.BlockSpec((tm, LANES), tab_blk),
            pl.BlockSpec((tm, LANES), tab_blk),
            pl.BlockSpec((n_seg, WINDOW, KV_WIDTH), seg_blk),
            pl.BlockSpec((n_seg, WINDOW, KV_WIDTH), seg_blk),
            cst_blk,
            _resident((1, D_MODEL)),
            _resident(wconv.shape),
            _resident(wqkv.shape),
            _resident(taps.shape),
            _resident(gconv_b.shape),
            _resident((1, ATTN_WIDTH)),
            _resident((N_KV_HEADS, SUBLANES, 2 * ab)),
        ],
        out_specs=[
            pl.BlockSpec((tm, D_CONV + ATTN_WIDTH), row_blk),
            pl.BlockSpec((n_seg, keep, KV_WIDTH), seg_blk),
            pl.BlockSpec((n_seg, keep, KV_WIDTH), seg_blk),
            cst_blk,
        ],
        out_shape=[
            jax.ShapeDtypeStruct((rows, D_CONV + ATTN_WIDTH), BF16),
            jax.ShapeDtypeStruct((n_segs_total, keep, KV_WIDTH), F32),
            jax.ShapeDtypeStruct((n_segs_total, keep, KV_WIDTH), F32),
            jax.ShapeDtypeStruct((n_segs_total, n_trips, SUBLANES, conv_cb), F32),
        ],
        scratch_shapes=[
            pltpu.VMEM((tm, D_MODEL), BF16),
            pltpu.VMEM((n_trips, tm, conv_cb), F32),
            pltpu.VMEM((tm, ATTN_WIDTH), BF16),
            pltpu.VMEM((tm, ATTN_WIDTH), F32),
            pltpu.VMEM((n_seg, N_KV_HEADS, WINDOW + seg, LANES), BF16),
            pltpu.VMEM((n_seg, N_KV_HEADS, WINDOW + seg, LANES), BF16),
            pltpu.VMEM((n_seg, N_KV_HEADS, LANES, WINDOW + seg), BF16),
            pltpu.VMEM((n_seg, N_KV_HEADS, LANES, WINDOW + seg), BF16),
            pltpu.VMEM((SUBLANES + seg, conv_cb), F32),
        ],
        compiler_params=pltpu.CompilerParams(
            dimension_semantics=("arbitrary", "arbitrary"), vmem_limit_bytes=VMEM_LIMIT_BYTES),
        name="mixer",
    )(x2d, *tabs, kinit, vinit, cinit_b, gpre, wconv, wqkv, taps, gconv_b, gattn, sink_tab)
    cstate = jnp.swapaxes(cstate, 1, 2).reshape(n_segs_total, SUBLANES, D_CONV)
    return mix, k_tail, v_tail, cstate


def _mix_out_kernel(x_ref, mix_ref, wout_ref, gpost_ref, out_ref):
    half = x_ref.shape[0] // 2
    for r0 in (0, half):
        y = _dot(mix_ref[r0:r0 + half, :], wout_ref[...])
        out_ref[r0:r0 + half, :] = x_ref[r0:r0 + half, :] + y * _rms_scale(y) * gpost_ref[...]


def _mix_out(x2d, mix, wout, gpost, *, tm):
    rows = x2d.shape[0]
    row_blk = lambda i: (i, 0)
    width = mix.shape[1]
    return pl.pallas_call(
        _mix_out_kernel,
        grid=(rows // tm,),
        in_specs=[
            pl.BlockSpec((tm, D_MODEL), row_blk),
            pl.BlockSpec((tm, width), row_blk),
            _resident((width, D_MODEL)),
            _resident((1, D_MODEL)),
        ],
        out_specs=pl.BlockSpec((tm, D_MODEL), row_blk),
        out_shape=jax.ShapeDtypeStruct((rows, D_MODEL), F32),
        compiler_params=pltpu.CompilerParams(
            dimension_semantics=("arbitrary",), vmem_limit_bytes=VMEM_LIMIT_BYTES),
        name="mix_out",
    )(x2d, mix, wout, gpost)


def _cross_kernel(x_ref, mk_ref, mv_ref, gpre_ref, wq_ref, wo_ref, gpost_ref, gnext_ref,
                  out_ref, hnext_ref, o_ref, *, tm, n_seg):
    seg = tm // n_seg
    x = x_ref[...]
    hb = (x * _rms_scale(x) * gpre_ref[...]).astype(BF16)
    for hd in range(N_MEM_HEADS):
        c0 = hd * MEM_HEAD_DIM
        q = _dot(hb, wq_ref[:, c0:c0 + MEM_HEAD_DIM]).astype(BF16)
        for s in range(n_seg):
            r0 = s * seg
            sc = _dot_nt(q[r0:r0 + seg], mk_ref[s, :, c0:c0 + MEM_HEAD_DIM]) * (MEM_HEAD_DIM ** -0.5)
            p = jnp.exp(sc - jnp.max(sc, axis=-1, keepdims=True))
            den = jnp.sum(p, axis=-1, keepdims=True)
            o = _dot(p.astype(BF16), mv_ref[s, :, c0:c0 + MEM_HEAD_DIM]) * (1.0 / den)
            o_ref[r0:r0 + seg, c0:c0 + MEM_HEAD_DIM] = o.astype(BF16)
    half = tm // 2
    for r0 in (0, half):
        y = _dot(o_ref[r0:r0 + half, :], wo_ref[...])
        x_out = x[r0:r0 + half] + y * _rms_scale(y) * gpost_ref[...]
        out_ref[r0:r0 + half, :] = x_out
        hnext_ref[r0:r0 + half, :] = (x_out * _rms_scale(x_out) * gnext_ref[...]).astype(BF16)


def _cross(x2d, mk, mv, gpre, wq, wo, gpost, gnext, *, n_batch, tm, n_seg):
    rows = x2d.shape[0]
    nt = rows // (n_batch * tm)
    row_blk = lambda b, i: (b * nt + i, 0)
    if n_batch == 1:
        mem_blk = _resident((n_seg, N_MEM, D_MODEL))
    else:
        mem_blk = pl.BlockSpec((n_seg, N_MEM, D_MODEL), lambda b, i: (b, 0, 0))
    kern = functools.partial(_cross_kernel, tm=tm, n_seg=n_seg)
    return pl.pallas_call(
        kern,
        grid=(n_batch, nt),
        in_specs=[
            pl.BlockSpec((tm, D_MODEL), row_blk),
            mem_blk, mem_blk,
            _resident((1, D_MODEL)),
            _resident((D_MODEL, D_MODEL)),
            _resident((D_MODEL, D_MODEL)),
            _resident((1, D_MODEL)),
            _resident((1, D_MODEL)),
        ],
        out_specs=[pl.BlockSpec((tm, D_MODEL), row_blk), pl.BlockSpec((tm, D_MODEL), row_blk)],
        out_shape=[jax.ShapeDtypeStruct((rows, D_MODEL), F32),
                   jax.ShapeDtypeStruct((rows, D_MODEL), BF16)],
        scratch_shapes=[pltpu.VMEM((tm, D_MODEL), BF16)],
        compiler_params=pltpu.CompilerParams(
            dimension_semantics=("arbitrary", "arbitrary"), vmem_limit_bytes=VMEM_LIMIT_BYTES),
        name="cross_attn",
    )(x2d, mk, mv, gpre, wq, wo, gpost, gnext)


def _ffn_kernel(h_ref, x_hbm, ginit_ref, wg_ref, wu_ref, cw_ref, cb_ref, wd_ref, gpost_ref,
                out_ref, gstate_ref,
                gc_ref, cs_ref, xbuf_ref, xsem, *, tm, n_seg, fc, nt, nf):
    b = pl.program_id(0)
    i = pl.program_id(1)
    f = pl.program_id(2)
    seg = tm // n_seg

    def residual_copy():
        row0 = pl.multiple_of((b * nt + i) * tm, tm)
        return pltpu.make_async_copy(x_hbm.at[pl.ds(row0, tm)], xbuf_ref, xsem)

    @pl.when(f == 0)
    def _():
        out_ref[...] = jnp.zeros_like(out_ref)

    @pl.when(f == max(nf - 3, 0))
    def _():
        residual_copy().start()

    @pl.when(i == 0)
    def _():
        gc_ref[f] = ginit_ref[...]

    hb = h_ref[...]
    g = _dot(hb, wg_ref[...])
    u = _dot(hb, wu_ref[...])
    acts = []
    for s in range(n_seg):
        r0 = s * seg
        cur = g[r0:r0 + seg]
        a = _causal_conv(cs_ref, gc_ref[f, s], cur, cw_ref[...]) + cb_ref[...]
        acts.append((a * (1.0 / (1.0 + jnp.exp(-a))) * u[r0:r0 + seg]).astype(BF16))
        last = cur[seg - SUBLANES:seg]
        gc_ref[f, s] = last
        gstate_ref[s, f] = last
    act = acts[0] if n_seg == 1 else jnp.concatenate(acts, axis=0)
    out_ref[...] += _dot(act, wd_ref[...])

    @pl.when(f == nf - 1)
    def _():
        residual_copy().wait()
        y = out_ref[...]
        out_ref[...] = xbuf_ref[...] + y * _rms_scale(y) * gpost_ref[...]


def _ffn(h2d, x2d, ginit, wg, wu, cw, cb, wd, gpost, *, n_batch, tm, n_seg, fc):
    rows = x2d.shape[0]
    nt = rows // (n_batch * tm)
    nf = D_FF // fc
    seg = tm // n_seg
    row_blk = lambda b, i, f: (b * nt + i, 0)
    st_blk = pl.BlockSpec((n_seg, SUBLANES, fc), lambda b, i, f: (b, 0, f))
    kern = functools.partial(_ffn_kernel, tm=tm, n_seg=n_seg, fc=fc, nt=nt, nf=nf)
    return pl.pallas_call(
        kern,
        grid=(n_batch, nt, nf),
        in_specs=[
            pl.BlockSpec((tm, D_MODEL), row_blk),
            pl.BlockSpec(memory_space=pl.ANY),
            st_blk,
            pl.BlockSpec((D_MODEL, fc), lambda b, i, f: (0, f)),
            pl.BlockSpec((D_MODEL, fc), lambda b, i, f: (0, f)),
            pl.BlockSpec((CONV_WIDTH, fc), lambda b, i, f: (0, f)),
            pl.BlockSpec((1, fc), lambda b, i, f: (0, f)),
            pl.BlockSpec((fc, D_MODEL), lambda b, i, f: (f, 0)),
            _resident((1, D_MODEL)),
        ],
        out_specs=[
            pl.BlockSpec((tm, D_MODEL), row_blk),
            pl.BlockSpec((n_seg, nf, SUBLANES, fc), lambda b, i, f: (b, 0, 0, 0)),
        ],
        out_shape=[
            jax.ShapeDtypeStruct((rows, D_MODEL), F32),
            jax.ShapeDtypeStruct((n_batch * n_seg, nf, SUBLANES, fc), F32),
        ],
        scratch_shapes=[
            pltpu.VMEM((nf, n_seg, SUBLANES, fc), F32),
            pltpu.VMEM((SUBLANES + seg, fc), F32),
            pltpu.VMEM((tm, D_MODEL), F32),
            pltpu.SemaphoreType.DMA(()),
        ],
        compiler_params=pltpu.CompilerParams(
            dimension_semantics=("arbitrary", "arbitrary", "arbitrary"),
            vmem_limit_bytes=VMEM_LIMIT_BYTES),
        name="conv_ffn",
    )(h2d, x2d, ginit, wg, wu, cw, cb, wd, gpost)


def _rope_tables(pos):
    half = ROT_DIM // 2
    dim = jnp.arange(LANES) % HEAD_DIM
    inv = ROPE_THETA ** (-(2 * (dim % half)).astype(F32) / ROT_DIM)
    ang = pos[:, None] * inv[None, :]
    cos, sin = jnp.cos(ang), jnp.sin(ang)
    first, second = (dim < half)[None, :], ((dim >= half) & (dim < ROT_DIM))[None, :]
    return (jnp.where(first | second, cos, 1.0), jnp.where(first, -sin, 0.0),
            jnp.where(second, sin, 0.0))


def _pad_state(state):
    return jnp.pad(state, ((0, 0), (SUBLANES - (CONV_WIDTH - 1), 0), (0, 0)))


def _layer(x2d, tabs, kinit, vinit, cinit, finit, mk, mv, w, *, mixer_cfg, mix_out_tm, cross_cfg,
           ffn_cfg, carry):
    tile = lambda cfg: dict(n_batch=cfg[0], tm=cfg[1], n_seg=cfg[2])
    mix, k_new, v_new, cstate = _mixer(
        x2d, tabs, kinit, vinit, cinit, w["g_mix_pre"], w["w_mix_in"], w["conv_mix_w"],
        w["g_grp_conv"], w["g_grp_attn"], w["sinks"], carry=carry, **tile(mixer_cfg))
    x1 = _mix_out(x2d, mix, w["w_mix_out"], w["g_mix_post"], tm=mix_out_tm)
    x2, h_ffn = _cross(x1, mk, mv, w["g_x_pre"], w["w_xq"], w["w_xo"], w["g_x_post"],
                       w["g_ffn_pre"], **tile(cross_cfg))
    x3, fstate = _ffn(h_ffn, x2, finit, w["w_gate"], w["w_up"], w["conv_ffn_w"],
                      w["conv_ffn_b"], w["w_down"], w["g_ffn_post"], fc=FFN_CHUNK,
                      **tile(ffn_cfg))
    fstate = jnp.swapaxes(fstate, 1, 2).reshape(fstate.shape[0], SUBLANES, D_FF)
    return x3, k_new, v_new, cstate, fstate


def kernel(x_prompt, x_sample, cache_mem_k, cache_mem_v, cache_swa_k, cache_swa_v,
           state_mix_conv, state_ffn_conv, mem_prompt,
           g_mix_pre, w_mix_in, conv_mix_w, g_grp_conv, g_grp_attn, attn_sinks,
           w_mix_out, g_mix_post, g_mem, w_xk, w_xv, g_x_pre, w_xq, w_xo, g_x_post,
           g_ffn_pre, w_gate, w_up, conv_ffn_w, conv_ffn_b, w_down, g_ffn_post):
    n_p, s_p, _ = x_prompt.shape
    n_s, s_s, _ = x_sample.shape
    depth = w_mix_in.shape[0]
    swa_len = cache_swa_k.shape[2]
    tabs_p = _rope_tables(jnp.arange(s_p, dtype=F32))
    tabs_s = tuple(jnp.tile(a, (n_s // 2, 1))
                   for a in _rope_tables(PAST_LEN + jnp.arange(s_s, dtype=F32)))

    yp = x_prompt.reshape(n_p * s_p, D_MODEL)
    ys = x_sample.reshape(n_s * s_s, D_MODEL)
    outs = [[] for _ in range(10)]
    for l in range(depth):
        row = lambda a: a[l][None, :]
        w = {
            "sinks": attn_sinks[l],
            "g_mix_pre": row(g_mix_pre), "w_mix_in": w_mix_in[l].astype(BF16),
            "conv_mix_w": conv_mix_w[l], "g_grp_conv": row(g_grp_conv),
            "g_grp_attn": row(g_grp_attn),
            "w_mix_out": w_mix_out[l].astype(BF16), "g_mix_post": row(g_mix_post),
            "g_x_pre": row(g_x_pre), "w_xq": w_xq[l].astype(BF16), "w_xo": w_xo[l].astype(BF16),
            "g_x_post": row(g_x_post), "g_ffn_pre": row(g_ffn_pre),
            "w_gate": w_gate[l].astype(BF16), "w_up": w_up[l].astype(BF16),
            "conv_ffn_w": conv_ffn_w[l], "conv_ffn_b": row(conv_ffn_b),
            "w_down": w_down[l].astype(BF16), "g_ffn_post": row(g_ffn_post),
        }
        mk32, mv32, mkb, mvb = _memory_kv(
            mem_prompt.reshape(n_p * N_MEM, D_MODEL), row(g_mem),
            w_xk[l].astype(BF16), w_xv[l].astype(BF16))
        yp, kp, vp, cp, fp = _layer(
            yp, tabs_p,
            jnp.zeros((n_p, WINDOW, KV_WIDTH), F32), jnp.zeros((n_p, WINDOW, KV_WIDTH), F32),
            jnp.zeros((n_p, SUBLANES, D_CONV), F32), jnp.zeros((n_p, SUBLANES, D_FF), F32),
            mkb.reshape(n_p, N_MEM, D_MODEL), mvb.reshape(n_p, N_MEM, D_MODEL), w,
            mixer_cfg=(n_p, 512, 1), mix_out_tm=1024, cross_cfg=(n_p, 512, 1),
            ffn_cfg=(n_p, 1024, 1), carry=True)
        ys, ks, vs, cs, fs = _layer(
            ys, tabs_s,
            cache_swa_k[l].reshape(n_s, swa_len, KV_WIDTH),
            cache_swa_v[l].reshape(n_s, swa_len, KV_WIDTH),
            _pad_state(state_mix_conv[l]), _pad_state(state_ffn_conv[l]),
            cache_mem_k[l].astype(BF16).reshape(n_s, N_MEM, D_MODEL),
            cache_mem_v[l].astype(BF16).reshape(n_s, N_MEM, D_MODEL), w,
            mixer_cfg=(2, n_s * s_s // 2, n_s // 2), mix_out_tm=n_s * s_s,
            cross_cfg=(1, n_s * s_s, n_s),
            ffn_cfg=(1, n_s * s_s, n_s), carry=False)
        keep_p = min(WINDOW, s_p)
        tail = CONV_WIDTH - 1
        new_k = ks.reshape(n_s, s_s, N_KV_HEADS, HEAD_DIM)
        new_v = vs.reshape(n_s, s_s, N_KV_HEADS, HEAD_DIM)
        layer_out = (
            mk32.reshape(n_p, N_MEM, N_MEM_HEADS, MEM_HEAD_DIM),
            mv32.reshape(n_p, N_MEM, N_MEM_HEADS, MEM_HEAD_DIM),
            kp.reshape(n_p, keep_p, N_KV_HEADS, HEAD_DIM),
            vp.reshape(n_p, keep_p, N_KV_HEADS, HEAD_DIM),
            cp[:, SUBLANES - tail:], fp[:, SUBLANES - tail:],
            jnp.concatenate([cache_swa_k[l], new_k], axis=1)[:, -swa_len:],
            jnp.concatenate([cache_swa_v[l], new_v], axis=1)[:, -swa_len:],
            cs[:, SUBLANES - tail:], fs[:, SUBLANES - tail:],
        )
        for acc, o in zip(outs, layer_out):
            acc.append(o)

    return (yp.reshape(n_p, s_p, D_MODEL), ys.reshape(n_s, s_s, D_MODEL),
            *[jnp.stack(o) for o in outs])
```

```python
import functools

import jax
import jax.numpy as jnp
from jax import lax
from jax.experimental import pallas as pl
from jax.experimental.pallas import tpu as pltpu

D_MODEL = 2048
CHUNK = 64
WINDOW = 128
CONV_WIDTH = 3
HEAD_DIM = 64
N_Q_HEADS = 16
N_KV_HEADS = 4
ATTN_WIDTH = N_Q_HEADS * HEAD_DIM
KV_WIDTH = N_KV_HEADS * HEAD_DIM
D_CONV = D_MODEL // 2
ROT_DIM = HEAD_DIM // 4
ROPE_THETA = 500000.0
N_MEM = 256
N_MEM_HEADS = 4
MEM_HEAD_DIM = D_MODEL // N_MEM_HEADS
D_FF = 11 * D_MODEL // 4
PAST_LEN = 1024
EPS = 1e-6
NEG_INF = -1e30

BF16 = jnp.bfloat16
F32 = jnp.float32

LANES = 128
SUBLANES = 8
HALF = LANES // 2
BAND = WINDOW + CHUNK
VMEM_LIMIT_BYTES = 60 * 1024 * 1024
FFN_CHUNK = 512
ATTN_BLOCK = 128

_NT = (((1,), (1,)), ((), ()))


def _dot(a, b):
    return jnp.dot(a, b, preferred_element_type=F32)


def _dot_nt(a, b):
    return lax.dot_general(a, b, _NT, preferred_element_type=F32)


def _rms_scale(x):
    return lax.rsqrt(jnp.mean(x * x, axis=-1, keepdims=True) + EPS)


def _aligned(index, multiple):
    return index if isinstance(index, int) else pl.multiple_of(index, multiple)


def _resident(shape):
    nd = len(shape)
    return pl.BlockSpec(shape, lambda *_: (0,) * nd, pipeline_mode=pl.Buffered(1))


def _causal_conv(cs_ref, prev8, cur, taps):
    rows = cur.shape[0]
    cs_ref[0:SUBLANES, :] = prev8
    cs_ref[SUBLANES:SUBLANES + rows, :] = cur
    x1 = cs_ref[pl.ds(SUBLANES - 1, rows), :]
    x2 = cs_ref[pl.ds(SUBLANES - 2, rows), :]
    return taps[0:1] * x2 + taps[1:2] * x1 + taps[2:3] * cur


def _memkv_kernel(mem_ref, g_ref, wk_ref, wv_ref, k32_ref, v32_ref, kbf_ref, vbf_ref, hm_ref):
    @pl.when(pl.program_id(0) == 0)
    def _():
        m = mem_ref[...]
        hm_ref[...] = (m * _rms_scale(m) * g_ref[...]).astype(BF16)

    hm = hm_ref[...]
    k = _dot(hm, wk_ref[...])
    v = _dot(hm, wv_ref[...])
    k32_ref[...] = k
    v32_ref[...] = v
    kbf_ref[...] = k.astype(BF16)
    vbf_ref[...] = v.astype(BF16)


def _memory_kv(mem2d, g_mem, wk, wv):
    rows = mem2d.shape[0]
    nc = 512
    out_w = wk.shape[1]
    col = lambda j: (0, j)
    return pl.pallas_call(
        _memkv_kernel,
        grid=(out_w // nc,),
        in_specs=[
            _resident((rows, D_MODEL)),
            _resident((1, D_MODEL)),
            pl.BlockSpec((D_MODEL, nc), col),
            pl.BlockSpec((D_MODEL, nc), col),
        ],
        out_specs=[pl.BlockSpec((rows, nc), col)] * 4,
        out_shape=[
            jax.ShapeDtypeStruct((rows, out_w), F32),
            jax.ShapeDtypeStruct((rows, out_w), F32),
            jax.ShapeDtypeStruct((rows, out_w), BF16),
            jax.ShapeDtypeStruct((rows, out_w), BF16),
        ],
        scratch_shapes=[pltpu.VMEM((rows, D_MODEL), BF16)],
        compiler_params=pltpu.CompilerParams(
            dimension_semantics=("arbitrary",), vmem_limit_bytes=VMEM_LIMIT_BYTES),
        name="memory_kv",
    )(mem2d, g_mem, wk, wv)


def _mixer_kernel(x_ref, cos_ref, s1_ref, s2_ref, kinit_ref, vinit_ref, cinit_ref,
                  gpre_ref, win_ref, convw_ref, gconv_ref, gattn_ref, sink_ref,
                  mix_ref, kout_ref, vout_ref, cstate_ref,
                  h_ref, yc_ref, q_ref, ke_ref, ko_ref, ve_ref, vo_ref, cs_ref, att_ref,
                  *, tm, n_seg, carry, keep, conv_cb, ab):
    i = pl.program_id(1)
    seg = tm // n_seg
    nblk = seg // ab
    nkeys = ab + WINDOW
    lo = lax.broadcasted_iota(jnp.int32, (1, LANES), 1) < HALF

    def rope(blk, cos_t, s1_t, s2_t):
        return (blk * cos_t + pltpu.roll(blk, LANES - ROT_DIM // 2, 1) * s1_t
                + pltpu.roll(blk, ROT_DIM // 2, 1) * s2_t)

    def store_kv(kf, vf, s, row0):
        rows = kf.shape[0]
        for hp in range(N_KV_HEADS // 2):
            own = kf[:, LANES * hp:LANES * (hp + 1)]
            swp = pltpu.roll(own, HALF, 1)
            for par in range(2):
                h = 2 * hp + par
                low, high = (own, swp) if par == 0 else (swp, own)
                ke_ref[s, h, row0:row0 + rows, :] = jnp.where(lo, low, 0.0).astype(BF16)
                ko_ref[s, h, row0:row0 + rows, :] = jnp.where(lo, 0.0, high).astype(BF16)
        vt = vf.T.astype(BF16)
        zeros = jnp.zeros((HALF, rows), BF16)
        for h in range(N_KV_HEADS):
            vh = vt[HEAD_DIM * h:HEAD_DIM * (h + 1)]
            ve_ref[s, h, :, row0:row0 + rows] = jnp.concatenate([vh, zeros], axis=0)
            vo_ref[s, h, :, row0:row0 + rows] = jnp.concatenate([zeros, vh], axis=0)

    @pl.when(i == 0)
    def _():
        for s in range(n_seg):
            store_kv(kinit_ref[s], vinit_ref[s], s, 0)
            cstate_ref[s] = cinit_ref[s]
        att_ref[...] = jnp.zeros_like(att_ref)

    x = x_ref[...]
    h_ref[...] = (x * _rms_scale(x) * gpre_ref[...]).astype(BF16)
    hb = h_ref[...]

    def conv_block(cb):
        hbl = h_ref[...]
        c0 = pl.multiple_of(cb * conv_cb, conv_cb)
        bg = _dot(hbl, win_ref[:, pl.ds(c0, conv_cb)])
        cg = _dot(hbl, win_ref[:, pl.ds(D_CONV + c0, conv_cb)])
        ug = _dot(hbl, win_ref[:, pl.ds(2 * D_CONV + c0, conv_cb)])
        cu = cg * ug
        taps = convw_ref[cb]
        ssq_parts = []
        for s in range(n_seg):
            r0 = s * seg
            cur = cu[r0:r0 + seg]
            y = bg[r0:r0 + seg] * _causal_conv(cs_ref, cstate_ref[s, cb], cur, taps)
            ssq_parts.append(jnp.sum(y * y, axis=-1, keepdims=True))
            yc_ref[cb, r0:r0 + seg, :] = y
            cstate_ref[s, cb] = cur[seg - SUBLANES:seg]
        return ssq_parts[0] if n_seg == 1 else jnp.concatenate(ssq_parts, axis=0)

    cos_t = cos_ref[...]
    s1_t = s1_ref[...]
    s2_t = s2_ref[...]
    q0 = 3 * D_CONV
    qf = _dot(hb, win_ref[:, q0:q0 + ATTN_WIDTH])
    for jb in range(ATTN_WIDTH // LANES):
        blk = rope(qf[:, LANES * jb:LANES * (jb + 1)], cos_t, s1_t, s2_t)
        q_ref[:, LANES * jb:LANES * (jb + 1)] = (blk * (HEAD_DIM ** -0.5)).astype(BF16)
    k0 = q0 + ATTN_WIDTH
    kv = _dot(hb, win_ref[:, k0:k0 + 2 * KV_WIDTH])
    kf = jnp.concatenate(
        [rope(kv[:, LANES * jb:LANES * (jb + 1)], cos_t, s1_t, s2_t)
         for jb in range(KV_WIDTH // LANES)], axis=1)
    vf = kv[:, KV_WIDTH:2 * KV_WIDTH]
    for s in range(n_seg):
        r0 = s * seg
        store_kv(kf[r0:r0 + seg], vf[r0:r0 + seg], s, WINDOW)
        kout_ref[s] = kf[r0 + seg - keep:r0 + seg]
        vout_ref[s] = vf[r0 + seg - keep:r0 + seg]

    key_pos = lax.broadcasted_iota(jnp.int32, (nkeys, 2 * ab), 0)
    if ab > CHUNK:
        tok = lax.broadcasted_iota(jnp.int32, (nkeys, 2 * ab), 1) % ab
        first_key = (tok // CHUNK) * CHUNK
        band_bias = jnp.where((key_pos >= first_key) & (key_pos < first_key + BAND), 0.0, NEG_INF)
    else:
        band_bias = None
    feat_lo = lax.broadcasted_iota(jnp.int32, (LANES, 1), 0) < HALF

    def attend_probs(idx):
        s = 0 if n_seg == 1 else idx // nblk
        j = 0 if nblk == 1 else idx % nblk
        rows = pl.ds(pl.multiple_of(idx * ab, ab), ab)
        band = pl.ds(_aligned(j * ab, ab), nkeys)
        bias = band_bias
        if carry:
            first_valid = WINDOW - (i * seg + j * ab)
            start_bias = jnp.where(key_pos >= first_valid, 0.0, NEG_INF)
            bias = start_bias if bias is None else bias + start_bias
        heads = range(N_KV_HEADS)
        scores = []
        for h in heads:
            qa = jnp.concatenate(
                [q_ref[rows, 2 * LANES * h:2 * LANES * h + LANES],
                 q_ref[rows, 2 * LANES * h + LANES:2 * LANES * (h + 1)]], axis=0)
            st_e = _dot_nt(ke_ref[s, h, band, :], qa)
            st_o = _dot_nt(ko_ref[s, h, band, :], qa)
            if bias is not None:
                st_e = st_e + bias
                st_o = st_o + bias
            scores.append((st_e, st_o))
        stats = []
        for h in heads:
            sink_e = sink_ref[h, 0:1, :]
            sink_o = sink_ref[h, 1:2, :]
            m_e = jnp.maximum(jnp.max(scores[h][0], axis=0, keepdims=True), sink_e)
            m_o = jnp.maximum(jnp.max(scores[h][1], axis=0, keepdims=True), sink_o)
            stats.append((m_e, m_o, jnp.exp(sink_e - m_e), jnp.exp(sink_o - m_o)))
        probs = [(jnp.exp(scores[h][0] - stats[h][0]), jnp.exp(scores[h][1] - stats[h][1]))
                 for h in heads]
        return (s, rows, band), probs, stats

    def attend_values(where, probs, stats):
        s, _, band = where
        for h in range(N_KV_HEADS):
            p_e, p_o = probs[h]
            den_e = jnp.sum(p_e, axis=0, keepdims=True) + stats[h][2]
            den_o = jnp.sum(p_o, axis=0, keepdims=True) + stats[h][3]
            acc_t = (_dot(ve_ref[s, h, :, band], p_e.astype(BF16))
                     + _dot(vo_ref[s, h, :, band], p_o.astype(BF16)))
            att_ref[h] = acc_t * jnp.where(feat_lo, 1.0 / den_e, 1.0 / den_o)

    def finish_block(idx):
        rows = pl.ds(_aligned(idx * ab, ab), ab)
        outs = []
        for h in range(N_KV_HEADS):
            acc = att_ref[h].T
            outs += [acc[0:ab], acc[ab:2 * ab]]
        o = jnp.concatenate(outs, axis=1)
        mix_ref[rows, D_CONV:D_CONV + ATTN_WIDTH] = (
            o * _rms_scale(o) * gattn_ref[...]).astype(BF16)

    n_trips = n_seg * nblk

    def trip(idx, ssq):
        finish_block(jnp.maximum(idx - 1, 0))
        attn = attend_probs(idx)
        ssq = ssq + conv_block(idx)
        attend_values(*attn)
        return ssq

    ssq_c = lax.fori_loop(0, n_trips, trip, jnp.zeros((tm, 1), F32))
    finish_block(n_trips - 1)

    if carry:
        for h in range(N_KV_HEADS):
            for ref in (ke_ref, ko_ref):
                ref[0, h, 0:WINDOW, :] = ref[0, h, seg:seg + WINDOW, :]
            for ref in (ve_ref, vo_ref):
                ref[0, h, :, 0:WINDOW] = ref[0, h, :, seg:seg + WINDOW]

    rs_c = lax.rsqrt(ssq_c * (1.0 / D_CONV) + EPS)
    for cb in range(D_CONV // conv_cb):
        mix_ref[:, cb * conv_cb:(cb + 1) * conv_cb] = (
            yc_ref[cb] * rs_c * gconv_ref[cb]).astype(BF16)


def _mixer(x2d, tabs, kinit, vinit, cinit, gpre, win, convw, gconv, gattn, sinks,
           *, n_batch, tm, n_seg, carry):
    rows = x2d.shape[0]
    nt = rows // (n_batch * tm)
    seg = tm // n_seg
    keep = min(WINDOW, seg)
    ab = min(ATTN_BLOCK, seg)
    n_trips = tm // ab
    conv_cb = D_CONV // n_trips
    row_blk = lambda b, i: (b * nt + i, 0)
    tab_blk = lambda b, i: (i, 0)
    seg_blk = lambda b, i: (b, 0, 0)
    cst_blk = pl.BlockSpec((n_seg, n_trips, SUBLANES, conv_cb), lambda b, i: (b, 0, 0, 0))
    kern = functools.partial(_mixer_kernel, tm=tm, n_seg=n_seg, carry=carry, keep=keep,
                             conv_cb=conv_cb, ab=ab)
    n_segs_total = n_batch * n_seg
    taps = jnp.swapaxes(convw.reshape(CONV_WIDTH, n_trips, conv_cb), 0, 1)
    gconv_b = gconv.reshape(n_trips, 1, conv_cb)
    cinit_b = jnp.swapaxes(cinit.reshape(n_segs_total, SUBLANES, n_trips, conv_cb), 1, 2)
    by_parity = jnp.swapaxes(sinks.reshape(N_KV_HEADS, 2, 2), 1, 2)[:, :, :, None]
    sink_tab = jnp.broadcast_to(by_parity, (N_KV_HEADS, 2, 2, ab)).reshape(N_KV_HEADS, 2, 2 * ab)
    sink_tab = jnp.pad(sink_tab, ((0, 0), (0, SUBLANES - 2), (0, 0)))
    mix, k_tail, v_tail, cstate = pl.pallas_call(
        kern,
        grid=(n_batch, nt),
        in_specs=[
            pl.BlockSpec((tm, D_MODEL), row_blk),
            pl.BlockSpec((tm, LANES), tab_blk),
            pl.BlockSpec((tm, LANES), tab_blk),
            pl.BlockSpec((tm, LANES), tab_blk),
            pl.BlockSpec((n_seg, WINDOW, KV_WIDTH), seg_blk),
            pl.BlockSpec((n_seg, WINDOW, KV_WIDTH), seg_blk),
            cst_blk,
            _resident((1, D_MODEL)),
            _resident(win.shape),
            _resident(taps.shape),
            _resident(gconv_b.shape),
            _resident((1, ATTN_WIDTH)),
            _resident((N_KV_HEADS, SUBLANES, 2 * ab)),
        ],
        out_specs=[
            pl.BlockSpec((tm, D_CONV + ATTN_WIDTH), row_blk),
            pl.BlockSpec((n_seg, keep, KV_WIDTH), seg_blk),
            pl.BlockSpec((n_seg, keep, KV_WIDTH), seg_blk),
            cst_blk,
        ],
        out_shape=[
            jax.ShapeDtypeStruct((rows, D_CONV + ATTN_WIDTH), BF16),
            jax.ShapeDtypeStruct((n_segs_total, keep, KV_WIDTH), F32),
            jax.ShapeDtypeStruct((n_segs_total, keep, KV_WIDTH), F32),
            jax.ShapeDtypeStruct((n_segs_total, n_trips, SUBLANES, conv_cb), F32),
        ],
        scratch_shapes=[
            pltpu.VMEM((tm, D_MODEL), BF16),
            pltpu.VMEM((n_trips, tm, conv_cb), F32),
            pltpu.VMEM((tm, ATTN_WIDTH), BF16),
            pltpu.VMEM((n_seg, N_KV_HEADS, WINDOW + seg, LANES), BF16),
            pltpu.VMEM((n_seg, N_KV_HEADS, WINDOW + seg, LANES), BF16),
            pltpu.VMEM((n_seg, N_KV_HEADS, LANES, WINDOW + seg), BF16),
            pltpu.VMEM((n_seg, N_KV_HEADS, LANES, WINDOW + seg), BF16),
            pltpu.VMEM((SUBLANES + seg, conv_cb), F32),
            pltpu.VMEM((N_KV_HEADS, LANES, 2 * ab), F32),
        ],
        compiler_params=pltpu.CompilerParams(
            dimension_semantics=("arbitrary", "arbitrary"), vmem_limit_bytes=VMEM_LIMIT_BYTES),
        name="mixer",
    )(x2d, *tabs, kinit, vinit, cinit_b, gpre, win, taps, gconv_b, gattn, sink_tab)
    cstate = jnp.swapaxes(cstate, 1, 2).reshape(n_segs_total, SUBLANES, D_CONV)
    return mix, k_tail, v_tail, cstate


def _mix_out_kernel(x_ref, mix_ref, wout_ref, gpost_ref, out_ref):
    half = x_ref.shape[0] // 2
    for r0 in (0, half):
        y = _dot(mix_ref[r0:r0 + half, :], wout_ref[...])
        out_ref[r0:r0 + half, :] = x_ref[r0:r0 + half, :] + y * _rms_scale(y) * gpost_ref[...]


def _mix_out(x2d, mix, wout, gpost, *, tm):
    rows = x2d.shape[0]
    row_blk = lambda i: (i, 0)
    width = mix.shape[1]
    return pl.pallas_call(
        _mix_out_kernel,
        grid=(rows // tm,),
        in_specs=[
            pl.BlockSpec((tm, D_MODEL), row_blk),
            pl.BlockSpec((tm, width), row_blk),
            _resident((width, D_MODEL)),
            _resident((1, D_MODEL)),
        ],
        out_specs=pl.BlockSpec((tm, D_MODEL), row_blk),
        out_shape=jax.ShapeDtypeStruct((rows, D_MODEL), F32),
        compiler_params=pltpu.CompilerParams(
            dimension_semantics=("arbitrary",), vmem_limit_bytes=VMEM_LIMIT_BYTES),
        name="mix_out",
    )(x2d, mix, wout, gpost)


def _cross_kernel(x_ref, mk_ref, mv_ref, gpre_ref, wq_ref, wo_ref, gpost_ref, gnext_ref,
                  out_ref, hnext_ref, o_ref, *, tm, n_seg):
    seg = tm // n_seg
    x = x_ref[...]
    hb = (x * _rms_scale(x) * gpre_ref[...]).astype(BF16)
    for hd in range(N_MEM_HEADS):
        c0 = hd * MEM_HEAD_DIM
        q = _dot(hb, wq_ref[:, c0:c0 + MEM_HEAD_DIM]).astype(BF16)
        for s in range(n_seg):
            r0 = s * seg
            sc = _dot_nt(q[r0:r0 + seg], mk_ref[s, :, c0:c0 + MEM_HEAD_DIM]) * (MEM_HEAD_DIM ** -0.5)
            p = jnp.exp(sc - jnp.max(sc, axis=-1, keepdims=True))
            den = jnp.sum(p, axis=-1, keepdims=True)
            o = _dot(p.astype(BF16), mv_ref[s, :, c0:c0 + MEM_HEAD_DIM]) * (1.0 / den)
            o_ref[r0:r0 + seg, c0:c0 + MEM_HEAD_DIM] = o.astype(BF16)
    half = tm // 2
    for r0 in (0, half):
        y = _dot(o_ref[r0:r0 + half, :], wo_ref[...])
        x_out = x[r0:r0 + half] + y * _rms_scale(y) * gpost_ref[...]
        out_ref[r0:r0 + half, :] = x_out
        hnext_ref[r0:r0 + half, :] = (x_out * _rms_scale(x_out) * gnext_ref[...]).astype(BF16)


def _cross(x2d, mk, mv, gpre, wq, wo, gpost, gnext, *, n_batch, tm, n_seg):
    rows = x2d.shape[0]
    nt = rows // (n_batch * tm)
    row_blk = lambda b, i: (b * nt + i, 0)
    if n_batch == 1:
        mem_blk = _resident((n_seg, N_MEM, D_MODEL))
    else:
        mem_blk = pl.BlockSpec((n_seg, N_MEM, D_MODEL), lambda b, i: (b, 0, 0))
    kern = functools.partial(_cross_kernel, tm=tm, n_seg=n_seg)
    return pl.pallas_call(
        kern,
        grid=(n_batch, nt),
        in_specs=[
            pl.BlockSpec((tm, D_MODEL), row_blk),
            mem_blk, mem_blk,
            _resident((1, D_MODEL)),
            _resident((D_MODEL, D_MODEL)),
            _resident((D_MODEL, D_MODEL)),
            _resident((1, D_MODEL)),
            _resident((1, D_MODEL)),
        ],
        out_specs=[pl.BlockSpec((tm, D_MODEL), row_blk), pl.BlockSpec((tm, D_MODEL), row_blk)],
        out_shape=[jax.ShapeDtypeStruct((rows, D_MODEL), F32),
                   jax.ShapeDtypeStruct((rows, D_MODEL), BF16)],
        scratch_shapes=[pltpu.VMEM((tm, D_MODEL), BF16)],
        compiler_params=pltpu.CompilerParams(
            dimension_semantics=("arbitrary", "arbitrary"), vmem_limit_bytes=VMEM_LIMIT_BYTES),
        name="cross_attn",
    )(x2d, mk, mv, gpre, wq, wo, gpost, gnext)


def _ffn_kernel(h_ref, x_hbm, ginit_ref, wg_ref, wu_ref, cw_ref, cb_ref, wd_ref, gpost_ref,
                out_ref, gstate_ref,
                gc_ref, cs_ref, xbuf_ref, xsem, *, tm, n_seg, fc, nt, nf):
    b = pl.program_id(0)
    i = pl.program_id(1)
    f = pl.program_id(2)
    seg = tm // n_seg

    def residual_copy():
        row0 = pl.multiple_of((b * nt + i) * tm, tm)
        return pltpu.make_async_copy(x_hbm.at[pl.ds(row0, tm)], xbuf_ref, xsem)

    @pl.when(f == 0)
    def _():
        out_ref[...] = jnp.zeros_like(out_ref)

    @pl.when(f == max(nf - 3, 0))
    def _():
        residual_copy().start()

    @pl.when(i == 0)
    def _():
        gc_ref[f] = ginit_ref[...]

    hb = h_ref[...]
    g = _dot(hb, wg_ref[...])
    u = _dot(hb, wu_ref[...])
    acts = []
    for s in range(n_seg):
        r0 = s * seg
        cur = g[r0:r0 + seg]
        a = _causal_conv(cs_ref, gc_ref[f, s], cur, cw_ref[...]) + cb_ref[...]
        acts.append((a * (1.0 / (1.0 + jnp.exp(-a))) * u[r0:r0 + seg]).astype(BF16))
        last = cur[seg - SUBLANES:seg]
        gc_ref[f, s] = last
        gstate_ref[s, f] = last
    act = acts[0] if n_seg == 1 else jnp.concatenate(acts, axis=0)
    out_ref[...] += _dot(act, wd_ref[...])

    @pl.when(f == nf - 1)
    def _():
        residual_copy().wait()
        y = out_ref[...]
        out_ref[...] = xbuf_ref[...] + y * _rms_scale(y) * gpost_ref[...]


def _ffn(h2d, x2d, ginit, wg, wu, cw, cb, wd, gpost, *, n_batch, tm, n_seg, fc):
    rows = x2d.shape[0]
    nt = rows // (n_batch * tm)
    nf = D_FF // fc
    seg = tm // n_seg
    row_blk = lambda b, i, f: (b * nt + i, 0)
    st_blk = pl.BlockSpec((n_seg, SUBLANES, fc), lambda b, i, f: (b, 0, f))
    kern = functools.partial(_ffn_kernel, tm=tm, n_seg=n_seg, fc=fc, nt=nt, nf=nf)
    return pl.pallas_call(
        kern,
        grid=(n_batch, nt, nf),
        in_specs=[
            pl.BlockSpec((tm, D_MODEL), row_blk),
            pl.BlockSpec(memory_space=pl.ANY),
            st_blk,
            pl.BlockSpec((D_MODEL, fc), lambda b, i, f: (0, f)),
            pl.BlockSpec((D_MODEL, fc), lambda b, i, f: (0, f)),
            pl.BlockSpec((CONV_WIDTH, fc), lambda b, i, f: (0, f)),
            pl.BlockSpec((1, fc), lambda b, i, f: (0, f)),
            pl.BlockSpec((fc, D_MODEL), lambda b, i, f: (f, 0)),
            _resident((1, D_MODEL)),
        ],
        out_specs=[
            pl.BlockSpec((tm, D_MODEL), row_blk),
            pl.BlockSpec((n_seg, nf, SUBLANES, fc), lambda b, i, f: (b, 0, 0, 0)),
        ],
        out_shape=[
            jax.ShapeDtypeStruct((rows, D_MODEL), F32),
            jax.ShapeDtypeStruct((n_batch * n_seg, nf, SUBLANES, fc), F32),
        ],
        scratch_shapes=[
            pltpu.VMEM((nf, n_seg, SUBLANES, fc), F32),
            pltpu.VMEM((SUBLANES + seg, fc), F32),
            pltpu.VMEM((tm, D_MODEL), F32),
            pltpu.SemaphoreType.DMA(()),
        ],
        compiler_params=pltpu.CompilerParams(
            dimension_semantics=("arbitrary", "arbitrary", "arbitrary"),
            vmem_limit_bytes=VMEM_LIMIT_BYTES),
        name="conv_ffn",
    )(h2d, x2d, ginit, wg, wu, cw, cb, wd, gpost)


def _rope_tables(pos):
    half = ROT_DIM // 2
    dim = jnp.arange(LANES) % HEAD_DIM
    inv = ROPE_THETA ** (-(2 * (dim % half)).astype(F32) / ROT_DIM)
    ang = pos[:, None] * inv[None, :]
    cos, sin = jnp.cos(ang), jnp.sin(ang)
    first, second = (dim < half)[None, :], ((dim >= half) & (dim < ROT_DIM))[None, :]
    return (jnp.where(first | second, cos, 1.0), jnp.where(first, -sin, 0.0),
            jnp.where(second, sin, 0.0))


def _pad_state(state):
    return jnp.pad(state, ((0, 0), (SUBLANES - (CONV_WIDTH - 1), 0), (0, 0)))


def _layer(x2d, tabs, kinit, vinit, cinit, finit, mk, mv, w, *, mixer_cfg, mix_out_tm, cross_cfg,
           ffn_cfg, carry):
    tile = lambda cfg: dict(n_batch=cfg[0], tm=cfg[1], n_seg=cfg[2])
    mix, k_new, v_new, cstate = _mixer(
        x2d, tabs, kinit, vinit, cinit, w["g_mix_pre"], w["w_mix_in"], w["conv_mix_w"],
        w["g_grp_conv"], w["g_grp_attn"], w["sinks"], carry=carry, **tile(mixer_cfg))
    x1 = _mix_out(x2d, mix, w["w_mix_out"], w["g_mix_post"], tm=mix_out_tm)
    x2, h_ffn = _cross(x1, mk, mv, w["g_x_pre"], w["w_xq"], w["w_xo"], w["g_x_post"],
                       w["g_ffn_pre"], **tile(cross_cfg))
    x3, fstate = _ffn(h_ffn, x2, finit, w["w_gate"], w["w_up"], w["conv_ffn_w"],
                      w["conv_ffn_b"], w["w_down"], w["g_ffn_post"], fc=FFN_CHUNK,
                      **tile(ffn_cfg))
    fstate = jnp.swapaxes(fstate, 1, 2).reshape(fstate.shape[0], SUBLANES, D_FF)
    return x3, k_new, v_new, cstate, fstate


def kernel(x_prompt, x_sample, cache_mem_k, cache_mem_v, cache_swa_k, cache_swa_v,
           state_mix_conv, state_ffn_conv, mem_prompt,
           g_mix_pre, w_mix_in, conv_mix_w, g_grp_conv, g_grp_attn, attn_sinks,
           w_mix_out, g_mix_post, g_mem, w_xk, w_xv, g_x_pre, w_xq, w_xo, g_x_post,
           g_ffn_pre, w_gate, w_up, conv_ffn_w, conv_ffn_b, w_down, g_ffn_post):
    n_p, s_p, _ = x_prompt.shape
    n_s, s_s, _ = x_sample.shape
    depth = w_mix_in.shape[0]
    swa_len = cache_swa_k.shape[2]
    tabs_p = _rope_tables(jnp.arange(s_p, dtype=F32))
    tabs_s = tuple(jnp.tile(a, (n_s // 2, 1))
                   for a in _rope_tables(PAST_LEN + jnp.arange(s_s, dtype=F32)))

    yp = x_prompt.reshape(n_p * s_p, D_MODEL)
    ys = x_sample.reshape(n_s * s_s, D_MODEL)
    outs = [[] for _ in range(10)]
    for l in range(depth):
        row = lambda a: a[l][None, :]
        w = {
            "sinks": attn_sinks[l],
            "g_mix_pre": row(g_mix_pre), "w_mix_in": w_mix_in[l].astype(BF16),
            "conv_mix_w": conv_mix_w[l], "g_grp_conv": row(g_grp_conv),
            "g_grp_attn": row(g_grp_attn),
            "w_mix_out": w_mix_out[l].astype(BF16), "g_mix_post": row(g_mix_post),
            "g_x_pre": row(g_x_pre), "w_xq": w_xq[l].astype(BF16), "w_xo": w_xo[l].astype(BF16),
            "g_x_post": row(g_x_post), "g_ffn_pre": row(g_ffn_pre),
            "w_gate": w_gate[l].astype(BF16), "w_up": w_up[l].astype(BF16),
            "conv_ffn_w": conv_ffn_w[l], "conv_ffn_b": row(conv_ffn_b),
            "w_down": w_down[l].astype(BF16), "g_ffn_post": row(g_ffn_post),
        }
        mk32, mv32, mkb, mvb = _memory_kv(
            mem_prompt.reshape(n_p * N_MEM, D_MODEL), row(g_mem),
            w_xk[l].astype(BF16), w_xv[l].astype(BF16))
        yp, kp, vp, cp, fp = _layer(
            yp, tabs_p,
            jnp.zeros((n_p, WINDOW, KV_WIDTH), F32), jnp.zeros((n_p, WINDOW, KV_WIDTH), F32),
            jnp.zeros((n_p, SUBLANES, D_CONV), F32), jnp.zeros((n_p, SUBLANES, D_FF), F32),
            mkb.reshape(n_p, N_MEM, D_MODEL), mvb.reshape(n_p, N_MEM, D_MODEL), w,
            mixer_cfg=(n_p, 512, 1), mix_out_tm=1024, cross_cfg=(n_p, 512, 1),
            ffn_cfg=(n_p, 1024, 1), carry=True)
        ys, ks, vs, cs, fs = _layer(
            ys, tabs_s,
            cache_swa_k[l].reshape(n_s, swa_len, KV_WIDTH),
            cache_swa_v[l].reshape(n_s, swa_len, KV_WIDTH),
            _pad_state(state_mix_conv[l]), _pad_state(state_ffn_conv[l]),
            cache_mem_k[l].reshape(n_s, N_MEM, D_MODEL).astype(BF16),
            cache_mem_v[l].reshape(n_s, N_MEM, D_MODEL).astype(BF16), w,
            mixer_cfg=(2, n_s * s_s // 2, n_s // 2), mix_out_tm=n_s * s_s,
            cross_cfg=(1, n_s * s_s, n_s),
            ffn_cfg=(1, n_s * s_s, n_s), carry=False)
        keep_p = min(WINDOW, s_p)
        tail = CONV_WIDTH - 1
        new_k = ks.reshape(n_s, s_s, N_KV_HEADS, HEAD_DIM)
        new_v = vs.reshape(n_s, s_s, N_KV_HEADS, HEAD_DIM)
        layer_out = (
            mk32.reshape(n_p, N_MEM, N_MEM_HEADS, MEM_HEAD_DIM),
            mv32.reshape(n_p, N_MEM, N_MEM_HEADS, MEM_HEAD_DIM),
            kp.reshape(n_p, keep_p, N_KV_HEADS, HEAD_DIM),
            vp.reshape(n_p, keep_p, N_KV_HEADS, HEAD_DIM),
            cp[:, SUBLANES - tail:], fp[:, SUBLANES - tail:],
            jnp.concatenate([cache_swa_k[l], new_k], axis=1)[:, -swa_len:],
            jnp.concatenate([cache_swa_v[l], new_v], axis=1)[:, -swa_len:],
            cs[:, SUBLANES - tail:], fs[:, SUBLANES - tail:],
        )
        for acc, o in zip(outs, layer_out):
            acc.append(o)

    return (yp.reshape(n_p, s_p, D_MODEL), ys.reshape(n_s, s_s, D_MODEL),
            *[jnp.stack(o) for o in outs])
```

```python
import functools

import jax
import jax.numpy as jnp
from jax import lax
from jax.experimental import pallas as pl
from jax.experimental.pallas import tpu as pltpu

D_MODEL = 2048
CHUNK = 64
WINDOW = 128
CONV_WIDTH = 3
HEAD_DIM = 64
N_Q_HEADS = 16
N_KV_HEADS = 4
ATTN_WIDTH = N_Q_HEADS * HEAD_DIM
KV_WIDTH = N_KV_HEADS * HEAD_DIM
D_CONV = D_MODEL // 2
ROT_DIM = HEAD_DIM // 4
ROPE_THETA = 500000.0
N_MEM = 256
N_MEM_HEADS = 4
MEM_HEAD_DIM = D_MODEL // N_MEM_HEADS
D_FF = 11 * D_MODEL // 4
PAST_LEN = 1024
EPS = 1e-6
NEG_INF = -1e30

BF16 = jnp.bfloat16
F32 = jnp.float32

LANES = 128
SUBLANES = 8
HALF = LANES // 2
BAND = WINDOW + CHUNK
VMEM_LIMIT_BYTES = 60 * 1024 * 1024
FFN_CHUNK = 512
ATTN_BLOCK = 128

_NT = (((1,), (1,)), ((), ()))


def _dot(a, b):
    return jnp.dot(a, b, preferred_element_type=F32)


def _dot_nt(a, b):
    return lax.dot_general(a, b, _NT, preferred_element_type=F32)


def _rms_scale(x):
    return lax.rsqrt(jnp.mean(x * x, axis=-1, keepdims=True) + EPS)


def _aligned(index, multiple):
    return index if isinstance(index, int) else pl.multiple_of(index, multiple)


def _resident(shape):
    nd = len(shape)
    return pl.BlockSpec(shape, lambda *_: (0,) * nd, pipeline_mode=pl.Buffered(1))


def _causal_conv(cs_ref, prev8, cur, taps):
    rows = cur.shape[0]
    cs_ref[0:SUBLANES, :] = prev8
    cs_ref[SUBLANES:SUBLANES + rows, :] = cur
    x1 = cs_ref[pl.ds(SUBLANES - 1, rows), :]
    x2 = cs_ref[pl.ds(SUBLANES - 2, rows), :]
    return taps[0:1] * x2 + taps[1:2] * x1 + taps[2:3] * cur


def _memkv_kernel(mem_ref, g_ref, wk_ref, wv_ref, k32_ref, v32_ref, kbf_ref, vbf_ref):
    m = mem_ref[...]
    hm = (m * _rms_scale(m) * g_ref[...]).astype(BF16)
    k = _dot(hm, wk_ref[...])
    v = _dot(hm, wv_ref[...])
    for h in range(N_MEM_HEADS):
        k32_ref[:, h, :] = k[:, MEM_HEAD_DIM * h:MEM_HEAD_DIM * (h + 1)]
        v32_ref[:, h, :] = v[:, MEM_HEAD_DIM * h:MEM_HEAD_DIM * (h + 1)]
    kbf_ref[...] = k.astype(BF16)
    vbf_ref[...] = v.astype(BF16)


def _memory_kv(mem2d, g_mem, wk, wv):
    rows = mem2d.shape[0]
    tm = N_MEM
    out_w = wk.shape[1]
    row = lambda i: (i, 0)
    row3 = lambda i: (i, 0, 0)
    return pl.pallas_call(
        _memkv_kernel,
        grid=(rows // tm,),
        in_specs=[
            pl.BlockSpec((tm, D_MODEL), row),
            _resident((1, D_MODEL)),
            _resident((D_MODEL, out_w)),
            _resident((D_MODEL, out_w)),
        ],
        out_specs=[
            pl.BlockSpec((tm, N_MEM_HEADS, MEM_HEAD_DIM), row3),
            pl.BlockSpec((tm, N_MEM_HEADS, MEM_HEAD_DIM), row3),
            pl.BlockSpec((tm, out_w), row),
            pl.BlockSpec((tm, out_w), row),
        ],
        out_shape=[
            jax.ShapeDtypeStruct((rows, N_MEM_HEADS, MEM_HEAD_DIM), F32),
            jax.ShapeDtypeStruct((rows, N_MEM_HEADS, MEM_HEAD_DIM), F32),
            jax.ShapeDtypeStruct((rows, out_w), BF16),
            jax.ShapeDtypeStruct((rows, out_w), BF16),
        ],
        compiler_params=pltpu.CompilerParams(
            dimension_semantics=("arbitrary",), vmem_limit_bytes=VMEM_LIMIT_BYTES),
        name="memory_kv",
    )(mem2d, g_mem, wk, wv)


def _mixer_kernel(x_ref, cos_ref, s1_ref, s2_ref, kinit_ref, vinit_ref, cinit_ref,
                  gpre_ref, win_ref, convw_ref, gconv_ref, gattn_ref, sink_ref,
                  mix_ref, kout_ref, vout_ref, cstate_ref,
                  h_ref, yc_ref, q_ref, ke_ref, ko_ref, ve_ref, vo_ref, cs_ref, att_ref,
                  *, tm, n_seg, carry, keep, conv_cb, ab):
    i = pl.program_id(1)
    seg = tm // n_seg
    nblk = seg // ab
    nkeys = ab + WINDOW
    lo = lax.broadcasted_iota(jnp.int32, (1, LANES), 1) < HALF

    def rope(blk, cos_t, s1_t, s2_t):
        return (blk * cos_t + pltpu.roll(blk, LANES - ROT_DIM // 2, 1) * s1_t
                + pltpu.roll(blk, ROT_DIM // 2, 1) * s2_t)

    def store_kv(kf, vf, s, row0):
        rows = kf.shape[0]
        for hp in range(N_KV_HEADS // 2):
            own = kf[:, LANES * hp:LANES * (hp + 1)]
            swp = pltpu.roll(own, HALF, 1)
            for par in range(2):
                h = 2 * hp + par
                low, high = (own, swp) if par == 0 else (swp, own)
                ke_ref[s, h, row0:row0 + rows, :] = jnp.where(lo, low, 0.0).astype(BF16)
                ko_ref[s, h, row0:row0 + rows, :] = jnp.where(lo, 0.0, high).astype(BF16)
        vt = vf.T.astype(BF16)
        zeros = jnp.zeros((HALF, rows), BF16)
        for h in range(N_KV_HEADS):
            vh = vt[HEAD_DIM * h:HEAD_DIM * (h + 1)]
            ve_ref[s, h, :, row0:row0 + rows] = jnp.concatenate([vh, zeros], axis=0)
            vo_ref[s, h, :, row0:row0 + rows] = jnp.concatenate([zeros, vh], axis=0)

    @pl.when(i == 0)
    def _():
        for s in range(n_seg):
            store_kv(kinit_ref[s], vinit_ref[s], s, 0)
            cstate_ref[s] = cinit_ref[s]
        att_ref[...] = jnp.zeros_like(att_ref)

    x = x_ref[...]
    h_ref[...] = (x * _rms_scale(x) * gpre_ref[...]).astype(BF16)
    hb = h_ref[...]

    def conv_block(cb):
        hbl = h_ref[...]
        c0 = pl.multiple_of(cb * conv_cb, conv_cb)
        bg = _dot(hbl, win_ref[:, pl.ds(c0, conv_cb)])
        cg = _dot(hbl, win_ref[:, pl.ds(D_CONV + c0, conv_cb)])
        ug = _dot(hbl, win_ref[:, pl.ds(2 * D_CONV + c0, conv_cb)])
        cu = cg * ug
        taps = convw_ref[cb]
        ssq_parts = []
        for s in range(n_seg):
            r0 = s * seg
            cur = cu[r0:r0 + seg]
            y = bg[r0:r0 + seg] * _causal_conv(cs_ref, cstate_ref[s, cb], cur, taps)
            ssq_parts.append(jnp.sum(y * y, axis=-1, keepdims=True))
            yc_ref[cb, r0:r0 + seg, :] = y
            cstate_ref[s, cb] = cur[seg - SUBLANES:seg]
        return ssq_parts[0] if n_seg == 1 else jnp.concatenate(ssq_parts, axis=0)

    cos_t = cos_ref[...]
    s1_t = s1_ref[...]
    s2_t = s2_ref[...]
    q0 = 3 * D_CONV
    qf = _dot(hb, win_ref[:, q0:q0 + ATTN_WIDTH])
    for jb in range(ATTN_WIDTH // LANES):
        blk = rope(qf[:, LANES * jb:LANES * (jb + 1)], cos_t, s1_t, s2_t)
        q_ref[:, LANES * jb:LANES * (jb + 1)] = (blk * (HEAD_DIM ** -0.5)).astype(BF16)
    k0 = q0 + ATTN_WIDTH
    kv = _dot(hb, win_ref[:, k0:k0 + 2 * KV_WIDTH])
    kf = jnp.concatenate(
        [rope(kv[:, LANES * jb:LANES * (jb + 1)], cos_t, s1_t, s2_t)
         for jb in range(KV_WIDTH // LANES)], axis=1)
    vf = kv[:, KV_WIDTH:2 * KV_WIDTH]
    for s in range(n_seg):
        r0 = s * seg
        store_kv(kf[r0:r0 + seg], vf[r0:r0 + seg], s, WINDOW)
        kout_ref[s] = kf[r0 + seg - keep:r0 + seg]
        vout_ref[s] = vf[r0 + seg - keep:r0 + seg]

    key_pos = lax.broadcasted_iota(jnp.int32, (nkeys, 2 * ab), 0)
    if ab > CHUNK:
        tok = lax.broadcasted_iota(jnp.int32, (nkeys, 2 * ab), 1) % ab
        first_key = (tok // CHUNK) * CHUNK
        band_bias = jnp.where((key_pos >= first_key) & (key_pos < first_key + BAND), 0.0, NEG_INF)
    else:
        band_bias = None
    feat_lo = lax.broadcasted_iota(jnp.int32, (LANES, 1), 0) < HALF

    def attend_probs(idx):
        s = 0 if n_seg == 1 else idx // nblk
        j = 0 if nblk == 1 else idx % nblk
        rows = pl.ds(pl.multiple_of(idx * ab, ab), ab)
        band = pl.ds(_aligned(j * ab, ab), nkeys)
        bias = band_bias
        if carry:
            first_valid = WINDOW - (i * seg + j * ab)
            start_bias = jnp.where(key_pos >= first_valid, 0.0, NEG_INF)
            bias = start_bias if bias is None else bias + start_bias
        heads = range(N_KV_HEADS)
        scores = []
        for h in heads:
            qa = jnp.concatenate(
                [q_ref[rows, 2 * LANES * h:2 * LANES * h + LANES],
                 q_ref[rows, 2 * LANES * h + LANES:2 * LANES * (h + 1)]], axis=0)
            st_e = _dot_nt(ke_ref[s, h, band, :], qa)
            st_o = _dot_nt(ko_ref[s, h, band, :], qa)
            if bias is not None:
                st_e = st_e + bias
                st_o = st_o + bias
            scores.append((st_e, st_o))
        stats = []
        for h in heads:
            sink_e = sink_ref[h, 0:1, :]
            sink_o = sink_ref[h, 1:2, :]
            m_e = jnp.maximum(jnp.max(scores[h][0], axis=0, keepdims=True), sink_e)
            m_o = jnp.maximum(jnp.max(scores[h][1], axis=0, keepdims=True), sink_o)
            stats.append((m_e, m_o, jnp.exp(sink_e - m_e), jnp.exp(sink_o - m_o)))
        probs = [(jnp.exp(scores[h][0] - stats[h][0]), jnp.exp(scores[h][1] - stats[h][1]))
                 for h in heads]
        return (s, rows, band), probs, stats

    def attend_values(where, probs, stats):
        s, _, band = where
        for h in range(N_KV_HEADS):
            p_e, p_o = probs[h]
            den_e = jnp.sum(p_e, axis=0, keepdims=True) + stats[h][2]
            den_o = jnp.sum(p_o, axis=0, keepdims=True) + stats[h][3]
            acc_t = (_dot(ve_ref[s, h, :, band], p_e.astype(BF16))
                     + _dot(vo_ref[s, h, :, band], p_o.astype(BF16)))
            att_ref[h] = acc_t * jnp.where(feat_lo, 1.0 / den_e, 1.0 / den_o)

    def finish_block(idx):
        rows = pl.ds(_aligned(idx * ab, ab), ab)
        outs = []
        for h in range(N_KV_HEADS):
            acc = att_ref[h].T
            outs += [acc[0:ab], acc[ab:2 * ab]]
        o = jnp.concatenate(outs, axis=1)
        mix_ref[rows, D_CONV:D_CONV + ATTN_WIDTH] = (
            o * _rms_scale(o) * gattn_ref[...]).astype(BF16)

    n_trips = n_seg * nblk

    def trip(idx, ssq):
        finish_block(jnp.maximum(idx - 1, 0))
        attn = attend_probs(idx)
        ssq = ssq + conv_block(idx)
        attend_values(*attn)
        return ssq

    ssq_c = lax.fori_loop(0, n_trips, trip, jnp.zeros((tm, 1), F32))
    finish_block(n_trips - 1)

    if carry:
        for h in range(N_KV_HEADS):
            for ref in (ke_ref, ko_ref):
                ref[0, h, 0:WINDOW, :] = ref[0, h, seg:seg + WINDOW, :]
            for ref in (ve_ref, vo_ref):
                ref[0, h, :, 0:WINDOW] = ref[0, h, :, seg:seg + WINDOW]

    rs_c = lax.rsqrt(ssq_c * (1.0 / D_CONV) + EPS)
    for cb in range(D_CONV // conv_cb):
        mix_ref[:, cb * conv_cb:(cb + 1) * conv_cb] = (
            yc_ref[cb] * rs_c * gconv_ref[cb]).astype(BF16)


def _mixer(x2d, tabs, kinit, vinit, cinit, gpre, win, convw, gconv, gattn, sinks,
           *, n_batch, tm, n_seg, carry):
    rows = x2d.shape[0]
    nt = rows // (n_batch * tm)
    seg = tm // n_seg
    keep = min(WINDOW, seg)
    ab = min(ATTN_BLOCK, seg)
    n_trips = tm // ab
    conv_cb = D_CONV // n_trips
    row_blk = lambda b, i: (b * nt + i, 0)
    tab_blk = lambda b, i: (i, 0)
    seg_blk = lambda b, i: (b, 0, 0)
    cst_blk = pl.BlockSpec((n_seg, n_trips, SUBLANES, conv_cb), lambda b, i: (b, 0, 0, 0))
    kern = functools.partial(_mixer_kernel, tm=tm, n_seg=n_seg, carry=carry, keep=keep,
                             conv_cb=conv_cb, ab=ab)
    n_segs_total = n_batch * n_seg
    taps = jnp.swapaxes(convw.reshape(CONV_WIDTH, n_trips, conv_cb), 0, 1)
    gconv_b = gconv.reshape(n_trips, 1, conv_cb)
    cinit_b = jnp.swapaxes(cinit.reshape(n_segs_total, SUBLANES, n_trips, conv_cb), 1, 2)
    by_parity = jnp.swapaxes(sinks.reshape(N_KV_HEADS, 2, 2), 1, 2)[:, :, :, None]
    sink_tab = jnp.broadcast_to(by_parity, (N_KV_HEADS, 2, 2, ab)).reshape(N_KV_HEADS, 2, 2 * ab)
    sink_tab = jnp.pad(sink_tab, ((0, 0), (0, SUBLANES - 2), (0, 0)))
    mix, k_tail, v_tail, cstate = pl.pallas_call(
        kern,
        grid=(n_batch, nt),
        in_specs=[
            pl.BlockSpec((tm, D_MODEL), row_blk),
            pl.BlockSpec((tm, LANES), tab_blk),
            pl.BlockSpec((tm, LANES), tab_blk),
            pl.BlockSpec((tm, LANES), tab_blk),
            pl.BlockSpec((n_seg, WINDOW, KV_WIDTH), seg_blk),
            pl.BlockSpec((n_seg, WINDOW, KV_WIDTH), seg_blk),
            cst_blk,
            _resident((1, D_MODEL)),
            _resident(win.shape),
            _resident(taps.shape),
            _resident(gconv_b.shape),
            _resident((1, ATTN_WIDTH)),
            _resident((N_KV_HEADS, SUBLANES, 2 * ab)),
        ],
        out_specs=[
            pl.BlockSpec((tm, D_CONV + ATTN_WIDTH), row_blk),
            pl.BlockSpec((n_seg, keep, KV_WIDTH), seg_blk),
            pl.BlockSpec((n_seg, keep, KV_WIDTH), seg_blk),
            cst_blk,
        ],
        out_shape=[
            jax.ShapeDtypeStruct((rows, D_CONV + ATTN_WIDTH), BF16),
            jax.ShapeDtypeStruct((n_segs_total, keep, KV_WIDTH), F32),
            jax.ShapeDtypeStruct((n_segs_total, keep, KV_WIDTH), F32),
            jax.ShapeDtypeStruct((n_segs_total, n_trips, SUBLANES, conv_cb), F32),
        ],
        scratch_shapes=[
            pltpu.VMEM((tm, D_MODEL), BF16),
            pltpu.VMEM((n_trips, tm, conv_cb), F32),
            pltpu.VMEM((tm, ATTN_WIDTH), BF16),
            pltpu.VMEM((n_seg, N_KV_HEADS, WINDOW + seg, LANES), BF16),
            pltpu.VMEM((n_seg, N_KV_HEADS, WINDOW + seg, LANES), BF16),
            pltpu.VMEM((n_seg, N_KV_HEADS, LANES, WINDOW + seg), BF16),
            pltpu.VMEM((n_seg, N_KV_HEADS, LANES, WINDOW + seg), BF16),
            pltpu.VMEM((SUBLANES + seg, conv_cb), F32),
            pltpu.VMEM((N_KV_HEADS, LANES, 2 * ab), F32),
        ],
        compiler_params=pltpu.CompilerParams(
            dimension_semantics=("arbitrary", "arbitrary"), vmem_limit_bytes=VMEM_LIMIT_BYTES),
        name="mixer",
    )(x2d, *tabs, kinit, vinit, cinit_b, gpre, win, taps, gconv_b, gattn, sink_tab)
    cstate = jnp.swapaxes(cstate, 1, 2).reshape(n_segs_total, SUBLANES, D_CONV)
    return mix, k_tail, v_tail, cstate


def _mix_out_kernel(x_ref, mix_ref, wout_ref, gpost_ref, out_ref):
    half = x_ref.shape[0] // 2
    for r0 in (0, half):
        y = _dot(mix_ref[r0:r0 + half, :], wout_ref[...])
        out_ref[r0:r0 + half, :] = x_ref[r0:r0 + half, :] + y * _rms_scale(y) * gpost_ref[...]


def _mix_out(x2d, mix, wout, gpost, *, tm):
    rows = x2d.shape[0]
    row_blk = lambda i: (i, 0)
    width = mix.shape[1]
    return pl.pallas_call(
        _mix_out_kernel,
        grid=(rows // tm,),
        in_specs=[
            pl.BlockSpec((tm, D_MODEL), row_blk),
            pl.BlockSpec((tm, width), row_blk),
            _resident((width, D_MODEL)),
            _resident((1, D_MODEL)),
        ],
        out_specs=pl.BlockSpec((tm, D_MODEL), row_blk),
        out_shape=jax.ShapeDtypeStruct((rows, D_MODEL), F32),
        compiler_params=pltpu.CompilerParams(
            dimension_semantics=("arbitrary",), vmem_limit_bytes=VMEM_LIMIT_BYTES),
        name="mix_out",
    )(x2d, mix, wout, gpost)


def _cross_kernel(x_ref, mk_ref, mv_ref, gpre_ref, wq_ref, wo_ref, gpost_ref, gnext_ref,
                  out_ref, hnext_ref, o_ref, *, tm, n_seg):
    seg = tm // n_seg
    x = x_ref[...]
    hb = (x * _rms_scale(x) * gpre_ref[...]).astype(BF16)
    for hd in range(N_MEM_HEADS):
        c0 = hd * MEM_HEAD_DIM
        q = _dot(hb, wq_ref[:, c0:c0 + MEM_HEAD_DIM]).astype(BF16)
        for s in range(n_seg):
            r0 = s * seg
            sc = _dot_nt(q[r0:r0 + seg], mk_ref[s, :, c0:c0 + MEM_HEAD_DIM]) * (MEM_HEAD_DIM ** -0.5)
            p = jnp.exp(sc - jnp.max(sc, axis=-1, keepdims=True))
            den = jnp.sum(p, axis=-1, keepdims=True)
            o = _dot(p.astype(BF16), mv_ref[s, :, c0:c0 + MEM_HEAD_DIM]) * (1.0 / den)
            o_ref[r0:r0 + seg, c0:c0 + MEM_HEAD_DIM] = o.astype(BF16)
    half = tm // 2
    for r0 in (0, half):
        y = _dot(o_ref[r0:r0 + half, :], wo_ref[...])
        x_out = x[r0:r0 + half] + y * _rms_scale(y) * gpost_ref[...]
        out_ref[r0:r0 + half, :] = x_out
        hnext_ref[r0:r0 + half, :] = (x_out * _rms_scale(x_out) * gnext_ref[...]).astype(BF16)


def _cross(x2d, mk, mv, gpre, wq, wo, gpost, gnext, *, n_batch, tm, n_seg):
    rows = x2d.shape[0]
    nt = rows // (n_batch * tm)
    row_blk = lambda b, i: (b * nt + i, 0)
    if n_batch == 1:
        mem_blk = _resident((n_seg, N_MEM, D_MODEL))
    else:
        mem_blk = pl.BlockSpec((n_seg, N_MEM, D_MODEL), lambda b, i: (b, 0, 0))
    kern = functools.partial(_cross_kernel, tm=tm, n_seg=n_seg)
    return pl.pallas_call(
        kern,
        grid=(n_batch, nt),
        in_specs=[
            pl.BlockSpec((tm, D_MODEL), row_blk),
            mem_blk, mem_blk,
            _resident((1, D_MODEL)),
            _resident((D_MODEL, D_MODEL)),
            _resident((D_MODEL, D_MODEL)),
            _resident((1, D_MODEL)),
            _resident((1, D_MODEL)),
        ],
        out_specs=[pl.BlockSpec((tm, D_MODEL), row_blk), pl.BlockSpec((tm, D_MODEL), row_blk)],
        out_shape=[jax.ShapeDtypeStruct((rows, D_MODEL), F32),
                   jax.ShapeDtypeStruct((rows, D_MODEL), BF16)],
        scratch_shapes=[pltpu.VMEM((tm, D_MODEL), BF16)],
        compiler_params=pltpu.CompilerParams(
            dimension_semantics=("arbitrary", "arbitrary"), vmem_limit_bytes=VMEM_LIMIT_BYTES),
        name="cross_attn",
    )(x2d, mk, mv, gpre, wq, wo, gpost, gnext)


def _ffn_kernel(h_ref, x_hbm, ginit_ref, wg_ref, wu_ref, cw_ref, cb_ref, wd_ref, gpost_ref,
                out_ref, gstate_ref,
                gc_ref, cs_ref, xbuf_ref, xsem, *, tm, n_seg, fc, nt, nf):
    b = pl.program_id(0)
    i = pl.program_id(1)
    f = pl.program_id(2)
    seg = tm // n_seg

    def residual_copy():
        row0 = pl.multiple_of((b * nt + i) * tm, tm)
        return pltpu.make_async_copy(x_hbm.at[pl.ds(row0, tm)], xbuf_ref, xsem)

    @pl.when(f == 0)
    def _():
        out_ref[...] = jnp.zeros_like(out_ref)

    @pl.when(f == max(nf - 3, 0))
    def _():
        residual_copy().start()

    @pl.when(i == 0)
    def _():
        gc_ref[f] = ginit_ref[...]

    hb = h_ref[...]
    g = _dot(hb, wg_ref[...])
    u = _dot(hb, wu_ref[...])
    acts = []
    for s in range(n_seg):
        r0 = s * seg
        cur = g[r0:r0 + seg]
        a = _causal_conv(cs_ref, gc_ref[f, s], cur, cw_ref[...]) + cb_ref[...]
        acts.append((a * (1.0 / (1.0 + jnp.exp(-a))) * u[r0:r0 + seg]).astype(BF16))
        last = cur[seg - SUBLANES:seg]
        gc_ref[f, s] = last
        gstate_ref[s, f] = last
    act = acts[0] if n_seg == 1 else jnp.concatenate(acts, axis=0)
    out_ref[...] += _dot(act, wd_ref[...])

    @pl.when(f == nf - 1)
    def _():
        residual_copy().wait()
        y = out_ref[...]
        out_ref[...] = xbuf_ref[...] + y * _rms_scale(y) * gpost_ref[...]


def _ffn(h2d, x2d, ginit, wg, wu, cw, cb, wd, gpost, *, n_batch, tm, n_seg, fc):
    rows = x2d.shape[0]
    nt = rows // (n_batch * tm)
    nf = D_FF // fc
    seg = tm // n_seg
    row_blk = lambda b, i, f: (b * nt + i, 0)
    st_blk = pl.BlockSpec((n_seg, SUBLANES, fc), lambda b, i, f: (b, 0, f))
    kern = functools.partial(_ffn_kernel, tm=tm, n_seg=n_seg, fc=fc, nt=nt, nf=nf)
    return pl.pallas_call(
        kern,
        grid=(n_batch, nt, nf),
        in_specs=[
            pl.BlockSpec((tm, D_MODEL), row_blk),
            pl.BlockSpec(memory_space=pl.ANY),
            st_blk,
            pl.BlockSpec((D_MODEL, fc), lambda b, i, f: (0, f)),
            pl.BlockSpec((D_MODEL, fc), lambda b, i, f: (0, f)),
            pl.BlockSpec((CONV_WIDTH, fc), lambda b, i, f: (0, f)),
            pl.BlockSpec((1, fc), lambda b, i, f: (0, f)),
            pl.BlockSpec((fc, D_MODEL), lambda b, i, f: (f, 0)),
            _resident((1, D_MODEL)),
        ],
        out_specs=[
            pl.BlockSpec((tm, D_MODEL), row_blk),
            pl.BlockSpec((n_seg, nf, SUBLANES, fc), lambda b, i, f: (b, 0, 0, 0)),
        ],
        out_shape=[
            jax.ShapeDtypeStruct((rows, D_MODEL), F32),
            jax.ShapeDtypeStruct((n_batch * n_seg, nf, SUBLANES, fc), F32),
        ],
        scratch_shapes=[
            pltpu.VMEM((nf, n_seg, SUBLANES, fc), F32),
            pltpu.VMEM((SUBLANES + seg, fc), F32),
            pltpu.VMEM((tm, D_MODEL), F32),
            pltpu.SemaphoreType.DMA(()),
        ],
        compiler_params=pltpu.CompilerParams(
            dimension_semantics=("arbitrary", "arbitrary", "arbitrary"),
            vmem_limit_bytes=VMEM_LIMIT_BYTES),
        name="conv_ffn",
    )(h2d, x2d, ginit, wg, wu, cw, cb, wd, gpost)


def _rope_tables(pos):
    half = ROT_DIM // 2
    inv = ROPE_THETA ** (-jnp.arange(0, ROT_DIM, 2, dtype=F32) / ROT_DIM)
    ang = pos[:, None] * inv[None, :]
    cos = jnp.tile(jnp.cos(ang), (1, LANES // half))
    sin = jnp.tile(jnp.sin(ang), (1, LANES // half))
    dim = jnp.arange(LANES) % HEAD_DIM
    first, second = (dim < half)[None, :], ((dim >= half) & (dim < ROT_DIM))[None, :]
    return (jnp.where(first | second, cos, 1.0), jnp.where(first, -sin, 0.0),
            jnp.where(second, sin, 0.0))


def _pad_state(state):
    return jnp.pad(state, ((0, 0), (SUBLANES - (CONV_WIDTH - 1), 0), (0, 0)))


def _layer(x2d, tabs, kinit, vinit, cinit, finit, mk, mv, w, *, mixer_cfg, mix_out_tm, cross_cfg,
           ffn_cfg, carry):
    tile = lambda cfg: dict(n_batch=cfg[0], tm=cfg[1], n_seg=cfg[2])
    mix, k_new, v_new, cstate = _mixer(
        x2d, tabs, kinit, vinit, cinit, w["g_mix_pre"], w["w_mix_in"], w["conv_mix_w"],
        w["g_grp_conv"], w["g_grp_attn"], w["sinks"], carry=carry, **tile(mixer_cfg))
    x1 = _mix_out(x2d, mix, w["w_mix_out"], w["g_mix_post"], tm=mix_out_tm)
    x2, h_ffn = _cross(x1, mk, mv, w["g_x_pre"], w["w_xq"], w["w_xo"], w["g_x_post"],
                       w["g_ffn_pre"], **tile(cross_cfg))
    x3, fstate = _ffn(h_ffn, x2, finit, w["w_gate"], w["w_up"], w["conv_ffn_w"],
                      w["conv_ffn_b"], w["w_down"], w["g_ffn_post"], fc=FFN_CHUNK,
                      **tile(ffn_cfg))
    fstate = jnp.swapaxes(fstate, 1, 2).reshape(fstate.shape[0], SUBLANES, D_FF)
    return x3, k_new, v_new, cstate, fstate


def kernel(x_prompt, x_sample, cache_mem_k, cache_mem_v, cache_swa_k, cache_swa_v,
           state_mix_conv, state_ffn_conv, mem_prompt,
           g_mix_pre, w_mix_in, conv_mix_w, g_grp_conv, g_grp_attn, attn_sinks,
           w_mix_out, g_mix_post, g_mem, w_xk, w_xv, g_x_pre, w_xq, w_xo, g_x_post,
           g_ffn_pre, w_gate, w_up, conv_ffn_w, conv_ffn_b, w_down, g_ffn_post):
    n_p, s_p, _ = x_prompt.shape
    n_s, s_s, _ = x_sample.shape
    depth = w_mix_in.shape[0]
    swa_len = cache_swa_k.shape[2]
    tabs_p = _rope_tables(jnp.arange(s_p, dtype=F32))
    tabs_s = tuple(jnp.tile(a, (n_s // 2, 1))
                   for a in _rope_tables(PAST_LEN + jnp.arange(s_s, dtype=F32)))

    yp = x_prompt.reshape(n_p * s_p, D_MODEL)
    ys = x_sample.reshape(n_s * s_s, D_MODEL)
    outs = [[] for _ in range(10)]
    for l in range(depth):
        row = lambda a: a[l][None, :]
        w = {
            "sinks": attn_sinks[l],
            "g_mix_pre": row(g_mix_pre), "w_mix_in": w_mix_in[l].astype(BF16),
            "conv_mix_w": conv_mix_w[l], "g_grp_conv": row(g_grp_conv),
            "g_grp_attn": row(g_grp_attn),
            "w_mix_out": w_mix_out[l].astype(BF16), "g_mix_post": row(g_mix_post),
            "g_x_pre": row(g_x_pre), "w_xq": w_xq[l].astype(BF16), "w_xo": w_xo[l].astype(BF16),
            "g_x_post": row(g_x_post), "g_ffn_pre": row(g_ffn_pre),
            "w_gate": w_gate[l].astype(BF16), "w_up": w_up[l].astype(BF16),
            "conv_ffn_w": conv_ffn_w[l], "conv_ffn_b": row(conv_ffn_b),
            "w_down": w_down[l].astype(BF16), "g_ffn_post": row(g_ffn_post),
        }
        mk32, mv32, mkb, mvb = _memory_kv(
            mem_prompt.reshape(n_p * N_MEM, D_MODEL), row(g_mem),
            w_xk[l].astype(BF16), w_xv[l].astype(BF16))
        yp, kp, vp, cp, fp = _layer(
            yp, tabs_p,
            jnp.zeros((n_p, WINDOW, KV_WIDTH), F32), jnp.zeros((n_p, WINDOW, KV_WIDTH), F32),
            jnp.zeros((n_p, SUBLANES, D_CONV), F32), jnp.zeros((n_p, SUBLANES, D_FF), F32),
            mkb.reshape(n_p, N_MEM, D_MODEL), mvb.reshape(n_p, N_MEM, D_MODEL), w,
            mixer_cfg=(n_p, 512, 1), mix_out_tm=1024, cross_cfg=(n_p, 512, 1),
            ffn_cfg=(n_p, 1024, 1), carry=True)
        ys, ks, vs, cs, fs = _layer(
            ys, tabs_s,
            cache_swa_k[l].reshape(n_s, swa_len, KV_WIDTH),
            cache_swa_v[l].reshape(n_s, swa_len, KV_WIDTH),
            _pad_state(state_mix_conv[l]), _pad_state(state_ffn_conv[l]),
            cache_mem_k[l].reshape(n_s, N_MEM, D_MODEL).astype(BF16),
            cache_mem_v[l].reshape(n_s, N_MEM, D_MODEL).astype(BF16), w,
            mixer_cfg=(2, n_s * s_s // 2, n_s // 2), mix_out_tm=n_s * s_s,
            cross_cfg=(1, n_s * s_s, n_s),
            ffn_cfg=(1, n_s * s_s, n_s), carry=False)
        keep_p = min(WINDOW, s_p)
        tail = CONV_WIDTH - 1
        new_k = ks.reshape(n_s, s_s, N_KV_HEADS, HEAD_DIM)
        new_v = vs.reshape(n_s, s_s, N_KV_HEADS, HEAD_DIM)
        layer_out = (
            mk32.reshape(n_p, N_MEM, N_MEM_HEADS, MEM_HEAD_DIM),
            mv32.reshape(n_p, N_MEM, N_MEM_HEADS, MEM_HEAD_DIM),
            kp.reshape(n_p, keep_p, N_KV_HEADS, HEAD_DIM),
            vp.reshape(n_p, keep_p, N_KV_HEADS, HEAD_DIM),
            cp[:, SUBLANES - tail:], fp[:, SUBLANES - tail:],
            jnp.concatenate([cache_swa_k[l], new_k], axis=1)[:, -swa_len:],
            jnp.concatenate([cache_swa_v[l], new_v], axis=1)[:, -swa_len:],
            cs[:, SUBLANES - tail:], fs[:, SUBLANES - tail:],
        )
        for acc, o in zip(outs, layer_out):
            acc.append(o)

    return (yp.reshape(n_p, s_p, D_MODEL), ys.reshape(n_s, s_s, D_MODEL),
            *[jnp.stack(o) for o in outs])
```

```python
import functools

import jax
import jax.numpy as jnp
from jax import lax
from jax.experimental import pallas as pl
from jax.experimental.pallas import tpu as pltpu

D_MODEL = 2048
CHUNK = 64
WINDOW = 128
CONV_WIDTH = 3
HEAD_DIM = 64
N_Q_HEADS = 16
N_KV_HEADS = 4
ATTN_WIDTH = N_Q_HEADS * HEAD_DIM
KV_WIDTH = N_KV_HEADS * HEAD_DIM
D_CONV = D_MODEL // 2
ROT_DIM = HEAD_DIM // 4
ROPE_THETA = 500000.0
N_MEM = 256
N_MEM_HEADS = 4
MEM_HEAD_DIM = D_MODEL // N_MEM_HEADS
D_FF = 11 * D_MODEL // 4
PAST_LEN = 1024
EPS = 1e-6
NEG_INF = -1e30

BF16 = jnp.bfloat16
F32 = jnp.float32

LANES = 128
SUBLANES = 8
HALF = LANES // 2
BAND = WINDOW + CHUNK
VMEM_LIMIT_BYTES = 60 * 1024 * 1024
FFN_CHUNK = 512
ATTN_BLOCK = 128

_NT = (((1,), (1,)), ((), ()))


def _dot(a, b):
    return jnp.dot(a, b, preferred_element_type=F32)


def _dot_nt(a, b):
    return lax.dot_general(a, b, _NT, preferred_element_type=F32)


def _rms_scale(x):
    return lax.rsqrt(jnp.mean(x * x, axis=-1, keepdims=True) + EPS)


def _aligned(index, multiple):
    return index if isinstance(index, int) else pl.multiple_of(index, multiple)


def _resident(shape):
    nd = len(shape)
    return pl.BlockSpec(shape, lambda *_: (0,) * nd, pipeline_mode=pl.Buffered(1))


def _causal_conv(cs_ref, prev8, cur, taps):
    rows = cur.shape[0]
    cs_ref[0:SUBLANES, :] = prev8
    cs_ref[SUBLANES:SUBLANES + rows, :] = cur
    x1 = cs_ref[pl.ds(SUBLANES - 1, rows), :]
    x2 = cs_ref[pl.ds(SUBLANES - 2, rows), :]
    return taps[0:1] * x2 + taps[1:2] * x1 + taps[2:3] * cur


def _memkv_kernel(mem_ref, g_ref, wk_ref, wv_ref, k32_ref, v32_ref, kbf_ref, vbf_ref):
    m = mem_ref[...]
    hm = (m * _rms_scale(m) * g_ref[...]).astype(BF16)
    k = _dot(hm, wk_ref[...])
    v = _dot(hm, wv_ref[...])
    for h in range(N_MEM_HEADS):
        k32_ref[:, h, :] = k[:, MEM_HEAD_DIM * h:MEM_HEAD_DIM * (h + 1)]
        v32_ref[:, h, :] = v[:, MEM_HEAD_DIM * h:MEM_HEAD_DIM * (h + 1)]
    kbf_ref[...] = k.astype(BF16)
    vbf_ref[...] = v.astype(BF16)


def _memory_kv(mem2d, g_mem, wk, wv):
    rows = mem2d.shape[0]
    tm = N_MEM
    out_w = wk.shape[1]
    row = lambda i: (i, 0)
    row3 = lambda i: (i, 0, 0)
    return pl.pallas_call(
        _memkv_kernel,
        grid=(rows // tm,),
        in_specs=[
            pl.BlockSpec((tm, D_MODEL), row),
            _resident((1, D_MODEL)),
            _resident((D_MODEL, out_w)),
            _resident((D_MODEL, out_w)),
        ],
        out_specs=[
            pl.BlockSpec((tm, N_MEM_HEADS, MEM_HEAD_DIM), row3),
            pl.BlockSpec((tm, N_MEM_HEADS, MEM_HEAD_DIM), row3),
            pl.BlockSpec((tm, out_w), row),
            pl.BlockSpec((tm, out_w), row),
        ],
        out_shape=[
            jax.ShapeDtypeStruct((rows, N_MEM_HEADS, MEM_HEAD_DIM), F32),
            jax.ShapeDtypeStruct((rows, N_MEM_HEADS, MEM_HEAD_DIM), F32),
            jax.ShapeDtypeStruct((rows, out_w), BF16),
            jax.ShapeDtypeStruct((rows, out_w), BF16),
        ],
        compiler_params=pltpu.CompilerParams(
            dimension_semantics=("arbitrary",), vmem_limit_bytes=VMEM_LIMIT_BYTES),
        name="memory_kv",
    )(mem2d, g_mem, wk, wv)


def _mixer_kernel(x_ref, cos_ref, s1_ref, s2_ref, kinit_ref, vinit_ref, cinit_ref,
                  gpre_ref, win_ref, convw_ref, gconv_ref, gattn_ref, sink_ref,
                  mix_ref, kout_ref, vout_ref, cstate_ref,
                  h_ref, yc_ref, q_ref, ke_ref, ko_ref, ve_ref, vo_ref, cs_ref,
                  *, tm, n_seg, carry, keep, conv_cb, ab):
    i = pl.program_id(1)
    seg = tm // n_seg
    nblk = seg // ab
    nkeys = ab + WINDOW
    lo = lax.broadcasted_iota(jnp.int32, (1, LANES), 1) < HALF

    def rope(blk, cos_t, s1_t, s2_t):
        return (blk * cos_t + pltpu.roll(blk, LANES - ROT_DIM // 2, 1) * s1_t
                + pltpu.roll(blk, ROT_DIM // 2, 1) * s2_t)

    def store_kv(kf, vf, s, row0):
        rows = kf.shape[0]
        for hp in range(N_KV_HEADS // 2):
            own = kf[:, LANES * hp:LANES * (hp + 1)]
            swp = pltpu.roll(own, HALF, 1)
            for par in range(2):
                h = 2 * hp + par
                low, high = (own, swp) if par == 0 else (swp, own)
                ke_ref[s, h, row0:row0 + rows, :] = jnp.where(lo, low, 0.0).astype(BF16)
                ko_ref[s, h, row0:row0 + rows, :] = jnp.where(lo, 0.0, high).astype(BF16)
        vt = vf.T.astype(BF16)
        zeros = jnp.zeros((HALF, rows), BF16)
        for h in range(N_KV_HEADS):
            vh = vt[HEAD_DIM * h:HEAD_DIM * (h + 1)]
            ve_ref[s, h, :, row0:row0 + rows] = jnp.concatenate([vh, zeros], axis=0)
            vo_ref[s, h, :, row0:row0 + rows] = jnp.concatenate([zeros, vh], axis=0)

    @pl.when(i == 0)
    def _():
        for s in range(n_seg):
            store_kv(kinit_ref[s], vinit_ref[s], s, 0)
            cstate_ref[s] = cinit_ref[s]

    x = x_ref[...]
    h_ref[...] = (x * _rms_scale(x) * gpre_ref[...]).astype(BF16)
    hb = h_ref[...]

    def conv_block(cb):
        hbl = h_ref[...]
        c0 = pl.multiple_of(cb * conv_cb, conv_cb)
        bg = _dot(hbl, win_ref[:, pl.ds(c0, conv_cb)])
        cg = _dot(hbl, win_ref[:, pl.ds(D_CONV + c0, conv_cb)])
        ug = _dot(hbl, win_ref[:, pl.ds(2 * D_CONV + c0, conv_cb)])
        cu = cg * ug
        taps = convw_ref[cb]
        ssq_parts = []
        for s in range(n_seg):
            r0 = s * seg
            cur = cu[r0:r0 + seg]
            y = bg[r0:r0 + seg] * _causal_conv(cs_ref, cstate_ref[s, cb], cur, taps)
            ssq_parts.append(jnp.sum(y * y, axis=-1, keepdims=True))
            yc_ref[cb, r0:r0 + seg, :] = y
            cstate_ref[s, cb] = cur[seg - SUBLANES:seg]
        return ssq_parts[0] if n_seg == 1 else jnp.concatenate(ssq_parts, axis=0)

    cos_t = cos_ref[...]
    s1_t = s1_ref[...]
    s2_t = s2_ref[...]
    q0 = 3 * D_CONV
    qf = _dot(hb, win_ref[:, q0:q0 + ATTN_WIDTH])
    for jb in range(ATTN_WIDTH // LANES):
        blk = rope(qf[:, LANES * jb:LANES * (jb + 1)], cos_t, s1_t, s2_t)
        q_ref[:, LANES * jb:LANES * (jb + 1)] = (blk * (HEAD_DIM ** -0.5)).astype(BF16)
    k0 = q0 + ATTN_WIDTH
    kv = _dot(hb, win_ref[:, k0:k0 + 2 * KV_WIDTH])
    kf = jnp.concatenate(
        [rope(kv[:, LANES * jb:LANES * (jb + 1)], cos_t, s1_t, s2_t)
         for jb in range(KV_WIDTH // LANES)], axis=1)
    vf = kv[:, KV_WIDTH:2 * KV_WIDTH]
    for s in range(n_seg):
        r0 = s * seg
        store_kv(kf[r0:r0 + seg], vf[r0:r0 + seg], s, WINDOW)
        kout_ref[s] = kf[r0 + seg - keep:r0 + seg]
        vout_ref[s] = vf[r0 + seg - keep:r0 + seg]

    key_pos = lax.broadcasted_iota(jnp.int32, (nkeys, 2 * ab), 0)
    if ab > CHUNK:
        tok = lax.broadcasted_iota(jnp.int32, (nkeys, 2 * ab), 1) % ab
        first_key = (tok // CHUNK) * CHUNK
        band_bias = jnp.where((key_pos >= first_key) & (key_pos < first_key + BAND), 0.0, NEG_INF)
    else:
        band_bias = None
    feat_lo = lax.broadcasted_iota(jnp.int32, (LANES, 1), 0) < HALF

    def block_place(idx):
        s = 0 if n_seg == 1 else idx // nblk
        j = 0 if nblk == 1 else idx % nblk
        return s, pl.ds(_aligned(idx * ab, ab), ab), pl.ds(_aligned(j * ab, ab), nkeys)

    def attend_block(idx):
        s, rows, band = block_place(idx)
        j = 0 if nblk == 1 else idx % nblk
        bias = band_bias
        if carry:
            first_valid = WINDOW - (i * seg + j * ab)
            start_bias = jnp.where(key_pos >= first_valid, 0.0, NEG_INF)
            bias = start_bias if bias is None else bias + start_bias
        heads = range(N_KV_HEADS)
        scores = []
        for h in heads:
            qa = jnp.concatenate(
                [q_ref[rows, 2 * LANES * h:2 * LANES * h + LANES],
                 q_ref[rows, 2 * LANES * h + LANES:2 * LANES * (h + 1)]], axis=0)
            st_e = _dot_nt(ke_ref[s, h, band, :], qa)
            st_o = _dot_nt(ko_ref[s, h, band, :], qa)
            if bias is not None:
                st_e = st_e + bias
                st_o = st_o + bias
            scores.append((st_e, st_o))
        stats = []
        for h in heads:
            sink_e = sink_ref[h, 0:1, :]
            sink_o = sink_ref[h, 1:2, :]
            m_e = jnp.maximum(jnp.max(scores[h][0], axis=0, keepdims=True), sink_e)
            m_o = jnp.maximum(jnp.max(scores[h][1], axis=0, keepdims=True), sink_o)
            stats.append((m_e, m_o, jnp.exp(sink_e - m_e), jnp.exp(sink_o - m_o)))
        outs = []
        for h in heads:
            p_e = jnp.exp(scores[h][0] - stats[h][0])
            p_o = jnp.exp(scores[h][1] - stats[h][1])
            den_e = jnp.sum(p_e, axis=0, keepdims=True) + stats[h][2]
            den_o = jnp.sum(p_o, axis=0, keepdims=True) + stats[h][3]
            acc_t = (_dot(ve_ref[s, h, :, band], p_e.astype(BF16))
                     + _dot(vo_ref[s, h, :, band], p_o.astype(BF16)))
            acc = (acc_t * jnp.where(feat_lo, 1.0 / den_e, 1.0 / den_o)).T
            outs += [acc[0:ab], acc[ab:2 * ab]]
        o = jnp.concatenate(outs, axis=1)
        mix_ref[rows, D_CONV:D_CONV + ATTN_WIDTH] = (
            o * _rms_scale(o) * gattn_ref[...]).astype(BF16)

    def trip(idx, ssq):
        attend_block(idx)
        return ssq + conv_block(idx)

    ssq_c = lax.fori_loop(0, n_seg * nblk, trip, jnp.zeros((tm, 1), F32))

    if carry:
        for h in range(N_KV_HEADS):
            for ref in (ke_ref, ko_ref):
                ref[0, h, 0:WINDOW, :] = ref[0, h, seg:seg + WINDOW, :]
            for ref in (ve_ref, vo_ref):
                ref[0, h, :, 0:WINDOW] = ref[0, h, :, seg:seg + WINDOW]

    rs_c = lax.rsqrt(ssq_c * (1.0 / D_CONV) + EPS)
    for cb in range(D_CONV // conv_cb):
        mix_ref[:, cb * conv_cb:(cb + 1) * conv_cb] = (
            yc_ref[cb] * rs_c * gconv_ref[cb]).astype(BF16)


def _mixer(x2d, tabs, kinit, vinit, cinit, gpre, win, convw, gconv, gattn, sinks,
           *, n_batch, tm, n_seg, carry):
    rows = x2d.shape[0]
    nt = rows // (n_batch * tm)
    seg = tm // n_seg
    keep = min(WINDOW, seg)
    ab = min(ATTN_BLOCK, seg)
    n_trips = tm // ab
    conv_cb = D_CONV // n_trips
    row_blk = lambda b, i: (b * nt + i, 0)
    tab_blk = lambda b, i: (i, 0)
    seg_blk = lambda b, i: (b, 0, 0)
    cst_blk = pl.BlockSpec((n_seg, n_trips, SUBLANES, conv_cb), lambda b, i: (b, 0, 0, 0))
    kern = functools.partial(_mixer_kernel, tm=tm, n_seg=n_seg, carry=carry, keep=keep,
                             conv_cb=conv_cb, ab=ab)
    n_segs_total = n_batch * n_seg
    taps = jnp.swapaxes(convw.reshape(CONV_WIDTH, n_trips, conv_cb), 0, 1)
    gconv_b = gconv.reshape(n_trips, 1, conv_cb)
    cinit_b = jnp.swapaxes(cinit.reshape(n_segs_total, SUBLANES, n_trips, conv_cb), 1, 2)
    by_parity = jnp.swapaxes(sinks.reshape(N_KV_HEADS, 2, 2), 1, 2)[:, :, :, None]
    sink_tab = jnp.broadcast_to(by_parity, (N_KV_HEADS, 2, 2, ab)).reshape(N_KV_HEADS, 2, 2 * ab)
    sink_tab = jnp.pad(sink_tab, ((0, 0), (0, SUBLANES - 2), (0, 0)))
    mix, k_tail, v_tail, cstate = pl.pallas_call(
        kern,
        grid=(n_batch, nt),
        in_specs=[
            pl.BlockSpec((tm, D_MODEL), row_blk),
            pl.BlockSpec((tm, LANES), tab_blk),
            pl.BlockSpec((tm, LANES), tab_blk),
            pl.BlockSpec((tm, LANES), tab_blk),
            pl.BlockSpec((n_seg, WINDOW, KV_WIDTH), seg_blk),
            pl.BlockSpec((n_seg, WINDOW, KV_WIDTH), seg_blk),
            cst_blk,
            _resident((1, D_MODEL)),
            _resident(win.shape),
            _resident(taps.shape),
            _resident(gconv_b.shape),
            _resident((1, ATTN_WIDTH)),
            _resident((N_KV_HEADS, SUBLANES, 2 * ab)),
        ],
        out_specs=[
            pl.BlockSpec((tm, D_CONV + ATTN_WIDTH), row_blk),
            pl.BlockSpec((n_seg, keep, KV_WIDTH), seg_blk),
            pl.BlockSpec((n_seg, keep, KV_WIDTH), seg_blk),
            cst_blk,
        ],
        out_shape=[
            jax.ShapeDtypeStruct((rows, D_CONV + ATTN_WIDTH), BF16),
            jax.ShapeDtypeStruct((n_segs_total, keep, KV_WIDTH), F32),
            jax.ShapeDtypeStruct((n_segs_total, keep, KV_WIDTH), F32),
            jax.ShapeDtypeStruct((n_segs_total, n_trips, SUBLANES, conv_cb), F32),
        ],
        scratch_shapes=[
            pltpu.VMEM((tm, D_MODEL), BF16),
            pltpu.VMEM((n_trips, tm, conv_cb), F32),
            pltpu.VMEM((tm, ATTN_WIDTH), BF16),
            pltpu.VMEM((n_seg, N_KV_HEADS, WINDOW + seg, LANES), BF16),
            pltpu.VMEM((n_seg, N_KV_HEADS, WINDOW + seg, LANES), BF16),
            pltpu.VMEM((n_seg, N_KV_HEADS, LANES, WINDOW + seg), BF16),
            pltpu.VMEM((n_seg, N_KV_HEADS, LANES, WINDOW + seg), BF16),
            pltpu.VMEM((SUBLANES + seg, conv_cb), F32),
        ],
        compiler_params=pltpu.CompilerParams(
            dimension_semantics=("arbitrary", "arbitrary"), vmem_limit_bytes=VMEM_LIMIT_BYTES),
        name="mixer",
    )(x2d, *tabs, kinit, vinit, cinit_b, gpre, win, taps, gconv_b, gattn, sink_tab)
    cstate = jnp.swapaxes(cstate, 1, 2).reshape(n_segs_total, SUBLANES, D_CONV)
    return mix, k_tail, v_tail, cstate


def _mix_out_kernel(x_ref, mix_ref, wout_ref, gpost_ref, out_ref):
    half = x_ref.shape[0] // 2
    for r0 in (0, half):
        y = _dot(mix_ref[r0:r0 + half, :], wout_ref[...])
        out_ref[r0:r0 + half, :] = x_ref[r0:r0 + half, :] + y * _rms_scale(y) * gpost_ref[...]


def _mix_out(x2d, mix, wout, gpost, *, tm):
    rows = x2d.shape[0]
    row_blk = lambda i: (i, 0)
    width = mix.shape[1]
    return pl.pallas_call(
        _mix_out_kernel,
        grid=(rows // tm,),
        in_specs=[
            pl.BlockSpec((tm, D_MODEL), row_blk),
            pl.BlockSpec((tm, width), row_blk),
            _resident((width, D_MODEL)),
            _resident((1, D_MODEL)),
        ],
        out_specs=pl.BlockSpec((tm, D_MODEL), row_blk),
        out_shape=jax.ShapeDtypeStruct((rows, D_MODEL), F32),
        compiler_params=pltpu.CompilerParams(
            dimension_semantics=("arbitrary",), vmem_limit_bytes=VMEM_LIMIT_BYTES),
        name="mix_out",
    )(x2d, mix, wout, gpost)


def _cross_kernel(x_ref, mk_ref, mv_ref, gpre_ref, wq_ref, wo_ref, gpost_ref, gnext_ref,
                  out_ref, hnext_ref, o_ref, *, tm, n_seg):
    seg = tm // n_seg
    x = x_ref[...]
    hb = (x * _rms_scale(x) * gpre_ref[...]).astype(BF16)
    for hd in range(N_MEM_HEADS):
        c0 = hd * MEM_HEAD_DIM
        q = _dot(hb, wq_ref[:, c0:c0 + MEM_HEAD_DIM]).astype(BF16)
        for s in range(n_seg):
            r0 = s * seg
            sc = _dot_nt(q[r0:r0 + seg], mk_ref[s, :, c0:c0 + MEM_HEAD_DIM]) * (MEM_HEAD_DIM ** -0.5)
            p = jnp.exp(sc - jnp.max(sc, axis=-1, keepdims=True))
            den = jnp.sum(p, axis=-1, keepdims=True)
            o = _dot(p.astype(BF16), mv_ref[s, :, c0:c0 + MEM_HEAD_DIM]) * (1.0 / den)
            o_ref[r0:r0 + seg, c0:c0 + MEM_HEAD_DIM] = o.astype(BF16)
    half = tm // 2
    for r0 in (0, half):
        y = _dot(o_ref[r0:r0 + half, :], wo_ref[...])
        x_out = x[r0:r0 + half] + y * _rms_scale(y) * gpost_ref[...]
        out_ref[r0:r0 + half, :] = x_out
        hnext_ref[r0:r0 + half, :] = (x_out * _rms_scale(x_out) * gnext_ref[...]).astype(BF16)


def _cross(x2d, mk, mv, gpre, wq, wo, gpost, gnext, *, n_batch, tm, n_seg):
    rows = x2d.shape[0]
    nt = rows // (n_batch * tm)
    row_blk = lambda b, i: (b * nt + i, 0)
    if n_batch == 1:
        mem_blk = _resident((n_seg, N_MEM, D_MODEL))
    else:
        mem_blk = pl.BlockSpec((n_seg, N_MEM, D_MODEL), lambda b, i: (b, 0, 0))
    kern = functools.partial(_cross_kernel, tm=tm, n_seg=n_seg)
    return pl.pallas_call(
        kern,
        grid=(n_batch, nt),
        in_specs=[
            pl.BlockSpec((tm, D_MODEL), row_blk),
            mem_blk, mem_blk,
            _resident((1, D_MODEL)),
            _resident((D_MODEL, D_MODEL)),
            _resident((D_MODEL, D_MODEL)),
            _resident((1, D_MODEL)),
            _resident((1, D_MODEL)),
        ],
        out_specs=[pl.BlockSpec((tm, D_MODEL), row_blk), pl.BlockSpec((tm, D_MODEL), row_blk)],
        out_shape=[jax.ShapeDtypeStruct((rows, D_MODEL), F32),
                   jax.ShapeDtypeStruct((rows, D_MODEL), BF16)],
        scratch_shapes=[pltpu.VMEM((tm, D_MODEL), BF16)],
        compiler_params=pltpu.CompilerParams(
            dimension_semantics=("arbitrary", "arbitrary"), vmem_limit_bytes=VMEM_LIMIT_BYTES),
        name="cross_attn",
    )(x2d, mk, mv, gpre, wq, wo, gpost, gnext)


def _ffn_kernel(h_ref, x_hbm, ginit_ref, wg_ref, wu_ref, cw_ref, cb_ref, wd_ref, gpost_ref,
                out_ref, gstate_ref,
                gc_ref, cs_ref, xbuf_ref, xsem, *, tm, n_seg, fc, nt, nf):
    b = pl.program_id(0)
    i = pl.program_id(1)
    f = pl.program_id(2)
    seg = tm // n_seg

    def residual_copy():
        row0 = pl.multiple_of((b * nt + i) * tm, tm)
        return pltpu.make_async_copy(x_hbm.at[pl.ds(row0, tm)], xbuf_ref, xsem)

    @pl.when(f == 0)
    def _():
        out_ref[...] = jnp.zeros_like(out_ref)

    @pl.when(f == max(nf - 3, 0))
    def _():
        residual_copy().start()

    @pl.when(i == 0)
    def _():
        gc_ref[f] = ginit_ref[...]

    hb = h_ref[...]
    g = _dot(hb, wg_ref[...])
    u = _dot(hb, wu_ref[...])
    acts = []
    for s in range(n_seg):
        r0 = s * seg
        cur = g[r0:r0 + seg]
        a = _causal_conv(cs_ref, gc_ref[f, s], cur, cw_ref[...]) + cb_ref[...]
        acts.append((a * (1.0 / (1.0 + jnp.exp(-a))) * u[r0:r0 + seg]).astype(BF16))
        last = cur[seg - SUBLANES:seg]
        gc_ref[f, s] = last
        gstate_ref[s, f] = last
    act = acts[0] if n_seg == 1 else jnp.concatenate(acts, axis=0)
    out_ref[...] += _dot(act, wd_ref[...])

    @pl.when(f == nf - 1)
    def _():
        residual_copy().wait()
        y = out_ref[...]
        out_ref[...] = xbuf_ref[...] + y * _rms_scale(y) * gpost_ref[...]


def _ffn(h2d, x2d, ginit, wg, wu, cw, cb, wd, gpost, *, n_batch, tm, n_seg, fc):
    rows = x2d.shape[0]
    nt = rows // (n_batch * tm)
    nf = D_FF // fc
    seg = tm // n_seg
    row_blk = lambda b, i, f: (b * nt + i, 0)
    st_blk = pl.BlockSpec((n_seg, SUBLANES, fc), lambda b, i, f: (b, 0, f))
    kern = functools.partial(_ffn_kernel, tm=tm, n_seg=n_seg, fc=fc, nt=nt, nf=nf)
    return pl.pallas_call(
        kern,
        grid=(n_batch, nt, nf),
        in_specs=[
            pl.BlockSpec((tm, D_MODEL), row_blk),
            pl.BlockSpec(memory_space=pl.ANY),
            st_blk,
            pl.BlockSpec((D_MODEL, fc), lambda b, i, f: (0, f)),
            pl.BlockSpec((D_MODEL, fc), lambda b, i, f: (0, f)),
            pl.BlockSpec((CONV_WIDTH, fc), lambda b, i, f: (0, f)),
            pl.BlockSpec((1, fc), lambda b, i, f: (0, f)),
            pl.BlockSpec((fc, D_MODEL), lambda b, i, f: (f, 0)),
            _resident((1, D_MODEL)),
        ],
        out_specs=[
            pl.BlockSpec((tm, D_MODEL), row_blk),
            pl.BlockSpec((n_seg, nf, SUBLANES, fc), lambda b, i, f: (b, 0, 0, 0)),
        ],
        out_shape=[
            jax.ShapeDtypeStruct((rows, D_MODEL), F32),
            jax.ShapeDtypeStruct((n_batch * n_seg, nf, SUBLANES, fc), F32),
        ],
        scratch_shapes=[
            pltpu.VMEM((nf, n_seg, SUBLANES, fc), F32),
            pltpu.VMEM((SUBLANES + seg, fc), F32),
            pltpu.VMEM((tm, D_MODEL), F32),
            pltpu.SemaphoreType.DMA(()),
        ],
        compiler_params=pltpu.CompilerParams(
            dimension_semantics=("arbitrary", "arbitrary", "arbitrary"),
            vmem_limit_bytes=VMEM_LIMIT_BYTES),
        name="conv_ffn",
    )(h2d, x2d, ginit, wg, wu, cw, cb, wd, gpost)


def _rope_tables(pos):
    half = ROT_DIM // 2
    inv = ROPE_THETA ** (-jnp.arange(0, ROT_DIM, 2, dtype=F32) / ROT_DIM)
    ang = pos[:, None] * inv[None, :]
    cos = jnp.tile(jnp.cos(ang), (1, LANES // half))
    sin = jnp.tile(jnp.sin(ang), (1, LANES // half))
    dim = jnp.arange(LANES) % HEAD_DIM
    first, second = (dim < half)[None, :], ((dim >= half) & (dim < ROT_DIM))[None, :]
    return (jnp.where(first | second, cos, 1.0), jnp.where(first, -sin, 0.0),
            jnp.where(second, sin, 0.0))


def _merge_heads_bf16(mem):
    n, m, heads, d = mem.shape
    return mem.reshape(n, m, heads * d).astype(BF16)


def _pad_state(state):
    return jnp.pad(state, ((0, 0), (SUBLANES - (CONV_WIDTH - 1), 0), (0, 0)))


def _layer(x2d, tabs, kinit, vinit, cinit, finit, mk, mv, w, *, mixer_cfg, mix_out_tm, cross_cfg,
           ffn_cfg, carry):
    tile = lambda cfg: dict(n_batch=cfg[0], tm=cfg[1], n_seg=cfg[2])
    mix, k_new, v_new, cstate = _mixer(
        x2d, tabs, kinit, vinit, cinit, w["g_mix_pre"], w["w_mix_in"], w["conv_mix_w"],
        w["g_grp_conv"], w["g_grp_attn"], w["sinks"], carry=carry, **tile(mixer_cfg))
    x1 = _mix_out(x2d, mix, w["w_mix_out"], w["g_mix_post"], tm=mix_out_tm)
    x2, h_ffn = _cross(x1, mk, mv, w["g_x_pre"], w["w_xq"], w["w_xo"], w["g_x_post"],
                       w["g_ffn_pre"], **tile(cross_cfg))
    x3, fstate = _ffn(h_ffn, x2, finit, w["w_gate"], w["w_up"], w["conv_ffn_w"],
                      w["conv_ffn_b"], w["w_down"], w["g_ffn_post"], fc=FFN_CHUNK,
                      **tile(ffn_cfg))
    fstate = jnp.swapaxes(fstate, 1, 2).reshape(fstate.shape[0], SUBLANES, D_FF)
    return x3, k_new, v_new, cstate, fstate


def kernel(x_prompt, x_sample, cache_mem_k, cache_mem_v, cache_swa_k, cache_swa_v,
           state_mix_conv, state_ffn_conv, mem_prompt,
           g_mix_pre, w_mix_in, conv_mix_w, g_grp_conv, g_grp_attn, attn_sinks,
           w_mix_out, g_mix_post, g_mem, w_xk, w_xv, g_x_pre, w_xq, w_xo, g_x_post,
           g_ffn_pre, w_gate, w_up, conv_ffn_w, conv_ffn_b, w_down, g_ffn_post):
    n_p, s_p, _ = x_prompt.shape
    n_s, s_s, _ = x_sample.shape
    depth = w_mix_in.shape[0]
    swa_len = cache_swa_k.shape[2]
    tabs_p = _rope_tables(jnp.arange(s_p, dtype=F32))
    tabs_s = tuple(jnp.tile(a, (n_s // 2, 1))
                   for a in _rope_tables(PAST_LEN + jnp.arange(s_s, dtype=F32)))

    yp = x_prompt.reshape(n_p * s_p, D_MODEL)
    ys = x_sample.reshape(n_s * s_s, D_MODEL)
    outs = [[] for _ in range(10)]
    for l in range(depth):
        row = lambda a: a[l][None, :]
        w = {
            "sinks": attn_sinks[l],
            "g_mix_pre": row(g_mix_pre), "w_mix_in": w_mix_in[l].astype(BF16),
            "conv_mix_w": conv_mix_w[l], "g_grp_conv": row(g_grp_conv),
            "g_grp_attn": row(g_grp_attn),
            "w_mix_out": w_mix_out[l].astype(BF16), "g_mix_post": row(g_mix_post),
            "g_x_pre": row(g_x_pre), "w_xq": w_xq[l].astype(BF16), "w_xo": w_xo[l].astype(BF16),
            "g_x_post": row(g_x_post), "g_ffn_pre": row(g_ffn_pre),
            "w_gate": w_gate[l].astype(BF16), "w_up": w_up[l].astype(BF16),
            "conv_ffn_w": conv_ffn_w[l], "conv_ffn_b": row(conv_ffn_b),
            "w_down": w_down[l].astype(BF16), "g_ffn_post": row(g_ffn_post),
        }
        mk32, mv32, mkb, mvb = _memory_kv(
            mem_prompt.reshape(n_p * N_MEM, D_MODEL), row(g_mem),
            w_xk[l].astype(BF16), w_xv[l].astype(BF16))
        yp, kp, vp, cp, fp = _layer(
            yp, tabs_p,
            jnp.zeros((n_p, WINDOW, KV_WIDTH), F32), jnp.zeros((n_p, WINDOW, KV_WIDTH), F32),
            jnp.zeros((n_p, SUBLANES, D_CONV), F32), jnp.zeros((n_p, SUBLANES, D_FF), F32),
            mkb.reshape(n_p, N_MEM, D_MODEL), mvb.reshape(n_p, N_MEM, D_MODEL), w,
            mixer_cfg=(n_p, 512, 1), mix_out_tm=1024, cross_cfg=(n_p, 512, 1),
            ffn_cfg=(n_p, 1024, 1), carry=True)
        ys, ks, vs, cs, fs = _layer(
            ys, tabs_s,
            cache_swa_k[l].reshape(n_s, swa_len, KV_WIDTH),
            cache_swa_v[l].reshape(n_s, swa_len, KV_WIDTH),
            _pad_state(state_mix_conv[l]), _pad_state(state_ffn_conv[l]),
            _merge_heads_bf16(cache_mem_k[l]), _merge_heads_bf16(cache_mem_v[l]), w,
            mixer_cfg=(2, n_s * s_s // 2, n_s // 2), mix_out_tm=n_s * s_s,
            cross_cfg=(1, n_s * s_s, n_s),
            ffn_cfg=(1, n_s * s_s, n_s), carry=False)
        keep_p = min(WINDOW, s_p)
        tail = CONV_WIDTH - 1
        new_k = ks.reshape(n_s, s_s, N_KV_HEADS, HEAD_DIM)
        new_v = vs.reshape(n_s, s_s, N_KV_HEADS, HEAD_DIM)
        layer_out = (
            mk32.reshape(n_p, N_MEM, N_MEM_HEADS, MEM_HEAD_DIM),
            mv32.reshape(n_p, N_MEM, N_MEM_HEADS, MEM_HEAD_DIM),
            kp.reshape(n_p, keep_p, N_KV_HEADS, HEAD_DIM),
            vp.reshape(n_p, keep_p, N_KV_HEADS, HEAD_DIM),
            cp[:, SUBLANES - tail:], fp[:, SUBLANES - tail:],
            jnp.concatenate([cache_swa_k[l], new_k], axis=1)[:, -swa_len:],
            jnp.concatenate([cache_swa_v[l], new_v], axis=1)[:, -swa_len:],
            cs[:, SUBLANES - tail:], fs[:, SUBLANES - tail:],
        )
        for acc, o in zip(outs, layer_out):
            acc.append(o)

    return (yp.reshape(n_p, s_p, D_MODEL), ys.reshape(n_s, s_s, D_MODEL),
            *[jnp.stack(o) for o in outs])
```

```python
import functools

import jax
import jax.numpy as jnp
from jax import lax
from jax.experimental import pallas as pl
from jax.experimental.pallas import tpu as pltpu

D_MODEL = 2048
CHUNK = 64
WINDOW = 128
CONV_WIDTH = 3
HEAD_DIM = 64
N_Q_HEADS = 16
N_KV_HEADS = 4
ATTN_WIDTH = N_Q_HEADS * HEAD_DIM
KV_WIDTH = N_KV_HEADS * HEAD_DIM
D_CONV = D_MODEL // 2
ROT_DIM = HEAD_DIM // 4
ROPE_THETA = 500000.0
N_MEM = 256
N_MEM_HEADS = 4
MEM_HEAD_DIM = D_MODEL // N_MEM_HEADS
D_FF = 11 * D_MODEL // 4
PAST_LEN = 1024
EPS = 1e-6
NEG_INF = -1e30

BF16 = jnp.bfloat16
F32 = jnp.float32

LANES = 128
SUBLANES = 8
HALF = LANES // 2
BAND = WINDOW + CHUNK
VMEM_LIMIT_BYTES = 60 * 1024 * 1024
FFN_CHUNK = 512
ATTN_BLOCK = 128

_NT = (((1,), (1,)), ((), ()))


def _dot(a, b):
    return jnp.dot(a, b, preferred_element_type=F32)


def _dot_nt(a, b):
    return lax.dot_general(a, b, _NT, preferred_element_type=F32)


def _rms_scale(x):
    return lax.rsqrt(jnp.mean(x * x, axis=-1, keepdims=True) + EPS)


def _aligned(index, multiple):
    return index if isinstance(index, int) else pl.multiple_of(index, multiple)


def _resident(shape):
    nd = len(shape)
    return pl.BlockSpec(shape, lambda *_: (0,) * nd, pipeline_mode=pl.Buffered(1))


def _causal_conv(cs_ref, prev8, cur, taps):
    rows = cur.shape[0]
    cs_ref[0:SUBLANES, :] = prev8
    cs_ref[SUBLANES:SUBLANES + rows, :] = cur
    x1 = cs_ref[pl.ds(SUBLANES - 1, rows), :]
    x2 = cs_ref[pl.ds(SUBLANES - 2, rows), :]
    return taps[0:1] * x2 + taps[1:2] * x1 + taps[2:3] * cur


def _memkv_kernel(mem_ref, g_ref, wk_ref, wv_ref, k32_ref, v32_ref, kbf_ref, vbf_ref):
    m = mem_ref[...]
    hm = (m * _rms_scale(m) * g_ref[...]).astype(BF16)
    k = _dot(hm, wk_ref[...])
    v = _dot(hm, wv_ref[...])
    for h in range(N_MEM_HEADS):
        k32_ref[:, h, :] = k[:, MEM_HEAD_DIM * h:MEM_HEAD_DIM * (h + 1)]
        v32_ref[:, h, :] = v[:, MEM_HEAD_DIM * h:MEM_HEAD_DIM * (h + 1)]
    kbf_ref[...] = k.astype(BF16)
    vbf_ref[...] = v.astype(BF16)


def _memory_kv(mem2d, g_mem, wk, wv):
    rows = mem2d.shape[0]
    tm = N_MEM
    out_w = wk.shape[1]
    row = lambda i: (i, 0)
    row3 = lambda i: (i, 0, 0)
    return pl.pallas_call(
        _memkv_kernel,
        grid=(rows // tm,),
        in_specs=[
            pl.BlockSpec((tm, D_MODEL), row),
            _resident((1, D_MODEL)),
            _resident((D_MODEL, out_w)),
            _resident((D_MODEL, out_w)),
        ],
        out_specs=[
            pl.BlockSpec((tm, N_MEM_HEADS, MEM_HEAD_DIM), row3),
            pl.BlockSpec((tm, N_MEM_HEADS, MEM_HEAD_DIM), row3),
            pl.BlockSpec((tm, out_w), row),
            pl.BlockSpec((tm, out_w), row),
        ],
        out_shape=[
            jax.ShapeDtypeStruct((rows, N_MEM_HEADS, MEM_HEAD_DIM), F32),
            jax.ShapeDtypeStruct((rows, N_MEM_HEADS, MEM_HEAD_DIM), F32),
            jax.ShapeDtypeStruct((rows, out_w), BF16),
            jax.ShapeDtypeStruct((rows, out_w), BF16),
        ],
        compiler_params=pltpu.CompilerParams(
            dimension_semantics=("arbitrary",), vmem_limit_bytes=VMEM_LIMIT_BYTES),
        name="memory_kv",
    )(mem2d, g_mem, wk, wv)


def _mixer_kernel(x_ref, cos_ref, s1_ref, s2_ref, kinit_ref, vinit_ref, cinit_ref,
                  gpre_ref, win_ref, convw_ref, gconv_ref, gattn_ref, sink_ref,
                  mix_ref, kout_ref, vout_ref, cstate_ref,
                  h_ref, yc_ref, q_ref, ke_ref, ko_ref, ve_ref, vo_ref, cs_ref, att_ref, ssq_ref,
                  *, tm, n_seg, carry, keep, conv_cb, ab):
    i = pl.program_id(1)
    seg = tm // n_seg
    nblk = seg // ab
    nkeys = ab + WINDOW
    lo = lax.broadcasted_iota(jnp.int32, (1, LANES), 1) < HALF

    def rope(blk, cos_t, s1_t, s2_t):
        return (blk * cos_t + pltpu.roll(blk, LANES - ROT_DIM // 2, 1) * s1_t
                + pltpu.roll(blk, ROT_DIM // 2, 1) * s2_t)

    def store_kv(kf, vf, s, row0):
        rows = kf.shape[0]
        for hp in range(N_KV_HEADS // 2):
            own = kf[:, LANES * hp:LANES * (hp + 1)]
            swp = pltpu.roll(own, HALF, 1)
            for par in range(2):
                h = 2 * hp + par
                low, high = (own, swp) if par == 0 else (swp, own)
                ke_ref[s, h, row0:row0 + rows, :] = jnp.where(lo, low, 0.0).astype(BF16)
                ko_ref[s, h, row0:row0 + rows, :] = jnp.where(lo, 0.0, high).astype(BF16)
        vt = vf.T.astype(BF16)
        zeros = jnp.zeros((HALF, rows), BF16)
        for h in range(N_KV_HEADS):
            vh = vt[HEAD_DIM * h:HEAD_DIM * (h + 1)]
            ve_ref[s, h, :, row0:row0 + rows] = jnp.concatenate([vh, zeros], axis=0)
            vo_ref[s, h, :, row0:row0 + rows] = jnp.concatenate([zeros, vh], axis=0)

    @pl.when(i == 0)
    def _():
        for s in range(n_seg):
            store_kv(kinit_ref[s], vinit_ref[s], s, 0)
            cstate_ref[s] = cinit_ref[s]

    x = x_ref[...]
    h_ref[...] = (x * _rms_scale(x) * gpre_ref[...]).astype(BF16)
    hb = h_ref[...]

    def conv_block(cb):
        hbl = h_ref[...]
        c0 = _aligned(cb * conv_cb, conv_cb)
        bg = _dot(hbl, win_ref[:, pl.ds(c0, conv_cb)])
        cg = _dot(hbl, win_ref[:, pl.ds(D_CONV + c0, conv_cb)])
        ug = _dot(hbl, win_ref[:, pl.ds(2 * D_CONV + c0, conv_cb)])
        cu = cg * ug
        taps = convw_ref[cb]
        ssq_parts = []
        for s in range(n_seg):
            r0 = s * seg
            cur = cu[r0:r0 + seg]
            y = bg[r0:r0 + seg] * _causal_conv(cs_ref, cstate_ref[s, cb], cur, taps)
            ssq_parts.append(jnp.sum(y * y, axis=-1, keepdims=True))
            yc_ref[cb, r0:r0 + seg, :] = y
            cstate_ref[s, cb] = cur[seg - SUBLANES:seg]
        return ssq_parts[0] if n_seg == 1 else jnp.concatenate(ssq_parts, axis=0)

    cos_t = cos_ref[...]
    s1_t = s1_ref[...]
    s2_t = s2_ref[...]
    q0 = 3 * D_CONV
    qf = _dot(hb, win_ref[:, q0:q0 + ATTN_WIDTH])
    for jb in range(ATTN_WIDTH // LANES):
        blk = rope(qf[:, LANES * jb:LANES * (jb + 1)], cos_t, s1_t, s2_t)
        q_ref[:, LANES * jb:LANES * (jb + 1)] = (blk * (HEAD_DIM ** -0.5)).astype(BF16)
    k0 = q0 + ATTN_WIDTH
    kv = _dot(hb, win_ref[:, k0:k0 + 2 * KV_WIDTH])
    kf = jnp.concatenate(
        [rope(kv[:, LANES * jb:LANES * (jb + 1)], cos_t, s1_t, s2_t)
         for jb in range(KV_WIDTH // LANES)], axis=1)
    vf = kv[:, KV_WIDTH:2 * KV_WIDTH]
    for s in range(n_seg):
        r0 = s * seg
        store_kv(kf[r0:r0 + seg], vf[r0:r0 + seg], s, WINDOW)
        kout_ref[s] = kf[r0 + seg - keep:r0 + seg]
        vout_ref[s] = vf[r0 + seg - keep:r0 + seg]

    key_pos = lax.broadcasted_iota(jnp.int32, (nkeys, 2 * ab), 0)
    if ab > CHUNK:
        tok = lax.broadcasted_iota(jnp.int32, (nkeys, 2 * ab), 1) % ab
        first_key = (tok // CHUNK) * CHUNK
        band_bias = jnp.where((key_pos >= first_key) & (key_pos < first_key + BAND), 0.0, NEG_INF)
    else:
        band_bias = None
    feat_lo = lax.broadcasted_iota(jnp.int32, (LANES, 1), 0) < HALF

    def block_place(idx):
        s = 0 if n_seg == 1 else idx // nblk
        j = 0 if nblk == 1 else idx % nblk
        return s, pl.ds(_aligned(idx * ab, ab), ab), pl.ds(_aligned(j * ab, ab), nkeys)

    def attend_block(idx):
        s, rows, band = block_place(idx)
        j = 0 if nblk == 1 else idx % nblk
        bias = band_bias
        if carry:
            first_valid = WINDOW - (i * seg + j * ab)
            start_bias = jnp.where(key_pos >= first_valid, 0.0, NEG_INF)
            bias = start_bias if bias is None else bias + start_bias
        heads = range(N_KV_HEADS)
        scores = []
        for h in heads:
            qa = jnp.concatenate(
                [q_ref[rows, 2 * LANES * h:2 * LANES * h + LANES],
                 q_ref[rows, 2 * LANES * h + LANES:2 * LANES * (h + 1)]], axis=0)
            st_e = _dot_nt(ke_ref[s, h, band, :], qa)
            st_o = _dot_nt(ko_ref[s, h, band, :], qa)
            if bias is not None:
                st_e = st_e + bias
                st_o = st_o + bias
            scores.append((st_e, st_o))
        stats = []
        for h in heads:
            sink_e = sink_ref[h, 0:1, :]
            sink_o = sink_ref[h, 1:2, :]
            m_e = jnp.maximum(jnp.max(scores[h][0], axis=0, keepdims=True), sink_e)
            m_o = jnp.maximum(jnp.max(scores[h][1], axis=0, keepdims=True), sink_o)
            stats.append((m_e, m_o, jnp.exp(sink_e - m_e), jnp.exp(sink_o - m_o)))
        for h in heads:
            p_e = jnp.exp(scores[h][0] - stats[h][0])
            p_o = jnp.exp(scores[h][1] - stats[h][1])
            den_e = jnp.sum(p_e, axis=0, keepdims=True) + stats[h][2]
            den_o = jnp.sum(p_o, axis=0, keepdims=True) + stats[h][3]
            acc_t = (_dot(ve_ref[s, h, :, band], p_e.astype(BF16))
                     + _dot(vo_ref[s, h, :, band], p_o.astype(BF16)))
            att_ref[h] = acc_t * jnp.where(feat_lo, 1.0 / den_e, 1.0 / den_o)

    def finish_block(idx):
        _, rows, _ = block_place(idx)
        outs = []
        for h in range(N_KV_HEADS):
            acc = att_ref[h].T
            outs += [acc[0:ab], acc[ab:2 * ab]]
        o = jnp.concatenate(outs, axis=1)
        mix_ref[rows, D_CONV:D_CONV + ATTN_WIDTH] = (
            o * _rms_scale(o) * gattn_ref[...]).astype(BF16)

    n_trips = n_seg * nblk
    for t in range(n_trips):
        @pl.when(i >= 0)
        def _(t=t):
            if t > 0:
                finish_block(t - 1)
            attend_block(t)
            ssq_ref[t] = conv_block(t)
    finish_block(n_trips - 1)
    ssq_c = ssq_ref[0]
    for t in range(1, n_trips):
        ssq_c = ssq_c + ssq_ref[t]

    if carry:
        for h in range(N_KV_HEADS):
            for ref in (ke_ref, ko_ref):
                ref[0, h, 0:WINDOW, :] = ref[0, h, seg:seg + WINDOW, :]
            for ref in (ve_ref, vo_ref):
                ref[0, h, :, 0:WINDOW] = ref[0, h, :, seg:seg + WINDOW]

    rs_c = lax.rsqrt(ssq_c * (1.0 / D_CONV) + EPS)
    for cb in range(D_CONV // conv_cb):
        mix_ref[:, cb * conv_cb:(cb + 1) * conv_cb] = (
            yc_ref[cb] * rs_c * gconv_ref[cb]).astype(BF16)


def _mixer(x2d, tabs, kinit, vinit, cinit, gpre, win, convw, gconv, gattn, sinks,
           *, n_batch, tm, n_seg, carry):
    rows = x2d.shape[0]
    nt = rows // (n_batch * tm)
    seg = tm // n_seg
    keep = min(WINDOW, seg)
    ab = min(ATTN_BLOCK, seg)
    n_trips = tm // ab
    conv_cb = D_CONV // n_trips
    row_blk = lambda b, i: (b * nt + i, 0)
    tab_blk = lambda b, i: (i, 0)
    seg_blk = lambda b, i: (b, 0, 0)
    cst_blk = pl.BlockSpec((n_seg, n_trips, SUBLANES, conv_cb), lambda b, i: (b, 0, 0, 0))
    kern = functools.partial(_mixer_kernel, tm=tm, n_seg=n_seg, carry=carry, keep=keep,
                             conv_cb=conv_cb, ab=ab)
    n_segs_total = n_batch * n_seg
    taps = jnp.swapaxes(convw.reshape(CONV_WIDTH, n_trips, conv_cb), 0, 1)
    gconv_b = gconv.reshape(n_trips, 1, conv_cb)
    cinit_b = jnp.swapaxes(cinit.reshape(n_segs_total, SUBLANES, n_trips, conv_cb), 1, 2)
    by_parity = jnp.swapaxes(sinks.reshape(N_KV_HEADS, 2, 2), 1, 2)[:, :, :, None]
    sink_tab = jnp.broadcast_to(by_parity, (N_KV_HEADS, 2, 2, ab)).reshape(N_KV_HEADS, 2, 2 * ab)
    sink_tab = jnp.pad(sink_tab, ((0, 0), (0, SUBLANES - 2), (0, 0)))
    mix, k_tail, v_tail, cstate = pl.pallas_call(
        kern,
        grid=(n_batch, nt),
        in_specs=[
            pl.BlockSpec((tm, D_MODEL), row_blk),
            pl.BlockSpec((tm, LANES), tab_blk),
            pl.BlockSpec((tm, LANES), tab_blk),
            pl.BlockSpec((tm, LANES), tab_blk),
            pl.BlockSpec((n_seg, WINDOW, KV_WIDTH), seg_blk),
            pl.BlockSpec((n_seg, WINDOW, KV_WIDTH), seg_blk),
            cst_blk,
            _resident((1, D_MODEL)),
            _resident(win.shape),
            _resident(taps.shape),
            _resident(gconv_b.shape),
            _resident((1, ATTN_WIDTH)),
            _resident((N_KV_HEADS, SUBLANES, 2 * ab)),
        ],
        out_specs=[
            pl.BlockSpec((tm, D_CONV + ATTN_WIDTH), row_blk),
            pl.BlockSpec((n_seg, keep, KV_WIDTH), seg_blk),
            pl.BlockSpec((n_seg, keep, KV_WIDTH), seg_blk),
            cst_blk,
        ],
        out_shape=[
            jax.ShapeDtypeStruct((rows, D_CONV + ATTN_WIDTH), BF16),
            jax.ShapeDtypeStruct((n_segs_total, keep, KV_WIDTH), F32),
            jax.ShapeDtypeStruct((n_segs_total, keep, KV_WIDTH), F32),
            jax.ShapeDtypeStruct((n_segs_total, n_trips, SUBLANES, conv_cb), F32),
        ],
        scratch_shapes=[
            pltpu.VMEM((tm, D_MODEL), BF16),
            pltpu.VMEM((n_trips, tm, conv_cb), F32),
            pltpu.VMEM((tm, ATTN_WIDTH), BF16),
            pltpu.VMEM((n_seg, N_KV_HEADS, WINDOW + seg, LANES), BF16),
            pltpu.VMEM((n_seg, N_KV_HEADS, WINDOW + seg, LANES), BF16),
            pltpu.VMEM((n_seg, N_KV_HEADS, LANES, WINDOW + seg), BF16),
            pltpu.VMEM((n_seg, N_KV_HEADS, LANES, WINDOW + seg), BF16),
            pltpu.VMEM((SUBLANES + seg, conv_cb), F32),
            pltpu.VMEM((N_KV_HEADS, LANES, 2 * ab), F32),
            pltpu.VMEM((n_trips, tm, 1), F32),
        ],
        compiler_params=pltpu.CompilerParams(
            dimension_semantics=("arbitrary", "arbitrary"), vmem_limit_bytes=VMEM_LIMIT_BYTES),
        name="mixer",
    )(x2d, *tabs, kinit, vinit, cinit_b, gpre, win, taps, gconv_b, gattn, sink_tab)
    cstate = jnp.swapaxes(cstate, 1, 2).reshape(n_segs_total, SUBLANES, D_CONV)
    return mix, k_tail, v_tail, cstate


def _mix_out_kernel(x_ref, mix_ref, wout_ref, gpost_ref, out_ref):
    half = x_ref.shape[0] // 2
    for r0 in (0, half):
        y = _dot(mix_ref[r0:r0 + half, :], wout_ref[...])
        out_ref[r0:r0 + half, :] = x_ref[r0:r0 + half, :] + y * _rms_scale(y) * gpost_ref[...]


def _mix_out(x2d, mix, wout, gpost, *, tm):
    rows = x2d.shape[0]
    row_blk = lambda i: (i, 0)
    width = mix.shape[1]
    return pl.pallas_call(
        _mix_out_kernel,
        grid=(rows // tm,),
        in_specs=[
            pl.BlockSpec((tm, D_MODEL), row_blk),
            pl.BlockSpec((tm, width), row_blk),
            _resident((width, D_MODEL)),
            _resident((1, D_MODEL)),
        ],
        out_specs=pl.BlockSpec((tm, D_MODEL), row_blk),
        out_shape=jax.ShapeDtypeStruct((rows, D_MODEL), F32),
        compiler_params=pltpu.CompilerParams(
            dimension_semantics=("arbitrary",), vmem_limit_bytes=VMEM_LIMIT_BYTES),
        name="mix_out",
    )(x2d, mix, wout, gpost)


def _cross_kernel(x_ref, mk_ref, mv_ref, gpre_ref, wq_ref, wo_ref, gpost_ref, gnext_ref,
                  out_ref, hnext_ref, o_ref, *, tm, n_seg):
    seg = tm // n_seg
    x = x_ref[...]
    hb = (x * _rms_scale(x) * gpre_ref[...]).astype(BF16)
    for hd in range(N_MEM_HEADS):
        c0 = hd * MEM_HEAD_DIM
        q = _dot(hb, wq_ref[:, c0:c0 + MEM_HEAD_DIM]).astype(BF16)
        for s in range(n_seg):
            r0 = s * seg
            sc = _dot_nt(q[r0:r0 + seg], mk_ref[s, :, c0:c0 + MEM_HEAD_DIM]) * (MEM_HEAD_DIM ** -0.5)
            p = jnp.exp(sc - jnp.max(sc, axis=-1, keepdims=True))
            den = jnp.sum(p, axis=-1, keepdims=True)
            o = _dot(p.astype(BF16), mv_ref[s, :, c0:c0 + MEM_HEAD_DIM]) * (1.0 / den)
            o_ref[r0:r0 + seg, c0:c0 + MEM_HEAD_DIM] = o.astype(BF16)
    half = tm // 2
    for r0 in (0, half):
        y = _dot(o_ref[r0:r0 + half, :], wo_ref[...])
        x_out = x[r0:r0 + half] + y * _rms_scale(y) * gpost_ref[...]
        out_ref[r0:r0 + half, :] = x_out
        hnext_ref[r0:r0 + half, :] = (x_out * _rms_scale(x_out) * gnext_ref[...]).astype(BF16)


def _cross(x2d, mk, mv, gpre, wq, wo, gpost, gnext, *, n_batch, tm, n_seg):
    rows = x2d.shape[0]
    nt = rows // (n_batch * tm)
    row_blk = lambda b, i: (b * nt + i, 0)
    if n_batch == 1:
        mem_blk = _resident((n_seg, N_MEM, D_MODEL))
    else:
        mem_blk = pl.BlockSpec((n_seg, N_MEM, D_MODEL), lambda b, i: (b, 0, 0))
    kern = functools.partial(_cross_kernel, tm=tm, n_seg=n_seg)
    return pl.pallas_call(
        kern,
        grid=(n_batch, nt),
        in_specs=[
            pl.BlockSpec((tm, D_MODEL), row_blk),
            mem_blk, mem_blk,
            _resident((1, D_MODEL)),
            _resident((D_MODEL, D_MODEL)),
            _resident((D_MODEL, D_MODEL)),
            _resident((1, D_MODEL)),
            _resident((1, D_MODEL)),
        ],
        out_specs=[pl.BlockSpec((tm, D_MODEL), row_blk), pl.BlockSpec((tm, D_MODEL), row_blk)],
        out_shape=[jax.ShapeDtypeStruct((rows, D_MODEL), F32),
                   jax.ShapeDtypeStruct((rows, D_MODEL), BF16)],
        scratch_shapes=[pltpu.VMEM((tm, D_MODEL), BF16)],
        compiler_params=pltpu.CompilerParams(
            dimension_semantics=("arbitrary", "arbitrary"), vmem_limit_bytes=VMEM_LIMIT_BYTES),
        name="cross_attn",
    )(x2d, mk, mv, gpre, wq, wo, gpost, gnext)


def _ffn_kernel(h_ref, x_hbm, ginit_ref, wg_ref, wu_ref, cw_ref, cb_ref, wd_ref, gpost_ref,
                out_ref, gstate_ref,
                gc_ref, cs_ref, xbuf_ref, xsem, *, tm, n_seg, fc, nt, nf):
    b = pl.program_id(0)
    i = pl.program_id(1)
    f = pl.program_id(2)
    seg = tm // n_seg

    def residual_copy():
        row0 = pl.multiple_of((b * nt + i) * tm, tm)
        return pltpu.make_async_copy(x_hbm.at[pl.ds(row0, tm)], xbuf_ref, xsem)

    @pl.when(f == 0)
    def _():
        out_ref[...] = jnp.zeros_like(out_ref)

    @pl.when(f == max(nf - 3, 0))
    def _():
        residual_copy().start()

    @pl.when(i == 0)
    def _():
        gc_ref[f] = ginit_ref[...]

    hb = h_ref[...]
    g = _dot(hb, wg_ref[...])
    u = _dot(hb, wu_ref[...])
    acts = []
    for s in range(n_seg):
        r0 = s * seg
        cur = g[r0:r0 + seg]
        a = _causal_conv(cs_ref, gc_ref[f, s], cur, cw_ref[...]) + cb_ref[...]
        acts.append((a * (1.0 / (1.0 + jnp.exp(-a))) * u[r0:r0 + seg]).astype(BF16))
        last = cur[seg - SUBLANES:seg]
        gc_ref[f, s] = last
        gstate_ref[s, f] = last
    act = acts[0] if n_seg == 1 else jnp.concatenate(acts, axis=0)
    out_ref[...] += _dot(act, wd_ref[...])

    @pl.when(f == nf - 1)
    def _():
        residual_copy().wait()
        y = out_ref[...]
        out_ref[...] = xbuf_ref[...] + y * _rms_scale(y) * gpost_ref[...]


def _ffn(h2d, x2d, ginit, wg, wu, cw, cb, wd, gpost, *, n_batch, tm, n_seg, fc):
    rows = x2d.shape[0]
    nt = rows // (n_batch * tm)
    nf = D_FF // fc
    seg = tm // n_seg
    row_blk = lambda b, i, f: (b * nt + i, 0)
    st_blk = pl.BlockSpec((n_seg, SUBLANES, fc), lambda b, i, f: (b, 0, f))
    kern = functools.partial(_ffn_kernel, tm=tm, n_seg=n_seg, fc=fc, nt=nt, nf=nf)
    return pl.pallas_call(
        kern,
        grid=(n_batch, nt, nf),
        in_specs=[
            pl.BlockSpec((tm, D_MODEL), row_blk),
            pl.BlockSpec(memory_space=pl.ANY),
            st_blk,
            pl.BlockSpec((D_MODEL, fc), lambda b, i, f: (0, f)),
            pl.BlockSpec((D_MODEL, fc), lambda b, i, f: (0, f)),
            pl.BlockSpec((CONV_WIDTH, fc), lambda b, i, f: (0, f)),
            pl.BlockSpec((1, fc), lambda b, i, f: (0, f)),
            pl.BlockSpec((fc, D_MODEL), lambda b, i, f: (f, 0)),
            _resident((1, D_MODEL)),
        ],
        out_specs=[
            pl.BlockSpec((tm, D_MODEL), row_blk),
            pl.BlockSpec((n_seg, nf, SUBLANES, fc), lambda b, i, f: (b, 0, 0, 0)),
        ],
        out_shape=[
            jax.ShapeDtypeStruct((rows, D_MODEL), F32),
            jax.ShapeDtypeStruct((n_batch * n_seg, nf, SUBLANES, fc), F32),
        ],
        scratch_shapes=[
            pltpu.VMEM((nf, n_seg, SUBLANES, fc), F32),
            pltpu.VMEM((SUBLANES + seg, fc), F32),
            pltpu.VMEM((tm, D_MODEL), F32),
            pltpu.SemaphoreType.DMA(()),
        ],
        compiler_params=pltpu.CompilerParams(
            dimension_semantics=("arbitrary", "arbitrary", "arbitrary"),
            vmem_limit_bytes=VMEM_LIMIT_BYTES),
        name="conv_ffn",
    )(h2d, x2d, ginit, wg, wu, cw, cb, wd, gpost)


def _rope_tables(pos):
    half = ROT_DIM // 2
    inv = ROPE_THETA ** (-jnp.arange(0, ROT_DIM, 2, dtype=F32) / ROT_DIM)
    ang = pos[:, None] * inv[None, :]
    cos = jnp.tile(jnp.cos(ang), (1, LANES // half))
    sin = jnp.tile(jnp.sin(ang), (1, LANES // half))
    dim = jnp.arange(LANES) % HEAD_DIM
    first, second = (dim < half)[None, :], ((dim >= half) & (dim < ROT_DIM))[None, :]
    return (jnp.where(first | second, cos, 1.0), jnp.where(first, -sin, 0.0),
            jnp.where(second, sin, 0.0))


def _merge_heads_bf16(mem):
    n, m, heads, d = mem.shape
    return mem.reshape(n, m, heads * d).astype(BF16)


def _pad_state(state):
    return jnp.pad(state, ((0, 0), (SUBLANES - (CONV_WIDTH - 1), 0), (0, 0)))


def _layer(x2d, tabs, kinit, vinit, cinit, finit, mk, mv, w, *, mixer_cfg, mix_out_tm, cross_cfg,
           ffn_cfg, carry):
    tile = lambda cfg: dict(n_batch=cfg[0], tm=cfg[1], n_seg=cfg[2])
    mix, k_new, v_new, cstate = _mixer(
        x2d, tabs, kinit, vinit, cinit, w["g_mix_pre"], w["w_mix_in"], w["conv_mix_w"],
        w["g_grp_conv"], w["g_grp_attn"], w["sinks"], carry=carry, **tile(mixer_cfg))
    x1 = _mix_out(x2d, mix, w["w_mix_out"], w["g_mix_post"], tm=mix_out_tm)
    x2, h_ffn = _cross(x1, mk, mv, w["g_x_pre"], w["w_xq"], w["w_xo"], w["g_x_post"],
                       w["g_ffn_pre"], **tile(cross_cfg))
    x3, fstate = _ffn(h_ffn, x2, finit, w["w_gate"], w["w_up"], w["conv_ffn_w"],
                      w["conv_ffn_b"], w["w_down"], w["g_ffn_post"], fc=FFN_CHUNK,
                      **tile(ffn_cfg))
    fstate = jnp.swapaxes(fstate, 1, 2).reshape(fstate.shape[0], SUBLANES, D_FF)
    return x3, k_new, v_new, cstate, fstate


def kernel(x_prompt, x_sample, cache_mem_k, cache_mem_v, cache_swa_k, cache_swa_v,
           state_mix_conv, state_ffn_conv, mem_prompt,
           g_mix_pre, w_mix_in, conv_mix_w, g_grp_conv, g_grp_attn, attn_sinks,
           w_mix_out, g_mix_post, g_mem, w_xk, w_xv, g_x_pre, w_xq, w_xo, g_x_post,
           g_ffn_pre, w_gate, w_up, conv_ffn_w, conv_ffn_b, w_down, g_ffn_post):
    n_p, s_p, _ = x_prompt.shape
    n_s, s_s, _ = x_sample.shape
    depth = w_mix_in.shape[0]
    swa_len = cache_swa_k.shape[2]
    tabs_p = _rope_tables(jnp.arange(s_p, dtype=F32))
    tabs_s = tuple(jnp.tile(a, (n_s // 2, 1))
                   for a in _rope_tables(PAST_LEN + jnp.arange(s_s, dtype=F32)))

    yp = x_prompt.reshape(n_p * s_p, D_MODEL)
    ys = x_sample.reshape(n_s * s_s, D_MODEL)
    outs = [[] for _ in range(10)]
    for l in range(depth):
        row = lambda a: a[l][None, :]
        w = {
            "sinks": attn_sinks[l],
            "g_mix_pre": row(g_mix_pre), "w_mix_in": w_mix_in[l].astype(BF16),
            "conv_mix_w": conv_mix_w[l], "g_grp_conv": row(g_grp_conv),
            "g_grp_attn": row(g_grp_attn),
            "w_mix_out": w_mix_out[l].astype(BF16), "g_mix_post": row(g_mix_post),
            "g_x_pre": row(g_x_pre), "w_xq": w_xq[l].astype(BF16), "w_xo": w_xo[l].astype(BF16),
            "g_x_post": row(g_x_post), "g_ffn_pre": row(g_ffn_pre),
            "w_gate": w_gate[l].astype(BF16), "w_up": w_up[l].astype(BF16),
            "conv_ffn_w": conv_ffn_w[l], "conv_ffn_b": row(conv_ffn_b),
            "w_down": w_down[l].astype(BF16), "g_ffn_post": row(g_ffn_post),
        }
        mk32, mv32, mkb, mvb = _memory_kv(
            mem_prompt.reshape(n_p * N_MEM, D_MODEL), row(g_mem),
            w_xk[l].astype(BF16), w_xv[l].astype(BF16))
        yp, kp, vp, cp, fp = _layer(
            yp, tabs_p,
            jnp.zeros((n_p, WINDOW, KV_WIDTH), F32), jnp.zeros((n_p, WINDOW, KV_WIDTH), F32),
            jnp.zeros((n_p, SUBLANES, D_CONV), F32), jnp.zeros((n_p, SUBLANES, D_FF), F32),
            mkb.reshape(n_p, N_MEM, D_MODEL), mvb.reshape(n_p, N_MEM, D_MODEL), w,
            mixer_cfg=(n_p, 512, 1), mix_out_tm=1024, cross_cfg=(n_p, 512, 1),
            ffn_cfg=(n_p, 1024, 1), carry=True)
        ys, ks, vs, cs, fs = _layer(
            ys, tabs_s,
            cache_swa_k[l].reshape(n_s, swa_len, KV_WIDTH),
            cache_swa_v[l].reshape(n_s, swa_len, KV_WIDTH),
            _pad_state(state_mix_conv[l]), _pad_state(state_ffn_conv[l]),
            _merge_heads_bf16(cache_mem_k[l]), _merge_heads_bf16(cache_mem_v[l]), w,
            mixer_cfg=(2, n_s * s_s // 2, n_s // 2), mix_out_tm=n_s * s_s,
            cross_cfg=(1, n_s * s_s, n_s),
            ffn_cfg=(1, n_s * s_s, n_s), carry=False)
        keep_p = min(WINDOW, s_p)
        tail = CONV_WIDTH - 1
        new_k = ks.reshape(n_s, s_s, N_KV_HEADS, HEAD_DIM)
        new_v = vs.reshape(n_s, s_s, N_KV_HEADS, HEAD_DIM)
        layer_out = (
            mk32.reshape(n_p, N_MEM, N_MEM_HEADS, MEM_HEAD_DIM),
            mv32.reshape(n_p, N_MEM, N_MEM_HEADS, MEM_HEAD_DIM),
            kp.reshape(n_p, keep_p, N_KV_HEADS, HEAD_DIM),
            vp.reshape(n_p, keep_p, N_KV_HEADS, HEAD_DIM),
            cp[:, SUBLANES - tail:], fp[:, SUBLANES - tail:],
            jnp.concatenate([cache_swa_k[l], new_k], axis=1)[:, -swa_len:],
            jnp.concatenate([cache_swa_v[l], new_v], axis=1)[:, -swa_len:],
            cs[:, SUBLANES - tail:], fs[:, SUBLANES - tail:],
        )
        for acc, o in zip(outs, layer_out):
            acc.append(o)

    return (yp.reshape(n_p, s_p, D_MODEL), ys.reshape(n_s, s_s, D_MODEL),
            *[jnp.stack(o) for o in outs])
```

```python
import functools

import jax
import jax.numpy as jnp
from jax import lax
from jax.experimental import pallas as pl
from jax.experimental.pallas import tpu as pltpu

D_MODEL = 2048
CHUNK = 64
WINDOW = 128
CONV_WIDTH = 3
HEAD_DIM = 64
N_Q_HEADS = 16
N_KV_HEADS = 4
ATTN_WIDTH = N_Q_HEADS * HEAD_DIM
KV_WIDTH = N_KV_HEADS * HEAD_DIM
D_CONV = D_MODEL // 2
ROT_DIM = HEAD_DIM // 4
ROPE_THETA = 500000.0
N_MEM = 256
N_MEM_HEADS = 4
MEM_HEAD_DIM = D_MODEL // N_MEM_HEADS
D_FF = 11 * D_MODEL // 4
PAST_LEN = 1024
EPS = 1e-6
NEG_INF = -1e30

BF16 = jnp.bfloat16
F32 = jnp.float32

LANES = 128
SUBLANES = 8
HALF = LANES // 2
BAND = WINDOW + CHUNK
VMEM_LIMIT_BYTES = 60 * 1024 * 1024
FFN_CHUNK = 512
ATTN_BLOCK = 128

_NT = (((1,), (1,)), ((), ()))


def _dot(a, b):
    return jnp.dot(a, b, preferred_element_type=F32)


def _dot_nt(a, b):
    return lax.dot_general(a, b, _NT, preferred_element_type=F32)


def _rms_scale(x):
    return lax.rsqrt(jnp.mean(x * x, axis=-1, keepdims=True) + EPS)


def _aligned(index, multiple):
    return index if isinstance(index, int) else pl.multiple_of(index, multiple)


def _resident(shape):
    nd = len(shape)
    return pl.BlockSpec(shape, lambda *_: (0,) * nd, pipeline_mode=pl.Buffered(1))


def _causal_conv(cs_ref, prev8, cur, taps):
    rows = cur.shape[0]
    cs_ref[0:SUBLANES, :] = prev8
    cs_ref[SUBLANES:SUBLANES + rows, :] = cur
    x1 = cs_ref[pl.ds(SUBLANES - 1, rows), :]
    x2 = cs_ref[pl.ds(SUBLANES - 2, rows), :]
    return taps[0:1] * x2 + taps[1:2] * x1 + taps[2:3] * cur


def _memkv_kernel(mem_ref, g_ref, wk_ref, wv_ref, k32_ref, v32_ref, kbf_ref, vbf_ref):
    m = mem_ref[...]
    hm = (m * _rms_scale(m) * g_ref[...]).astype(BF16)
    k = _dot(hm, wk_ref[...])
    v = _dot(hm, wv_ref[...])
    for h in range(N_MEM_HEADS):
        k32_ref[:, h, :] = k[:, MEM_HEAD_DIM * h:MEM_HEAD_DIM * (h + 1)]
        v32_ref[:, h, :] = v[:, MEM_HEAD_DIM * h:MEM_HEAD_DIM * (h + 1)]
    kbf_ref[...] = k.astype(BF16)
    vbf_ref[...] = v.astype(BF16)


def _memory_kv(mem2d, g_mem, wk, wv):
    rows = mem2d.shape[0]
    tm = N_MEM
    out_w = wk.shape[1]
    row = lambda i: (i, 0)
    row3 = lambda i: (i, 0, 0)
    return pl.pallas_call(
        _memkv_kernel,
        grid=(rows // tm,),
        in_specs=[
            pl.BlockSpec((tm, D_MODEL), row),
            _resident((1, D_MODEL)),
            _resident((D_MODEL, out_w)),
            _resident((D_MODEL, out_w)),
        ],
        out_specs=[
            pl.BlockSpec((tm, N_MEM_HEADS, MEM_HEAD_DIM), row3),
            pl.BlockSpec((tm, N_MEM_HEADS, MEM_HEAD_DIM), row3),
            pl.BlockSpec((tm, out_w), row),
            pl.BlockSpec((tm, out_w), row),
        ],
        out_shape=[
            jax.ShapeDtypeStruct((rows, N_MEM_HEADS, MEM_HEAD_DIM), F32),
            jax.ShapeDtypeStruct((rows, N_MEM_HEADS, MEM_HEAD_DIM), F32),
            jax.ShapeDtypeStruct((rows, out_w), BF16),
            jax.ShapeDtypeStruct((rows, out_w), BF16),
        ],
        compiler_params=pltpu.CompilerParams(
            dimension_semantics=("arbitrary",), vmem_limit_bytes=VMEM_LIMIT_BYTES),
        name="memory_kv",
    )(mem2d, g_mem, wk, wv)


def _mixer_kernel(x_ref, cos_ref, s1_ref, s2_ref, kinit_ref, vinit_ref, cinit_ref,
                  gpre_ref, win_ref, convw_ref, gconv_ref, gattn_ref, sink_ref,
                  mix_ref, kout_ref, vout_ref, cstate_ref,
                  h_ref, yc_ref, q_ref, ke_ref, ko_ref, ve_ref, vo_ref, cs_ref, ssq_ref,
                  *, tm, n_seg, carry, keep, conv_cb, ab):
    i = pl.program_id(1)
    seg = tm // n_seg
    nblk = seg // ab
    nkeys = ab + WINDOW
    lo = lax.broadcasted_iota(jnp.int32, (1, LANES), 1) < HALF

    def rope(blk, cos_t, s1_t, s2_t):
        return (blk * cos_t + pltpu.roll(blk, LANES - ROT_DIM // 2, 1) * s1_t
                + pltpu.roll(blk, ROT_DIM // 2, 1) * s2_t)

    def store_kv(kf, vf, s, row0):
        rows = kf.shape[0]
        for hp in range(N_KV_HEADS // 2):
            own = kf[:, LANES * hp:LANES * (hp + 1)]
            swp = pltpu.roll(own, HALF, 1)
            for par in range(2):
                h = 2 * hp + par
                low, high = (own, swp) if par == 0 else (swp, own)
                ke_ref[s, h, row0:row0 + rows, :] = jnp.where(lo, low, 0.0).astype(BF16)
                ko_ref[s, h, row0:row0 + rows, :] = jnp.where(lo, 0.0, high).astype(BF16)
        vt = vf.T.astype(BF16)
        zeros = jnp.zeros((HALF, rows), BF16)
        for h in range(N_KV_HEADS):
            vh = vt[HEAD_DIM * h:HEAD_DIM * (h + 1)]
            ve_ref[s, h, :, row0:row0 + rows] = jnp.concatenate([vh, zeros], axis=0)
            vo_ref[s, h, :, row0:row0 + rows] = jnp.concatenate([zeros, vh], axis=0)

    @pl.when(i == 0)
    def _():
        for s in range(n_seg):
            store_kv(kinit_ref[s], vinit_ref[s], s, 0)
            cstate_ref[s] = cinit_ref[s]

    x = x_ref[...]
    h_ref[...] = (x * _rms_scale(x) * gpre_ref[...]).astype(BF16)
    hb = h_ref[...]

    def conv_block(cb):
        hbl = h_ref[...]
        c0 = _aligned(cb * conv_cb, conv_cb)
        bg = _dot(hbl, win_ref[:, pl.ds(c0, conv_cb)])
        cg = _dot(hbl, win_ref[:, pl.ds(D_CONV + c0, conv_cb)])
        ug = _dot(hbl, win_ref[:, pl.ds(2 * D_CONV + c0, conv_cb)])
        cu = cg * ug
        taps = convw_ref[cb]
        ssq_parts = []
        for s in range(n_seg):
            r0 = s * seg
            cur = cu[r0:r0 + seg]
            y = bg[r0:r0 + seg] * _causal_conv(cs_ref, cstate_ref[s, cb], cur, taps)
            ssq_parts.append(jnp.sum(y * y, axis=-1, keepdims=True))
            yc_ref[cb, r0:r0 + seg, :] = y
            cstate_ref[s, cb] = cur[seg - SUBLANES:seg]
        return ssq_parts[0] if n_seg == 1 else jnp.concatenate(ssq_parts, axis=0)

    cos_t = cos_ref[...]
    s1_t = s1_ref[...]
    s2_t = s2_ref[...]
    q0 = 3 * D_CONV
    qf = _dot(hb, win_ref[:, q0:q0 + ATTN_WIDTH])
    for jb in range(ATTN_WIDTH // LANES):
        blk = rope(qf[:, LANES * jb:LANES * (jb + 1)], cos_t, s1_t, s2_t)
        q_ref[:, LANES * jb:LANES * (jb + 1)] = (blk * (HEAD_DIM ** -0.5)).astype(BF16)
    k0 = q0 + ATTN_WIDTH
    kv = _dot(hb, win_ref[:, k0:k0 + 2 * KV_WIDTH])
    kf = jnp.concatenate(
        [rope(kv[:, LANES * jb:LANES * (jb + 1)], cos_t, s1_t, s2_t)
         for jb in range(KV_WIDTH // LANES)], axis=1)
    vf = kv[:, KV_WIDTH:2 * KV_WIDTH]
    for s in range(n_seg):
        r0 = s * seg
        store_kv(kf[r0:r0 + seg], vf[r0:r0 + seg], s, WINDOW)
        kout_ref[s] = kf[r0 + seg - keep:r0 + seg]
        vout_ref[s] = vf[r0 + seg - keep:r0 + seg]

    key_pos = lax.broadcasted_iota(jnp.int32, (nkeys, 2 * ab), 0)
    if ab > CHUNK:
        tok = lax.broadcasted_iota(jnp.int32, (nkeys, 2 * ab), 1) % ab
        first_key = (tok // CHUNK) * CHUNK
        band_bias = jnp.where((key_pos >= first_key) & (key_pos < first_key + BAND), 0.0, NEG_INF)
    else:
        band_bias = None
    feat_lo = lax.broadcasted_iota(jnp.int32, (LANES, 1), 0) < HALF

    def block_place(idx):
        s = 0 if n_seg == 1 else idx // nblk
        j = 0 if nblk == 1 else idx % nblk
        return s, pl.ds(_aligned(idx * ab, ab), ab), pl.ds(_aligned(j * ab, ab), nkeys)

    def attend_block(idx):
        s, rows, band = block_place(idx)
        j = 0 if nblk == 1 else idx % nblk
        bias = band_bias
        if carry:
            first_valid = WINDOW - (i * seg + j * ab)
            start_bias = jnp.where(key_pos >= first_valid, 0.0, NEG_INF)
            bias = start_bias if bias is None else bias + start_bias
        heads = range(N_KV_HEADS)
        scores = []
        for h in heads:
            qa = jnp.concatenate(
                [q_ref[rows, 2 * LANES * h:2 * LANES * h + LANES],
                 q_ref[rows, 2 * LANES * h + LANES:2 * LANES * (h + 1)]], axis=0)
            st_e = _dot_nt(ke_ref[s, h, band, :], qa)
            st_o = _dot_nt(ko_ref[s, h, band, :], qa)
            if bias is not None:
                st_e = st_e + bias
                st_o = st_o + bias
            scores.append((st_e, st_o))
        stats = []
        for h in heads:
            sink_e = sink_ref[h, 0:1, :]
            sink_o = sink_ref[h, 1:2, :]
            m_e = jnp.maximum(jnp.max(scores[h][0], axis=0, keepdims=True), sink_e)
            m_o = jnp.maximum(jnp.max(scores[h][1], axis=0, keepdims=True), sink_o)
            stats.append((m_e, m_o, jnp.exp(sink_e - m_e), jnp.exp(sink_o - m_o)))
        outs = []
        for h in heads:
            p_e = jnp.exp(scores[h][0] - stats[h][0])
            p_o = jnp.exp(scores[h][1] - stats[h][1])
            den_e = jnp.sum(p_e, axis=0, keepdims=True) + stats[h][2]
            den_o = jnp.sum(p_o, axis=0, keepdims=True) + stats[h][3]
            acc_t = (_dot(ve_ref[s, h, :, band], p_e.astype(BF16))
                     + _dot(vo_ref[s, h, :, band], p_o.astype(BF16)))
            acc = (acc_t * jnp.where(feat_lo, 1.0 / den_e, 1.0 / den_o)).T
            outs += [acc[0:ab], acc[ab:2 * ab]]
        o = jnp.concatenate(outs, axis=1)
        mix_ref[rows, D_CONV:D_CONV + ATTN_WIDTH] = (
            o * _rms_scale(o) * gattn_ref[...]).astype(BF16)

    n_trips = n_seg * nblk
    for t in range(n_trips):
        @pl.when(i >= 0)
        def _(t=t):
            attend_block(t)
            ssq_ref[t] = conv_block(t)
    ssq_c = ssq_ref[0]
    for t in range(1, n_trips):
        ssq_c = ssq_c + ssq_ref[t]

    if carry:
        for h in range(N_KV_HEADS):
            for ref in (ke_ref, ko_ref):
                ref[0, h, 0:WINDOW, :] = ref[0, h, seg:seg + WINDOW, :]
            for ref in (ve_ref, vo_ref):
                ref[0, h, :, 0:WINDOW] = ref[0, h, :, seg:seg + WINDOW]

    rs_c = lax.rsqrt(ssq_c * (1.0 / D_CONV) + EPS)
    for cb in range(D_CONV // conv_cb):
        mix_ref[:, cb * conv_cb:(cb + 1) * conv_cb] = (
            yc_ref[cb] * rs_c * gconv_ref[cb]).astype(BF16)


def _mixer(x2d, tabs, kinit, vinit, cinit, gpre, win, convw, gconv, gattn, sinks,
           *, n_batch, tm, n_seg, carry):
    rows = x2d.shape[0]
    nt = rows // (n_batch * tm)
    seg = tm // n_seg
    keep = min(WINDOW, seg)
    ab = min(ATTN_BLOCK, seg)
    n_trips = tm // ab
    conv_cb = D_CONV // n_trips
    row_blk = lambda b, i: (b * nt + i, 0)
    tab_blk = lambda b, i: (i, 0)
    seg_blk = lambda b, i: (b, 0, 0)
    cst_blk = pl.BlockSpec((n_seg, n_trips, SUBLANES, conv_cb), lambda b, i: (b, 0, 0, 0))
    kern = functools.partial(_mixer_kernel, tm=tm, n_seg=n_seg, carry=carry, keep=keep,
                             conv_cb=conv_cb, ab=ab)
    n_segs_total = n_batch * n_seg
    taps = jnp.swapaxes(convw.reshape(CONV_WIDTH, n_trips, conv_cb), 0, 1)
    gconv_b = gconv.reshape(n_trips, 1, conv_cb)
    cinit_b = jnp.swapaxes(cinit.reshape(n_segs_total, SUBLANES, n_trips, conv_cb), 1, 2)
    by_parity = jnp.swapaxes(sinks.reshape(N_KV_HEADS, 2, 2), 1, 2)[:, :, :, None]
    sink_tab = jnp.broadcast_to(by_parity, (N_KV_HEADS, 2, 2, ab)).reshape(N_KV_HEADS, 2, 2 * ab)
    sink_tab = jnp.pad(sink_tab, ((0, 0), (0, SUBLANES - 2), (0, 0)))
    mix, k_tail, v_tail, cstate = pl.pallas_call(
        kern,
        grid=(n_batch, nt),
        in_specs=[
            pl.BlockSpec((tm, D_MODEL), row_blk),
            pl.BlockSpec((tm, LANES), tab_blk),
            pl.BlockSpec((tm, LANES), tab_blk),
            pl.BlockSpec((tm, LANES), tab_blk),
            pl.BlockSpec((n_seg, WINDOW, KV_WIDTH), seg_blk),
            pl.BlockSpec((n_seg, WINDOW, KV_WIDTH), seg_blk),
            cst_blk,
            _resident((1, D_MODEL)),
            _resident(win.shape),
            _resident(taps.shape),
            _resident(gconv_b.shape),
            _resident((1, ATTN_WIDTH)),
            _resident((N_KV_HEADS, SUBLANES, 2 * ab)),
        ],
        out_specs=[
            pl.BlockSpec((tm, D_CONV + ATTN_WIDTH), row_blk),
            pl.BlockSpec((n_seg, keep, KV_WIDTH), seg_blk),
            pl.BlockSpec((n_seg, keep, KV_WIDTH), seg_blk),
            cst_blk,
        ],
        out_shape=[
            jax.ShapeDtypeStruct((rows, D_CONV + ATTN_WIDTH), BF16),
            jax.ShapeDtypeStruct((n_segs_total, keep, KV_WIDTH), F32),
            jax.ShapeDtypeStruct((n_segs_total, keep, KV_WIDTH), F32),
            jax.ShapeDtypeStruct((n_segs_total, n_trips, SUBLANES, conv_cb), F32),
        ],
        scratch_shapes=[
            pltpu.VMEM((tm, D_MODEL), BF16),
            pltpu.VMEM((n_trips, tm, conv_cb), F32),
            pltpu.VMEM((tm, ATTN_WIDTH), BF16),
            pltpu.VMEM((n_seg, N_KV_HEADS, WINDOW + seg, LANES), BF16),
            pltpu.VMEM((n_seg, N_KV_HEADS, WINDOW + seg, LANES), BF16),
            pltpu.VMEM((n_seg, N_KV_HEADS, LANES, WINDOW + seg), BF16),
            pltpu.VMEM((n_seg, N_KV_HEADS, LANES, WINDOW + seg), BF16),
            pltpu.VMEM((SUBLANES + seg, conv_cb), F32),
            pltpu.VMEM((n_trips, tm, 1), F32),
        ],
        compiler_params=pltpu.CompilerParams(
            dimension_semantics=("arbitrary", "arbitrary"), vmem_limit_bytes=VMEM_LIMIT_BYTES),
        name="mixer",
    )(x2d, *tabs, kinit, vinit, cinit_b, gpre, win, taps, gconv_b, gattn, sink_tab)
    cstate = jnp.swapaxes(cstate, 1, 2).reshape(n_segs_total, SUBLANES, D_CONV)
    return mix, k_tail, v_tail, cstate


def _mix_out_kernel(x_ref, mix_ref, wout_ref, gpost_ref, out_ref):
    half = x_ref.shape[0] // 2
    for r0 in (0, half):
        y = _dot(mix_ref[r0:r0 + half, :], wout_ref[...])
        out_ref[r0:r0 + half, :] = x_ref[r0:r0 + half, :] + y * _rms_scale(y) * gpost_ref[...]


def _mix_out(x2d, mix, wout, gpost, *, tm):
    rows = x2d.shape[0]
    row_blk = lambda i: (i, 0)
    width = mix.shape[1]
    return pl.pallas_call(
        _mix_out_kernel,
        grid=(rows // tm,),
        in_specs=[
            pl.BlockSpec((tm, D_MODEL), row_blk),
            pl.BlockSpec((tm, width), row_blk),
            _resident((width, D_MODEL)),
            _resident((1, D_MODEL)),
        ],
        out_specs=pl.BlockSpec((tm, D_MODEL), row_blk),
        out_shape=jax.ShapeDtypeStruct((rows, D_MODEL), F32),
        compiler_params=pltpu.CompilerParams(
            dimension_semantics=("arbitrary",), vmem_limit_bytes=VMEM_LIMIT_BYTES),
        name="mix_out",
    )(x2d, mix, wout, gpost)


def _cross_kernel(x_ref, mk_ref, mv_ref, gpre_ref, wq_ref, wo_ref, gpost_ref, gnext_ref,
                  out_ref, hnext_ref, o_ref, *, tm, n_seg):
    seg = tm // n_seg
    x = x_ref[...]
    hb = (x * _rms_scale(x) * gpre_ref[...]).astype(BF16)
    for hd in range(N_MEM_HEADS):
        c0 = hd * MEM_HEAD_DIM
        q = _dot(hb, wq_ref[:, c0:c0 + MEM_HEAD_DIM]).astype(BF16)
        for s in range(n_seg):
            r0 = s * seg
            sc = _dot_nt(q[r0:r0 + seg], mk_ref[s, :, c0:c0 + MEM_HEAD_DIM]) * (MEM_HEAD_DIM ** -0.5)
            p = jnp.exp(sc - jnp.max(sc, axis=-1, keepdims=True))
            den = jnp.sum(p, axis=-1, keepdims=True)
            o = _dot(p.astype(BF16), mv_ref[s, :, c0:c0 + MEM_HEAD_DIM]) * (1.0 / den)
            o_ref[r0:r0 + seg, c0:c0 + MEM_HEAD_DIM] = o.astype(BF16)
    half = tm // 2
    for r0 in (0, half):
        y = _dot(o_ref[r0:r0 + half, :], wo_ref[...])
        x_out = x[r0:r0 + half] + y * _rms_scale(y) * gpost_ref[...]
        out_ref[r0:r0 + half, :] = x_out
        hnext_ref[r0:r0 + half, :] = (x_out * _rms_scale(x_out) * gnext_ref[...]).astype(BF16)


def _cross(x2d, mk, mv, gpre, wq, wo, gpost, gnext, *, n_batch, tm, n_seg):
    rows = x2d.shape[0]
    nt = rows // (n_batch * tm)
    row_blk = lambda b, i: (b * nt + i, 0)
    if n_batch == 1:
        mem_blk = _resident((n_seg, N_MEM, D_MODEL))
    else:
        mem_blk = pl.BlockSpec((n_seg, N_MEM, D_MODEL), lambda b, i: (b, 0, 0))
    kern = functools.partial(_cross_kernel, tm=tm, n_seg=n_seg)
    return pl.pallas_call(
        kern,
        grid=(n_batch, nt),
        in_specs=[
            pl.BlockSpec((tm, D_MODEL), row_blk),
            mem_blk, mem_blk,
            _resident((1, D_MODEL)),
            _resident((D_MODEL, D_MODEL)),
            _resident((D_MODEL, D_MODEL)),
            _resident((1, D_MODEL)),
            _resident((1, D_MODEL)),
        ],
        out_specs=[pl.BlockSpec((tm, D_MODEL), row_blk), pl.BlockSpec((tm, D_MODEL), row_blk)],
        out_shape=[jax.ShapeDtypeStruct((rows, D_MODEL), F32),
                   jax.ShapeDtypeStruct((rows, D_MODEL), BF16)],
        scratch_shapes=[pltpu.VMEM((tm, D_MODEL), BF16)],
        compiler_params=pltpu.CompilerParams(
            dimension_semantics=("arbitrary", "arbitrary"), vmem_limit_bytes=VMEM_LIMIT_BYTES),
        name="cross_attn",
    )(x2d, mk, mv, gpre, wq, wo, gpost, gnext)


def _ffn_kernel(h_ref, x_hbm, ginit_ref, wg_ref, wu_ref, cw_ref, cb_ref, wd_ref, gpost_ref,
                out_ref, gstate_ref,
                gc_ref, cs_ref, xbuf_ref, xsem, *, tm, n_seg, fc, nt, nf):
    b = pl.program_id(0)
    i = pl.program_id(1)
    f = pl.program_id(2)
    seg = tm // n_seg

    def residual_copy():
        row0 = pl.multiple_of((b * nt + i) * tm, tm)
        return pltpu.make_async_copy(x_hbm.at[pl.ds(row0, tm)], xbuf_ref, xsem)

    @pl.when(f == 0)
    def _():
        out_ref[...] = jnp.zeros_like(out_ref)

    @pl.when(f == max(nf - 3, 0))
    def _():
        residual_copy().start()

    @pl.when(i == 0)
    def _():
        gc_ref[f] = ginit_ref[...]

    hb = h_ref[...]
    g = _dot(hb, wg_ref[...])
    u = _dot(hb, wu_ref[...])
    acts = []
    for s in range(n_seg):
        r0 = s * seg
        cur = g[r0:r0 + seg]
        a = _causal_conv(cs_ref, gc_ref[f, s], cur, cw_ref[...]) + cb_ref[...]
        acts.append((a * (1.0 / (1.0 + jnp.exp(-a))) * u[r0:r0 + seg]).astype(BF16))
        last = cur[seg - SUBLANES:seg]
        gc_ref[f, s] = last
        gstate_ref[s, f] = last
    act = acts[0] if n_seg == 1 else jnp.concatenate(acts, axis=0)
    out_ref[...] += _dot(act, wd_ref[...])

    @pl.when(f == nf - 1)
    def _():
        residual_copy().wait()
        y = out_ref[...]
        out_ref[...] = xbuf_ref[...] + y * _rms_scale(y) * gpost_ref[...]


def _ffn(h2d, x2d, ginit, wg, wu, cw, cb, wd, gpost, *, n_batch, tm, n_seg, fc):
    rows = x2d.shape[0]
    nt = rows // (n_batch * tm)
    nf = D_FF // fc
    seg = tm // n_seg
    row_blk = lambda b, i, f: (b * nt + i, 0)
    st_blk = pl.BlockSpec((n_seg, SUBLANES, fc), lambda b, i, f: (b, 0, f))
    kern = functools.partial(_ffn_kernel, tm=tm, n_seg=n_seg, fc=fc, nt=nt, nf=nf)
    return pl.pallas_call(
        kern,
        grid=(n_batch, nt, nf),
        in_specs=[
            pl.BlockSpec((tm, D_MODEL), row_blk),
            pl.BlockSpec(memory_space=pl.ANY),
            st_blk,
            pl.BlockSpec((D_MODEL, fc), lambda b, i, f: (0, f)),
            pl.BlockSpec((D_MODEL, fc), lambda b, i, f: (0, f)),
            pl.BlockSpec((CONV_WIDTH, fc), lambda b, i, f: (0, f)),
            pl.BlockSpec((1, fc), lambda b, i, f: (0, f)),
            pl.BlockSpec((fc, D_MODEL), lambda b, i, f: (f, 0)),
            _resident((1, D_MODEL)),
        ],
        out_specs=[
            pl.BlockSpec((tm, D_MODEL), row_blk),
            pl.BlockSpec((n_seg, nf, SUBLANES, fc), lambda b, i, f: (b, 0, 0, 0)),
        ],
        out_shape=[
            jax.ShapeDtypeStruct((rows, D_MODEL), F32),
            jax.ShapeDtypeStruct((n_batch * n_seg, nf, SUBLANES, fc), F32),
        ],
        scratch_shapes=[
            pltpu.VMEM((nf, n_seg, SUBLANES, fc), F32),
            pltpu.VMEM((SUBLANES + seg, fc), F32),
            pltpu.VMEM((tm, D_MODEL), F32),
            pltpu.SemaphoreType.DMA(()),
        ],
        compiler_params=pltpu.CompilerParams(
            dimension_semantics=("arbitrary", "arbitrary", "arbitrary"),
            vmem_limit_bytes=VMEM_LIMIT_BYTES),
        name="conv_ffn",
    )(h2d, x2d, ginit, wg, wu, cw, cb, wd, gpost)


def _rope_tables(pos):
    half = ROT_DIM // 2
    inv = ROPE_THETA ** (-jnp.arange(0, ROT_DIM, 2, dtype=F32) / ROT_DIM)
    ang = pos[:, None] * inv[None, :]
    cos = jnp.tile(jnp.cos(ang), (1, LANES // half))
    sin = jnp.tile(jnp.sin(ang), (1, LANES // half))
    dim = jnp.arange(LANES) % HEAD_DIM
    first, second = (dim < half)[None, :], ((dim >= half) & (dim < ROT_DIM))[None, :]
    return (jnp.where(first | second, cos, 1.0), jnp.where(first, -sin, 0.0),
            jnp.where(second, sin, 0.0))


def _merge_heads_bf16(mem):
    n, m, heads, d = mem.shape
    return mem.reshape(n, m, heads * d).astype(BF16)


def _pad_state(state):
    return jnp.pad(state, ((0, 0), (SUBLANES - (CONV_WIDTH - 1), 0), (0, 0)))


def _layer(x2d, tabs, kinit, vinit, cinit, finit, mk, mv, w, *, mixer_cfg, mix_out_tm, cross_cfg,
           ffn_cfg, carry):
    tile = lambda cfg: dict(n_batch=cfg[0], tm=cfg[1], n_seg=cfg[2])
    mix, k_new, v_new, cstate = _mixer(
        x2d, tabs, kinit, vinit, cinit, w["g_mix_pre"], w["w_mix_in"], w["conv_mix_w"],
        w["g_grp_conv"], w["g_grp_attn"], w["sinks"], carry=carry, **tile(mixer_cfg))
    x1 = _mix_out(x2d, mix, w["w_mix_out"], w["g_mix_post"], tm=mix_out_tm)
    x2, h_ffn = _cross(x1, mk, mv, w["g_x_pre"], w["w_xq"], w["w_xo"], w["g_x_post"],
                       w["g_ffn_pre"], **tile(cross_cfg))
    x3, fstate = _ffn(h_ffn, x2, finit, w["w_gate"], w["w_up"], w["conv_ffn_w"],
                      w["conv_ffn_b"], w["w_down"], w["g_ffn_post"], fc=FFN_CHUNK,
                      **tile(ffn_cfg))
    fstate = jnp.swapaxes(fstate, 1, 2).reshape(fstate.shape[0], SUBLANES, D_FF)
    return x3, k_new, v_new, cstate, fstate


def kernel(x_prompt, x_sample, cache_mem_k, cache_mem_v, cache_swa_k, cache_swa_v,
           state_mix_conv, state_ffn_conv, mem_prompt,
           g_mix_pre, w_mix_in, conv_mix_w, g_grp_conv, g_grp_attn, attn_sinks,
           w_mix_out, g_mix_post, g_mem, w_xk, w_xv, g_x_pre, w_xq, w_xo, g_x_post,
           g_ffn_pre, w_gate, w_up, conv_ffn_w, conv_ffn_b, w_down, g_ffn_post):
    n_p, s_p, _ = x_prompt.shape
    n_s, s_s, _ = x_sample.shape
    depth = w_mix_in.shape[0]
    swa_len = cache_swa_k.shape[2]
    tabs_p = _rope_tables(jnp.arange(s_p, dtype=F32))
    tabs_s = tuple(jnp.tile(a, (n_s // 2, 1))
                   for a in _rope_tables(PAST_LEN + jnp.arange(s_s, dtype=F32)))

    yp = x_prompt.reshape(n_p * s_p, D_MODEL)
    ys = x_sample.reshape(n_s * s_s, D_MODEL)
    outs = [[] for _ in range(10)]
    for l in range(depth):
        row = lambda a: a[l][None, :]
        w = {
            "sinks": attn_sinks[l],
            "g_mix_pre": row(g_mix_pre), "w_mix_in": w_mix_in[l].astype(BF16),
            "conv_mix_w": conv_mix_w[l], "g_grp_conv": row(g_grp_conv),
            "g_grp_attn": row(g_grp_attn),
            "w_mix_out": w_mix_out[l].astype(BF16), "g_mix_post": row(g_mix_post),
            "g_x_pre": row(g_x_pre), "w_xq": w_xq[l].astype(BF16), "w_xo": w_xo[l].astype(BF16),
            "g_x_post": row(g_x_post), "g_ffn_pre": row(g_ffn_pre),
            "w_gate": w_gate[l].astype(BF16), "w_up": w_up[l].astype(BF16),
            "conv_ffn_w": conv_ffn_w[l], "conv_ffn_b": row(conv_ffn_b),
            "w_down": w_down[l].astype(BF16), "g_ffn_post": row(g_ffn_post),
        }
        mk32, mv32, mkb, mvb = _memory_kv(
            mem_prompt.reshape(n_p * N_MEM, D_MODEL), row(g_mem),
            w_xk[l].astype(BF16), w_xv[l].astype(BF16))
        yp, kp, vp, cp, fp = _layer(
            yp, tabs_p,
            jnp.zeros((n_p, WINDOW, KV_WIDTH), F32), jnp.zeros((n_p, WINDOW, KV_WIDTH), F32),
            jnp.zeros((n_p, SUBLANES, D_CONV), F32), jnp.zeros((n_p, SUBLANES, D_FF), F32),
            mkb.reshape(n_p, N_MEM, D_MODEL), mvb.reshape(n_p, N_MEM, D_MODEL), w,
            mixer_cfg=(n_p, 512, 1), mix_out_tm=1024, cross_cfg=(n_p, 512, 1),
            ffn_cfg=(n_p, 1024, 1), carry=True)
        ys, ks, vs, cs, fs = _layer(
            ys, tabs_s,
            cache_swa_k[l].reshape(n_s, swa_len, KV_WIDTH),
            cache_swa_v[l].reshape(n_s, swa_len, KV_WIDTH),
            _pad_state(state_mix_conv[l]), _pad_state(state_ffn_conv[l]),
            _merge_heads_bf16(cache_mem_k[l]), _merge_heads_bf16(cache_mem_v[l]), w,
            mixer_cfg=(2, n_s * s_s // 2, n_s // 2), mix_out_tm=n_s * s_s,
            cross_cfg=(1, n_s * s_s, n_s),
            ffn_cfg=(1, n_s * s_s, n_s), carry=False)
        keep_p = min(WINDOW, s_p)
        tail = CONV_WIDTH - 1
        new_k = ks.reshape(n_s, s_s, N_KV_HEADS, HEAD_DIM)
        new_v = vs.reshape(n_s, s_s, N_KV_HEADS, HEAD_DIM)
        layer_out = (
            mk32.reshape(n_p, N_MEM, N_MEM_HEADS, MEM_HEAD_DIM),
            mv32.reshape(n_p, N_MEM, N_MEM_HEADS, MEM_HEAD_DIM),
            kp.reshape(n_p, keep_p, N_KV_HEADS, HEAD_DIM),
            vp.reshape(n_p, keep_p, N_KV_HEADS, HEAD_DIM),
            cp[:, SUBLANES - tail:], fp[:, SUBLANES - tail:],
            jnp.concatenate([cache_swa_k[l], new_k], axis=1)[:, -swa_len:],
            jnp.concatenate([cache_swa_v[l], new_v], axis=1)[:, -swa_len:],
            cs[:, SUBLANES - tail:], fs[:, SUBLANES - tail:],
        )
        for acc, o in zip(outs, layer_out):
            acc.append(o)

    return (yp.reshape(n_p, s_p, D_MODEL), ys.reshape(n_s, s_s, D_MODEL),
            *[jnp.stack(o) for o in outs])
```

```python
import functools

import jax
import jax.numpy as jnp
from jax import lax
from jax.experimental import pallas as pl
from jax.experimental.pallas import tpu as pltpu

D_MODEL = 2048
CHUNK = 64
WINDOW = 128
CONV_WIDTH = 3
HEAD_DIM = 64
N_Q_HEADS = 16
N_KV_HEADS = 4
ATTN_WIDTH = N_Q_HEADS * HEAD_DIM
KV_WIDTH = N_KV_HEADS * HEAD_DIM
D_CONV = D_MODEL // 2
ROT_DIM = HEAD_DIM // 4
ROPE_THETA = 500000.0
N_MEM = 256
N_MEM_HEADS = 4
MEM_HEAD_DIM = D_MODEL // N_MEM_HEADS
D_FF = 11 * D_MODEL // 4
PAST_LEN = 1024
EPS = 1e-6
NEG_INF = -1e30

BF16 = jnp.bfloat16
F32 = jnp.float32

LANES = 128
SUBLANES = 8
HALF = LANES // 2
BAND = WINDOW + CHUNK
VMEM_LIMIT_BYTES = 60 * 1024 * 1024
FFN_CHUNK = 512
ATTN_BLOCK = 128

_NT = (((1,), (1,)), ((), ()))


def _dot(a, b):
    return jnp.dot(a, b, preferred_element_type=F32)


def _dot_nt(a, b):
    return lax.dot_general(a, b, _NT, preferred_element_type=F32)


def _rms_scale(x):
    return lax.rsqrt(jnp.mean(x * x, axis=-1, keepdims=True) + EPS)


def _aligned(index, multiple):
    return index if isinstance(index, int) else pl.multiple_of(index, multiple)


def _resident(shape):
    nd = len(shape)
    return pl.BlockSpec(shape, lambda *_: (0,) * nd, pipeline_mode=pl.Buffered(1))


def _causal_conv(cs_ref, prev8, cur, taps):
    rows = cur.shape[0]
    cs_ref[0:SUBLANES, :] = prev8
    cs_ref[SUBLANES:SUBLANES + rows, :] = cur
    x1 = cs_ref[pl.ds(SUBLANES - 1, rows), :]
    x2 = cs_ref[pl.ds(SUBLANES - 2, rows), :]
    return taps[0:1] * x2 + taps[1:2] * x1 + taps[2:3] * cur


def _memkv_kernel(mem_ref, g_ref, wk_ref, wv_ref, k32_ref, v32_ref, kbf_ref, vbf_ref):
    m = mem_ref[...]
    hm = (m * _rms_scale(m) * g_ref[...]).astype(BF16)
    k = _dot(hm, wk_ref[...])
    v = _dot(hm, wv_ref[...])
    for h in range(N_MEM_HEADS):
        k32_ref[:, h, :] = k[:, MEM_HEAD_DIM * h:MEM_HEAD_DIM * (h + 1)]
        v32_ref[:, h, :] = v[:, MEM_HEAD_DIM * h:MEM_HEAD_DIM * (h + 1)]
    kbf_ref[...] = k.astype(BF16)
    vbf_ref[...] = v.astype(BF16)


def _memory_kv(mem2d, g_mem, wk, wv):
    rows = mem2d.shape[0]
    tm = N_MEM
    out_w = wk.shape[1]
    row = lambda i: (i, 0)
    row3 = lambda i: (i, 0, 0)
    return pl.pallas_call(
        _memkv_kernel,
        grid=(rows // tm,),
        in_specs=[
            pl.BlockSpec((tm, D_MODEL), row),
            _resident((1, D_MODEL)),
            _resident((D_MODEL, out_w)),
            _resident((D_MODEL, out_w)),
        ],
        out_specs=[
            pl.BlockSpec((tm, N_MEM_HEADS, MEM_HEAD_DIM), row3),
            pl.BlockSpec((tm, N_MEM_HEADS, MEM_HEAD_DIM), row3),
            pl.BlockSpec((tm, out_w), row),
            pl.BlockSpec((tm, out_w), row),
        ],
        out_shape=[
            jax.ShapeDtypeStruct((rows, N_MEM_HEADS, MEM_HEAD_DIM), F32),
            jax.ShapeDtypeStruct((rows, N_MEM_HEADS, MEM_HEAD_DIM), F32),
            jax.ShapeDtypeStruct((rows, out_w), BF16),
            jax.ShapeDtypeStruct((rows, out_w), BF16),
        ],
        compiler_params=pltpu.CompilerParams(
            dimension_semantics=("arbitrary",), vmem_limit_bytes=VMEM_LIMIT_BYTES),
        name="memory_kv",
    )(mem2d, g_mem, wk, wv)


def _mixer_kernel(x_ref, cos_ref, s1_ref, s2_ref, kinit_ref, vinit_ref, cinit_ref,
                  gpre_ref, win_ref, convw_ref, gconv_ref, gattn_ref, sink_ref,
                  mix_ref, kout_ref, vout_ref, cstate_ref,
                  h_ref, yc_ref, q_ref, ke_ref, ko_ref, ve_ref, vo_ref, cs_ref, ssq_ref,
                  *, tm, n_seg, carry, keep, conv_cb, ab):
    i = pl.program_id(1)
    seg = tm // n_seg
    nblk = seg // ab
    nkeys = ab + WINDOW
    lo = lax.broadcasted_iota(jnp.int32, (1, LANES), 1) < HALF

    def rope(blk, cos_t, s1_t, s2_t):
        return (blk * cos_t + pltpu.roll(blk, LANES - ROT_DIM // 2, 1) * s1_t
                + pltpu.roll(blk, ROT_DIM // 2, 1) * s2_t)

    def store_kv(kf, vf, s, row0):
        rows = kf.shape[0]
        for hp in range(N_KV_HEADS // 2):
            own = kf[:, LANES * hp:LANES * (hp + 1)]
            swp = pltpu.roll(own, HALF, 1)
            for par in range(2):
                h = 2 * hp + par
                low, high = (own, swp) if par == 0 else (swp, own)
                ke_ref[s, h, row0:row0 + rows, :] = jnp.where(lo, low, 0.0).astype(BF16)
                ko_ref[s, h, row0:row0 + rows, :] = jnp.where(lo, 0.0, high).astype(BF16)
        vt = vf.T.astype(BF16)
        zeros = jnp.zeros((HALF, rows), BF16)
        for h in range(N_KV_HEADS):
            vh = vt[HEAD_DIM * h:HEAD_DIM * (h + 1)]
            ve_ref[s, h, :, row0:row0 + rows] = jnp.concatenate([vh, zeros], axis=0)
            vo_ref[s, h, :, row0:row0 + rows] = jnp.concatenate([zeros, vh], axis=0)

    @pl.when(i == 0)
    def _():
        for s in range(n_seg):
            store_kv(kinit_ref[s], vinit_ref[s], s, 0)
            cstate_ref[s] = cinit_ref[s]

    x = x_ref[...]
    h_ref[...] = (x * _rms_scale(x) * gpre_ref[...]).astype(BF16)
    hb = h_ref[...]

    def conv_block(cb):
        hbl = h_ref[...]
        c0 = _aligned(cb * conv_cb, conv_cb)
        bg = _dot(hbl, win_ref[:, pl.ds(c0, conv_cb)])
        cg = _dot(hbl, win_ref[:, pl.ds(D_CONV + c0, conv_cb)])
        ug = _dot(hbl, win_ref[:, pl.ds(2 * D_CONV + c0, conv_cb)])
        cu = cg * ug
        taps = convw_ref[cb]
        ssq_parts = []
        for s in range(n_seg):
            r0 = s * seg
            cur = cu[r0:r0 + seg]
            y = bg[r0:r0 + seg] * _causal_conv(cs_ref, cstate_ref[s, cb], cur, taps)
            ssq_parts.append(jnp.sum(y * y, axis=-1, keepdims=True))
            yc_ref[cb, r0:r0 + seg, :] = y
            cstate_ref[s, cb] = cur[seg - SUBLANES:seg]
        return ssq_parts[0] if n_seg == 1 else jnp.concatenate(ssq_parts, axis=0)

    cos_t = cos_ref[...]
    s1_t = s1_ref[...]
    s2_t = s2_ref[...]
    q0 = 3 * D_CONV
    qf = _dot(hb, win_ref[:, q0:q0 + ATTN_WIDTH])
    for jb in range(ATTN_WIDTH // LANES):
        blk = rope(qf[:, LANES * jb:LANES * (jb + 1)], cos_t, s1_t, s2_t)
        q_ref[:, LANES * jb:LANES * (jb + 1)] = (blk * (HEAD_DIM ** -0.5)).astype(BF16)
    k0 = q0 + ATTN_WIDTH
    kv = _dot(hb, win_ref[:, k0:k0 + 2 * KV_WIDTH])
    kf = jnp.concatenate(
        [rope(kv[:, LANES * jb:LANES * (jb + 1)], cos_t, s1_t, s2_t)
         for jb in range(KV_WIDTH // LANES)], axis=1)
    vf = kv[:, KV_WIDTH:2 * KV_WIDTH]
    for s in range(n_seg):
        r0 = s * seg
        store_kv(kf[r0:r0 + seg], vf[r0:r0 + seg], s, WINDOW)
        kout_ref[s] = kf[r0 + seg - keep:r0 + seg]
        vout_ref[s] = vf[r0 + seg - keep:r0 + seg]

    key_pos = lax.broadcasted_iota(jnp.int32, (nkeys, 2 * ab), 0)
    if ab > CHUNK:
        tok = lax.broadcasted_iota(jnp.int32, (nkeys, 2 * ab), 1) % ab
        first_key = (tok // CHUNK) * CHUNK
        band_bias = jnp.where((key_pos >= first_key) & (key_pos < first_key + BAND), 0.0, NEG_INF)
    else:
        band_bias = None
    feat_lo = lax.broadcasted_iota(jnp.int32, (LANES, 1), 0) < HALF

    def block_place(idx):
        s = 0 if n_seg == 1 else idx // nblk
        j = 0 if nblk == 1 else idx % nblk
        return s, pl.ds(_aligned(idx * ab, ab), ab), pl.ds(_aligned(j * ab, ab), nkeys)

    def attend_block(idx):
        s, rows, band = block_place(idx)
        j = 0 if nblk == 1 else idx % nblk
        bias = band_bias
        if carry:
            first_valid = WINDOW - (i * seg + j * ab)
            start_bias = jnp.where(key_pos >= first_valid, 0.0, NEG_INF)
            bias = start_bias if bias is None else bias + start_bias
        heads = range(N_KV_HEADS)
        scores = []
        for h in heads:
            qa = jnp.concatenate(
                [q_ref[rows, 2 * LANES * h:2 * LANES * h + LANES],
                 q_ref[rows, 2 * LANES * h + LANES:2 * LANES * (h + 1)]], axis=0)
            st_e = _dot_nt(ke_ref[s, h, band, :], qa)
            st_o = _dot_nt(ko_ref[s, h, band, :], qa)
            if bias is not None:
                st_e = st_e + bias
                st_o = st_o + bias
            scores.append((st_e, st_o))
        stats = []
        for h in heads:
            sink_e = sink_ref[h, 0:1, :]
            sink_o = sink_ref[h, 1:2, :]
            m_e = jnp.maximum(jnp.max(scores[h][0], axis=0, keepdims=True), sink_e)
            m_o = jnp.maximum(jnp.max(scores[h][1], axis=0, keepdims=True), sink_o)
            stats.append((m_e, m_o, jnp.exp(sink_e - m_e), jnp.exp(sink_o - m_o)))
        outs = []
        for h in heads:
            p_e = jnp.exp(scores[h][0] - stats[h][0])
            p_o = jnp.exp(scores[h][1] - stats[h][1])
            den_e = jnp.sum(p_e, axis=0, keepdims=True) + stats[h][2]
            den_o = jnp.sum(p_o, axis=0, keepdims=True) + stats[h][3]
            acc_t = (_dot(ve_ref[s, h, :, band], p_e.astype(BF16))
                     + _dot(vo_ref[s, h, :, band], p_o.astype(BF16)))
            acc = (acc_t * jnp.where(feat_lo, 1.0 / den_e, 1.0 / den_o)).T
            outs += [acc[0:ab], acc[ab:2 * ab]]
        o = jnp.concatenate(outs, axis=1)
        mix_ref[rows, D_CONV:D_CONV + ATTN_WIDTH] = (
            o * _rms_scale(o) * gattn_ref[...]).astype(BF16)

    n_trips = n_seg * nblk
    for t in range(n_trips):
        @pl.when(i >= 0)
        def _(t=t):
            attend_block(t)
            ssq_ref[t] = conv_block(t)
    ssq_c = ssq_ref[0]
    for t in range(1, n_trips):
        ssq_c = ssq_c + ssq_ref[t]

    if carry:
        for h in range(N_KV_HEADS):
            for ref in (ke_ref, ko_ref):
                ref[0, h, 0:WINDOW, :] = ref[0, h, seg:seg + WINDOW, :]
            for ref in (ve_ref, vo_ref):
                ref[0, h, :, 0:WINDOW] = ref[0, h, :, seg:seg + WINDOW]

    rs_c = lax.rsqrt(ssq_c * (1.0 / D_CONV) + EPS)
    for cb in range(D_CONV // conv_cb):
        mix_ref[:, cb * conv_cb:(cb + 1) * conv_cb] = (
            yc_ref[cb] * rs_c * gconv_ref[cb]).astype(BF16)


def _mixer(x2d, tabs, kinit, vinit, cinit, gpre, win, convw, gconv, gattn, sinks,
           *, n_batch, tm, n_seg, carry):
    rows = x2d.shape[0]
    nt = rows // (n_batch * tm)
    seg = tm // n_seg
    keep = min(WINDOW, seg)
    ab = min(ATTN_BLOCK, seg)
    n_trips = tm // ab
    conv_cb = D_CONV // n_trips
    row_blk = lambda b, i: (b * nt + i, 0)
    tab_blk = lambda b, i: (i, 0)
    seg_blk = lambda b, i: (b, 0, 0)
    cst_blk = pl.BlockSpec((n_seg, n_trips, SUBLANES, conv_cb), lambda b, i: (b, 0, 0, 0))
    kern = functools.partial(_mixer_kernel, tm=tm, n_seg=n_seg, carry=carry, keep=keep,
                             conv_cb=conv_cb, ab=ab)
    n_segs_total = n_batch * n_seg
    taps = jnp.swapaxes(convw.reshape(CONV_WIDTH, n_trips, conv_cb), 0, 1)
    gconv_b = gconv.reshape(n_trips, 1, conv_cb)
    cinit_b = jnp.swapaxes(cinit.reshape(n_segs_total, SUBLANES, n_trips, conv_cb), 1, 2)
    by_parity = jnp.swapaxes(sinks.reshape(N_KV_HEADS, 2, 2), 1, 2)[:, :, :, None]
    sink_tab = jnp.broadcast_to(by_parity, (N_KV_HEADS, 2, 2, ab)).reshape(N_KV_HEADS, 2, 2 * ab)
    sink_tab = jnp.pad(sink_tab, ((0, 0), (0, SUBLANES - 2), (0, 0)))
    mix, k_tail, v_tail, cstate = pl.pallas_call(
        kern,
        grid=(n_batch, nt),
        in_specs=[
            pl.BlockSpec((tm, D_MODEL), row_blk),
            pl.BlockSpec((tm, LANES), tab_blk),
            pl.BlockSpec((tm, LANES), tab_blk),
            pl.BlockSpec((tm, LANES), tab_blk),
            pl.BlockSpec((n_seg, WINDOW, KV_WIDTH), seg_blk),
            pl.BlockSpec((n_seg, WINDOW, KV_WIDTH), seg_blk),
            cst_blk,
            _resident((1, D_MODEL)),
            _resident(win.shape),
            _resident(taps.shape),
            _resident(gconv_b.shape),
            _resident((1, ATTN_WIDTH)),
            _resident((N_KV_HEADS, SUBLANES, 2 * ab)),
        ],
        out_specs=[
            pl.BlockSpec((tm, D_CONV + ATTN_WIDTH), row_blk),
            pl.BlockSpec((n_seg, keep, KV_WIDTH), seg_blk),
            pl.BlockSpec((n_seg, keep, KV_WIDTH), seg_blk),
            cst_blk,
        ],
        out_shape=[
            jax.ShapeDtypeStruct((rows, D_CONV + ATTN_WIDTH), BF16),
            jax.ShapeDtypeStruct((n_segs_total, keep, KV_WIDTH), F32),
            jax.ShapeDtypeStruct((n_segs_total, keep, KV_WIDTH), F32),
            jax.ShapeDtypeStruct((n_segs_total, n_trips, SUBLANES, conv_cb), F32),
        ],
        scratch_shapes=[
            pltpu.VMEM((tm, D_MODEL), BF16),
            pltpu.VMEM((n_trips, tm, conv_cb), F32),
            pltpu.VMEM((tm, ATTN_WIDTH), BF16),
            pltpu.VMEM((n_seg, N_KV_HEADS, WINDOW + seg, LANES), BF16),
            pltpu.VMEM((n_seg, N_KV_HEADS, WINDOW + seg, LANES), BF16),
            pltpu.VMEM((n_seg, N_KV_HEADS, LANES, WINDOW + seg), BF16),
            pltpu.VMEM((n_seg, N_KV_HEADS, LANES, WINDOW + seg), BF16),
            pltpu.VMEM((SUBLANES + seg, conv_cb), F32),
            pltpu.VMEM((n_trips, tm, 1), F32),
        ],
        compiler_params=pltpu.CompilerParams(
            dimension_semantics=("arbitrary", "arbitrary"), vmem_limit_bytes=VMEM_LIMIT_BYTES),
        name="mixer",
    )(x2d, *tabs, kinit, vinit, cinit_b, gpre, win, taps, gconv_b, gattn, sink_tab)
    cstate = jnp.swapaxes(cstate, 1, 2).reshape(n_segs_total, SUBLANES, D_CONV)
    return mix, k_tail, v_tail, cstate


def _mix_out_kernel(x_ref, mix_ref, wout_ref, gpost_ref, out_ref):
    half = x_ref.shape[0] // 2
    for r0 in (0, half):
        y = _dot(mix_ref[r0:r0 + half, :], wout_ref[...])
        out_ref[r0:r0 + half, :] = x_ref[r0:r0 + half, :] + y * _rms_scale(y) * gpost_ref[...]


def _mix_out(x2d, mix, wout, gpost, *, tm):
    rows = x2d.shape[0]
    row_blk = lambda i: (i, 0)
    width = mix.shape[1]
    return pl.pallas_call(
        _mix_out_kernel,
        grid=(rows // tm,),
        in_specs=[
            pl.BlockSpec((tm, D_MODEL), row_blk),
            pl.BlockSpec((tm, width), row_blk),
            _resident((width, D_MODEL)),
            _resident((1, D_MODEL)),
        ],
        out_specs=pl.BlockSpec((tm, D_MODEL), row_blk),
        out_shape=jax.ShapeDtypeStruct((rows, D_MODEL), F32),
        compiler_params=pltpu.CompilerParams(
            dimension_semantics=("arbitrary",), vmem_limit_bytes=VMEM_LIMIT_BYTES),
        name="mix_out",
    )(x2d, mix, wout, gpost)


def _cross_kernel(x_ref, mk_ref, mv_ref, gpre_ref, wq_ref, wo_ref, gpost_ref, gnext_ref,
                  out_ref, hnext_ref, o_ref, *, tm, n_seg):
    seg = tm // n_seg
    x = x_ref[...]
    hb = (x * _rms_scale(x) * gpre_ref[...]).astype(BF16)
    qs = [_dot(hb, wq_ref[:, 0:MEM_HEAD_DIM]).astype(BF16)]
    for hd in range(N_MEM_HEADS):
        c0 = hd * MEM_HEAD_DIM
        if hd + 1 < N_MEM_HEADS:
            qs.append(_dot(hb, wq_ref[:, c0 + MEM_HEAD_DIM:c0 + 2 * MEM_HEAD_DIM]).astype(BF16))
        q = qs[hd]
        for s in range(n_seg):
            r0 = s * seg
            sc = _dot_nt(q[r0:r0 + seg], mk_ref[s, :, c0:c0 + MEM_HEAD_DIM]) * (MEM_HEAD_DIM ** -0.5)
            p = jnp.exp(sc - jnp.max(sc, axis=-1, keepdims=True))
            den = jnp.sum(p, axis=-1, keepdims=True)
            o = _dot(p.astype(BF16), mv_ref[s, :, c0:c0 + MEM_HEAD_DIM]) * (1.0 / den)
            o_ref[r0:r0 + seg, c0:c0 + MEM_HEAD_DIM] = o.astype(BF16)
    half = tm // 2
    for r0 in (0, half):
        y = _dot(o_ref[r0:r0 + half, :], wo_ref[...])
        x_out = x[r0:r0 + half] + y * _rms_scale(y) * gpost_ref[...]
        out_ref[r0:r0 + half, :] = x_out
        hnext_ref[r0:r0 + half, :] = (x_out * _rms_scale(x_out) * gnext_ref[...]).astype(BF16)


def _cross(x2d, mk, mv, gpre, wq, wo, gpost, gnext, *, n_batch, tm, n_seg):
    rows = x2d.shape[0]
    nt = rows // (n_batch * tm)
    row_blk = lambda b, i: (b * nt + i, 0)
    if n_batch == 1:
        mem_blk = _resident((n_seg, N_MEM, D_MODEL))
    else:
        mem_blk = pl.BlockSpec((n_seg, N_MEM, D_MODEL), lambda b, i: (b, 0, 0))
    kern = functools.partial(_cross_kernel, tm=tm, n_seg=n_seg)
    return pl.pallas_call(
        kern,
        grid=(n_batch, nt),
        in_specs=[
            pl.BlockSpec((tm, D_MODEL), row_blk),
            mem_blk, mem_blk,
            _resident((1, D_MODEL)),
            _resident((D_MODEL, D_MODEL)),
            _resident((D_MODEL, D_MODEL)),
            _resident((1, D_MODEL)),
            _resident((1, D_MODEL)),
        ],
        out_specs=[pl.BlockSpec((tm, D_MODEL), row_blk), pl.BlockSpec((tm, D_MODEL), row_blk)],
        out_shape=[jax.ShapeDtypeStruct((rows, D_MODEL), F32),
                   jax.ShapeDtypeStruct((rows, D_MODEL), BF16)],
        scratch_shapes=[pltpu.VMEM((tm, D_MODEL), BF16)],
        compiler_params=pltpu.CompilerParams(
            dimension_semantics=("arbitrary", "arbitrary"), vmem_limit_bytes=VMEM_LIMIT_BYTES),
        name="cross_attn",
    )(x2d, mk, mv, gpre, wq, wo, gpost, gnext)


def _ffn_kernel(h_ref, x_hbm, ginit_ref, wg_ref, wu_ref, cw_ref, cb_ref, wd_ref, gpost_ref,
                out_ref, gstate_ref,
                gc_ref, cs_ref, xbuf_ref, xsem, *, tm, n_seg, fc, nt, nf):
    b = pl.program_id(0)
    i = pl.program_id(1)
    f = pl.program_id(2)
    seg = tm // n_seg

    def residual_copy():
        row0 = pl.multiple_of((b * nt + i) * tm, tm)
        return pltpu.make_async_copy(x_hbm.at[pl.ds(row0, tm)], xbuf_ref, xsem)

    @pl.when(f == 0)
    def _():
        out_ref[...] = jnp.zeros_like(out_ref)

    @pl.when(f == max(nf - 3, 0))
    def _():
        residual_copy().start()

    @pl.when(i == 0)
    def _():
        gc_ref[f] = ginit_ref[...]

    hb = h_ref[...]
    g = _dot(hb, wg_ref[...])
    u = _dot(hb, wu_ref[...])
    acts = []
    for s in range(n_seg):
        r0 = s * seg
        cur = g[r0:r0 + seg]
        a = _causal_conv(cs_ref, gc_ref[f, s], cur, cw_ref[...]) + cb_ref[...]
        acts.append((a * (1.0 / (1.0 + jnp.exp(-a))) * u[r0:r0 + seg]).astype(BF16))
        last = cur[seg - SUBLANES:seg]
        gc_ref[f, s] = last
        gstate_ref[s, f] = last
    act = acts[0] if n_seg == 1 else jnp.concatenate(acts, axis=0)
    out_ref[...] += _dot(act, wd_ref[...])

    @pl.when(f == nf - 1)
    def _():
        residual_copy().wait()
        y = out_ref[...]
        out_ref[...] = xbuf_ref[...] + y * _rms_scale(y) * gpost_ref[...]


def _ffn(h2d, x2d, ginit, wg, wu, cw, cb, wd, gpost, *, n_batch, tm, n_seg, fc):
    rows = x2d.shape[0]
    nt = rows // (n_batch * tm)
    nf = D_FF // fc
    seg = tm // n_seg
    row_blk = lambda b, i, f: (b * nt + i, 0)
    st_blk = pl.BlockSpec((n_seg, SUBLANES, fc), lambda b, i, f: (b, 0, f))
    kern = functools.partial(_ffn_kernel, tm=tm, n_seg=n_seg, fc=fc, nt=nt, nf=nf)
    return pl.pallas_call(
        kern,
        grid=(n_batch, nt, nf),
        in_specs=[
            pl.BlockSpec((tm, D_MODEL), row_blk),
            pl.BlockSpec(memory_space=pl.ANY),
            st_blk,
            pl.BlockSpec((D_MODEL, fc), lambda b, i, f: (0, f)),
            pl.BlockSpec((D_MODEL, fc), lambda b, i, f: (0, f)),
            pl.BlockSpec((CONV_WIDTH, fc), lambda b, i, f: (0, f)),
            pl.BlockSpec((1, fc), lambda b, i, f: (0, f)),
            pl.BlockSpec((fc, D_MODEL), lambda b, i, f: (f, 0)),
            _resident((1, D_MODEL)),
        ],
        out_specs=[
            pl.BlockSpec((tm, D_MODEL), row_blk),
            pl.BlockSpec((n_seg, nf, SUBLANES, fc), lambda b, i, f: (b, 0, 0, 0)),
        ],
        out_shape=[
            jax.ShapeDtypeStruct((rows, D_MODEL), F32),
            jax.ShapeDtypeStruct((n_batch * n_seg, nf, SUBLANES, fc), F32),
        ],
        scratch_shapes=[
            pltpu.VMEM((nf, n_seg, SUBLANES, fc), F32),
            pltpu.VMEM((SUBLANES + seg, fc), F32),
            pltpu.VMEM((tm, D_MODEL), F32),
            pltpu.SemaphoreType.DMA(()),
        ],
        compiler_params=pltpu.CompilerParams(
            dimension_semantics=("arbitrary", "arbitrary", "arbitrary"),
            vmem_limit_bytes=VMEM_LIMIT_BYTES),
        name="conv_ffn",
    )(h2d, x2d, ginit, wg, wu, cw, cb, wd, gpost)


def _rope_tables(pos):
    half = ROT_DIM // 2
    inv = ROPE_THETA ** (-jnp.arange(0, ROT_DIM, 2, dtype=F32) / ROT_DIM)
    ang = pos[:, None] * inv[None, :]
    cos = jnp.tile(jnp.cos(ang), (1, LANES // half))
    sin = jnp.tile(jnp.sin(ang), (1, LANES // half))
    dim = jnp.arange(LANES) % HEAD_DIM
    first, second = (dim < half)[None, :], ((dim >= half) & (dim < ROT_DIM))[None, :]
    return (jnp.where(first | second, cos, 1.0), jnp.where(first, -sin, 0.0),
            jnp.where(second, sin, 0.0))


def _merge_heads_bf16(mem):
    n, m, heads, d = mem.shape
    return mem.reshape(n, m, heads * d).astype(BF16)


def _pad_state(state):
    return jnp.pad(state, ((0, 0), (SUBLANES - (CONV_WIDTH - 1), 0), (0, 0)))


def _layer(x2d, tabs, kinit, vinit, cinit, finit, mk, mv, w, *, mixer_cfg, mix_out_tm, cross_cfg,
           ffn_cfg, carry):
    tile = lambda cfg: dict(n_batch=cfg[0], tm=cfg[1], n_seg=cfg[2])
    mix, k_new, v_new, cstate = _mixer(
        x2d, tabs, kinit, vinit, cinit, w["g_mix_pre"], w["w_mix_in"], w["conv_mix_w"],
        w["g_grp_conv"], w["g_grp_attn"], w["sinks"], carry=carry, **tile(mixer_cfg))
    x1 = _mix_out(x2d, mix, w["w_mix_out"], w["g_mix_post"], tm=mix_out_tm)
    x2, h_ffn = _cross(x1, mk, mv, w["g_x_pre"], w["w_xq"], w["w_xo"], w["g_x_post"],
                       w["g_ffn_pre"], **tile(cross_cfg))
    x3, fstate = _ffn(h_ffn, x2, finit, w["w_gate"], w["w_up"], w["conv_ffn_w"],
                      w["conv_ffn_b"], w["w_down"], w["g_ffn_post"], fc=FFN_CHUNK,
                      **tile(ffn_cfg))
    fstate = jnp.swapaxes(fstate, 1, 2).reshape(fstate.shape[0], SUBLANES, D_FF)
    return x3, k_new, v_new, cstate, fstate


def kernel(x_prompt, x_sample, cache_mem_k, cache_mem_v, cache_swa_k, cache_swa_v,
           state_mix_conv, state_ffn_conv, mem_prompt,
           g_mix_pre, w_mix_in, conv_mix_w, g_grp_conv, g_grp_attn, attn_sinks,
           w_mix_out, g_mix_post, g_mem, w_xk, w_xv, g_x_pre, w_xq, w_xo, g_x_post,
           g_ffn_pre, w_gate, w_up, conv_ffn_w, conv_ffn_b, w_down, g_ffn_post):
    n_p, s_p, _ = x_prompt.shape
    n_s, s_s, _ = x_sample.shape
    depth = w_mix_in.shape[0]
    swa_len = cache_swa_k.shape[2]
    tabs_p = _rope_tables(jnp.arange(s_p, dtype=F32))
    tabs_s = tuple(jnp.tile(a, (n_s // 2, 1))
                   for a in _rope_tables(PAST_LEN + jnp.arange(s_s, dtype=F32)))

    yp = x_prompt.reshape(n_p * s_p, D_MODEL)
    ys = x_sample.reshape(n_s * s_s, D_MODEL)
    outs = [[] for _ in range(10)]
    for l in range(depth):
        row = lambda a: a[l][None, :]
        w = {
            "sinks": attn_sinks[l],
            "g_mix_pre": row(g_mix_pre), "w_mix_in": w_mix_in[l].astype(BF16),
            "conv_mix_w": conv_mix_w[l], "g_grp_conv": row(g_grp_conv),
            "g_grp_attn": row(g_grp_attn),
            "w_mix_out": w_mix_out[l].astype(BF16), "g_mix_post": row(g_mix_post),
            "g_x_pre": row(g_x_pre), "w_xq": w_xq[l].astype(BF16), "w_xo": w_xo[l].astype(BF16),
            "g_x_post": row(g_x_post), "g_ffn_pre": row(g_ffn_pre),
            "w_gate": w_gate[l].astype(BF16), "w_up": w_up[l].astype(BF16),
            "conv_ffn_w": conv_ffn_w[l], "conv_ffn_b": row(conv_ffn_b),
            "w_down": w_down[l].astype(BF16), "g_ffn_post": row(g_ffn_post),
        }
        mk32, mv32, mkb, mvb = _memory_kv(
            mem_prompt.reshape(n_p * N_MEM, D_MODEL), row(g_mem),
            w_xk[l].astype(BF16), w_xv[l].astype(BF16))
        yp, kp, vp, cp, fp = _layer(
            yp, tabs_p,
            jnp.zeros((n_p, WINDOW, KV_WIDTH), F32), jnp.zeros((n_p, WINDOW, KV_WIDTH), F32),
            jnp.zeros((n_p, SUBLANES, D_CONV), F32), jnp.zeros((n_p, SUBLANES, D_FF), F32),
            mkb.reshape(n_p, N_MEM, D_MODEL), mvb.reshape(n_p, N_MEM, D_MODEL), w,
            mixer_cfg=(n_p, 512, 1), mix_out_tm=1024, cross_cfg=(n_p, 512, 1),
            ffn_cfg=(n_p, 1024, 1), carry=True)
        ys, ks, vs, cs, fs = _layer(
            ys, tabs_s,
            cache_swa_k[l].reshape(n_s, swa_len, KV_WIDTH),
            cache_swa_v[l].reshape(n_s, swa_len, KV_WIDTH),
            _pad_state(state_mix_conv[l]), _pad_state(state_ffn_conv[l]),
            _merge_heads_bf16(cache_mem_k[l]), _merge_heads_bf16(cache_mem_v[l]), w,
            mixer_cfg=(2, n_s * s_s // 2, n_s // 2), mix_out_tm=n_s * s_s,
            cross_cfg=(1, n_s * s_s, n_s),
            ffn_cfg=(1, n_s * s_s, n_s), carry=False)
        keep_p = min(WINDOW, s_p)
        tail = CONV_WIDTH - 1
        new_k = ks.reshape(n_s, s_s, N_KV_HEADS, HEAD_DIM)
        new_v = vs.reshape(n_s, s_s, N_KV_HEADS, HEAD_DIM)
        layer_out = (
            mk32.reshape(n_p, N_MEM, N_MEM_HEADS, MEM_HEAD_DIM),
            mv32.reshape(n_p, N_MEM, N_MEM_HEADS, MEM_HEAD_DIM),
            kp.reshape(n_p, keep_p, N_KV_HEADS, HEAD_DIM),
            vp.reshape(n_p, keep_p, N_KV_HEADS, HEAD_DIM),
            cp[:, SUBLANES - tail:], fp[:, SUBLANES - tail:],
            jnp.concatenate([cache_swa_k[l], new_k], axis=1)[:, -swa_len:],
            jnp.concatenate([cache_swa_v[l], new_v], axis=1)[:, -swa_len:],
            cs[:, SUBLANES - tail:], fs[:, SUBLANES - tail:],
        )
        for acc, o in zip(outs, layer_out):
            acc.append(o)

    return (yp.reshape(n_p, s_p, D_MODEL), ys.reshape(n_s, s_s, D_MODEL),
            *[jnp.stack(o) for o in outs])
```

```python
import functools

import jax
import jax.numpy as jnp
from jax import lax
from jax.experimental import pallas as pl
from jax.experimental.pallas import tpu as pltpu

D_MODEL = 2048
CHUNK = 64
WINDOW = 128
CONV_WIDTH = 3
HEAD_DIM = 64
N_Q_HEADS = 16
N_KV_HEADS = 4
ATTN_WIDTH = N_Q_HEADS * HEAD_DIM
KV_WIDTH = N_KV_HEADS * HEAD_DIM
D_CONV = D_MODEL // 2
ROT_DIM = HEAD_DIM // 4
ROPE_THETA = 500000.0
N_MEM = 256
N_MEM_HEADS = 4
MEM_HEAD_DIM = D_MODEL // N_MEM_HEADS
D_FF = 11 * D_MODEL // 4
PAST_LEN = 1024
EPS = 1e-6
NEG_INF = -1e30

BF16 = jnp.bfloat16
F32 = jnp.float32

LANES = 128
SUBLANES = 8
HALF = LANES // 2
BAND = WINDOW + CHUNK
VMEM_LIMIT_BYTES = 60 * 1024 * 1024
FFN_CHUNK = 512
ATTN_BLOCK = 128

_NT = (((1,), (1,)), ((), ()))


def _dot(a, b):
    return jnp.dot(a, b, preferred_element_type=F32)


def _dot_nt(a, b):
    return lax.dot_general(a, b, _NT, preferred_element_type=F32)


def _rms_scale(x):
    return lax.rsqrt(jnp.mean(x * x, axis=-1, keepdims=True) + EPS)


def _aligned(index, multiple):
    return index if isinstance(index, int) else pl.multiple_of(index, multiple)


def _resident(shape):
    nd = len(shape)
    return pl.BlockSpec(shape, lambda *_: (0,) * nd, pipeline_mode=pl.Buffered(1))


def _causal_conv(cs_ref, prev8, cur, taps):
    rows = cur.shape[0]
    cs_ref[0:SUBLANES, :] = prev8
    cs_ref[SUBLANES:SUBLANES + rows, :] = cur
    x1 = cs_ref[pl.ds(SUBLANES - 1, rows), :]
    x2 = cs_ref[pl.ds(SUBLANES - 2, rows), :]
    return taps[0:1] * x2 + taps[1:2] * x1 + taps[2:3] * cur


def _memkv_kernel(mem_ref, g_ref, wk_ref, wv_ref, k32_ref, v32_ref, kbf_ref, vbf_ref):
    m = mem_ref[...]
    hm = (m * _rms_scale(m) * g_ref[...]).astype(BF16)
    k = _dot(hm, wk_ref[...])
    v = _dot(hm, wv_ref[...])
    for h in range(N_MEM_HEADS):
        k32_ref[:, h, :] = k[:, MEM_HEAD_DIM * h:MEM_HEAD_DIM * (h + 1)]
        v32_ref[:, h, :] = v[:, MEM_HEAD_DIM * h:MEM_HEAD_DIM * (h + 1)]
    kbf_ref[...] = k.astype(BF16)
    vbf_ref[...] = v.astype(BF16)


def _memory_kv(mem2d, g_mem, wk, wv):
    rows = mem2d.shape[0]
    tm = N_MEM
    out_w = wk.shape[1]
    row = lambda i: (i, 0)
    row3 = lambda i: (i, 0, 0)
    return pl.pallas_call(
        _memkv_kernel,
        grid=(rows // tm,),
        in_specs=[
            pl.BlockSpec((tm, D_MODEL), row),
            _resident((1, D_MODEL)),
            _resident((D_MODEL, out_w)),
            _resident((D_MODEL, out_w)),
        ],
        out_specs=[
            pl.BlockSpec((tm, N_MEM_HEADS, MEM_HEAD_DIM), row3),
            pl.BlockSpec((tm, N_MEM_HEADS, MEM_HEAD_DIM), row3),
            pl.BlockSpec((tm, out_w), row),
            pl.BlockSpec((tm, out_w), row),
        ],
        out_shape=[
            jax.ShapeDtypeStruct((rows, N_MEM_HEADS, MEM_HEAD_DIM), F32),
            jax.ShapeDtypeStruct((rows, N_MEM_HEADS, MEM_HEAD_DIM), F32),
            jax.ShapeDtypeStruct((rows, out_w), BF16),
            jax.ShapeDtypeStruct((rows, out_w), BF16),
        ],
        compiler_params=pltpu.CompilerParams(
            dimension_semantics=("arbitrary",), vmem_limit_bytes=VMEM_LIMIT_BYTES),
        name="memory_kv",
    )(mem2d, g_mem, wk, wv)


def _mixer_kernel(x_ref, cos_ref, s1_ref, s2_ref, kinit_ref, vinit_ref, cinit_ref,
                  gpre_ref, win_ref, convw_ref, gconv_ref, gattn_ref, sink_ref,
                  mix_ref, kout_ref, vout_ref, cstate_ref,
                  h_ref, yc_ref, q_ref, ke_ref, ko_ref, ve_ref, vo_ref, cs_ref, ssq_ref,
                  *, tm, n_seg, carry, keep, conv_cb, ab):
    i = pl.program_id(1)
    seg = tm // n_seg
    nblk = seg // ab
    nkeys = ab + WINDOW
    lo = lax.broadcasted_iota(jnp.int32, (1, LANES), 1) < HALF

    def rope(blk, cos_t, s1_t, s2_t):
        return (blk * cos_t + pltpu.roll(blk, LANES - ROT_DIM // 2, 1) * s1_t
                + pltpu.roll(blk, ROT_DIM // 2, 1) * s2_t)

    def store_kv(kf, vf, s, row0):
        rows = kf.shape[0]
        for hp in range(N_KV_HEADS // 2):
            own = kf[:, LANES * hp:LANES * (hp + 1)]
            swp = pltpu.roll(own, HALF, 1)
            for par in range(2):
                h = 2 * hp + par
                low, high = (own, swp) if par == 0 else (swp, own)
                ke_ref[s, h, row0:row0 + rows, :] = jnp.where(lo, low, 0.0).astype(BF16)
                ko_ref[s, h, row0:row0 + rows, :] = jnp.where(lo, 0.0, high).astype(BF16)
        vt = vf.T.astype(BF16)
        zeros = jnp.zeros((HALF, rows), BF16)
        for h in range(N_KV_HEADS):
            vh = vt[HEAD_DIM * h:HEAD_DIM * (h + 1)]
            ve_ref[s, h, :, row0:row0 + rows] = jnp.concatenate([vh, zeros], axis=0)
            vo_ref[s, h, :, row0:row0 + rows] = jnp.concatenate([zeros, vh], axis=0)

    @pl.when(i == 0)
    def _():
        for s in range(n_seg):
            store_kv(kinit_ref[s], vinit_ref[s], s, 0)
            cstate_ref[s] = cinit_ref[s]

    x = x_ref[...]
    h_ref[...] = (x * _rms_scale(x) * gpre_ref[...]).astype(BF16)
    hb = h_ref[...]

    def conv_block(cb):
        hbl = h_ref[...]
        c0 = _aligned(cb * conv_cb, conv_cb)
        bg = _dot(hbl, win_ref[:, pl.ds(c0, conv_cb)])
        cg = _dot(hbl, win_ref[:, pl.ds(D_CONV + c0, conv_cb)])
        ug = _dot(hbl, win_ref[:, pl.ds(2 * D_CONV + c0, conv_cb)])
        cu = cg * ug
        taps = convw_ref[cb]
        ssq_parts = []
        for s in range(n_seg):
            r0 = s * seg
            cur = cu[r0:r0 + seg]
            y = bg[r0:r0 + seg] * _causal_conv(cs_ref, cstate_ref[s, cb], cur, taps)
            ssq_parts.append(jnp.sum(y * y, axis=-1, keepdims=True))
            yc_ref[cb, r0:r0 + seg, :] = y
            cstate_ref[s, cb] = cur[seg - SUBLANES:seg]
        return ssq_parts[0] if n_seg == 1 else jnp.concatenate(ssq_parts, axis=0)

    cos_t = cos_ref[...]
    s1_t = s1_ref[...]
    s2_t = s2_ref[...]
    q0 = 3 * D_CONV
    qf = _dot(hb, win_ref[:, q0:q0 + ATTN_WIDTH])
    for jb in range(ATTN_WIDTH // LANES):
        blk = rope(qf[:, LANES * jb:LANES * (jb + 1)], cos_t, s1_t, s2_t)
        q_ref[:, LANES * jb:LANES * (jb + 1)] = (blk * (HEAD_DIM ** -0.5)).astype(BF16)
    k0 = q0 + ATTN_WIDTH
    kv = _dot(hb, win_ref[:, k0:k0 + 2 * KV_WIDTH])
    kf = jnp.concatenate(
        [rope(kv[:, LANES * jb:LANES * (jb + 1)], cos_t, s1_t, s2_t)
         for jb in range(KV_WIDTH // LANES)], axis=1)
    vf = kv[:, KV_WIDTH:2 * KV_WIDTH]
    for s in range(n_seg):
        r0 = s * seg
        store_kv(kf[r0:r0 + seg], vf[r0:r0 + seg], s, WINDOW)
        kout_ref[s] = kf[r0 + seg - keep:r0 + seg]
        vout_ref[s] = vf[r0 + seg - keep:r0 + seg]

    key_pos = lax.broadcasted_iota(jnp.int32, (nkeys, 2 * ab), 0)
    if ab > CHUNK:
        tok = lax.broadcasted_iota(jnp.int32, (nkeys, 2 * ab), 1) % ab
        first_key = (tok // CHUNK) * CHUNK
        band_bias = jnp.where((key_pos >= first_key) & (key_pos < first_key + BAND), 0.0, NEG_INF)
    else:
        band_bias = None
    feat_lo = lax.broadcasted_iota(jnp.int32, (LANES, 1), 0) < HALF

    def block_place(idx):
        s = 0 if n_seg == 1 else idx // nblk
        j = 0 if nblk == 1 else idx % nblk
        return s, pl.ds(_aligned(idx * ab, ab), ab), pl.ds(_aligned(j * ab, ab), nkeys)

    def attend_block(idx):
        s, rows, band = block_place(idx)
        j = 0 if nblk == 1 else idx % nblk
        bias = band_bias
        if carry:
            first_valid = WINDOW - (i * seg + j * ab)
            start_bias = jnp.where(key_pos >= first_valid, 0.0, NEG_INF)
            bias = start_bias if bias is None else bias + start_bias
        heads = range(N_KV_HEADS)
        scores = []
        for h in heads:
            qa = jnp.concatenate(
                [q_ref[rows, 2 * LANES * h:2 * LANES * h + LANES],
                 q_ref[rows, 2 * LANES * h + LANES:2 * LANES * (h + 1)]], axis=0)
            st_e = _dot_nt(ke_ref[s, h, band, :], qa)
            st_o = _dot_nt(ko_ref[s, h, band, :], qa)
            if bias is not None:
                st_e = st_e + bias
                st_o = st_o + bias
            scores.append((st_e, st_o))
        stats = []
        for h in heads:
            sink_e = sink_ref[h, 0:1, :]
            sink_o = sink_ref[h, 1:2, :]
            m_e = jnp.maximum(jnp.max(scores[h][0], axis=0, keepdims=True), sink_e)
            m_o = jnp.maximum(jnp.max(scores[h][1], axis=0, keepdims=True), sink_o)
            stats.append((m_e, m_o, jnp.exp(sink_e - m_e), jnp.exp(sink_o - m_o)))
        outs = []
        for h in heads:
            p_e = jnp.exp(scores[h][0] - stats[h][0])
            p_o = jnp.exp(scores[h][1] - stats[h][1])
            den_e = jnp.sum(p_e, axis=0, keepdims=True) + stats[h][2]
            den_o = jnp.sum(p_o, axis=0, keepdims=True) + stats[h][3]
            acc_t = (_dot(ve_ref[s, h, :, band], p_e.astype(BF16))
                     + _dot(vo_ref[s, h, :, band], p_o.astype(BF16)))
            acc = (acc_t * jnp.where(feat_lo, 1.0 / den_e, 1.0 / den_o)).T
            outs += [acc[0:ab], acc[ab:2 * ab]]
        o = jnp.concatenate(outs, axis=1)
        mix_ref[rows, D_CONV:D_CONV + ATTN_WIDTH] = (
            o * _rms_scale(o) * gattn_ref[...]).astype(BF16)

    n_trips = n_seg * nblk
    for t in range(n_trips):
        @pl.when(i >= 0)
        def _(t=t):
            ssq_ref[t] = conv_block(t)
            attend_block(t)
    ssq_c = ssq_ref[0]
    for t in range(1, n_trips):
        ssq_c = ssq_c + ssq_ref[t]

    if carry:
        for h in range(N_KV_HEADS):
            for ref in (ke_ref, ko_ref):
                ref[0, h, 0:WINDOW, :] = ref[0, h, seg:seg + WINDOW, :]
            for ref in (ve_ref, vo_ref):
                ref[0, h, :, 0:WINDOW] = ref[0, h, :, seg:seg + WINDOW]

    rs_c = lax.rsqrt(ssq_c * (1.0 / D_CONV) + EPS)
    for cb in range(D_CONV // conv_cb):
        mix_ref[:, cb * conv_cb:(cb + 1) * conv_cb] = (
            yc_ref[cb] * rs_c * gconv_ref[cb]).astype(BF16)


def _mixer(x2d, tabs, kinit, vinit, cinit, gpre, win, convw, gconv, gattn, sinks,
           *, n_batch, tm, n_seg, carry):
    rows = x2d.shape[0]
    nt = rows // (n_batch * tm)
    seg = tm // n_seg
    keep = min(WINDOW, seg)
    ab = min(ATTN_BLOCK, seg)
    n_trips = tm // ab
    conv_cb = D_CONV // n_trips
    row_blk = lambda b, i: (b * nt + i, 0)
    tab_blk = lambda b, i: (i, 0)
    seg_blk = lambda b, i: (b, 0, 0)
    cst_blk = pl.BlockSpec((n_seg, n_trips, SUBLANES, conv_cb), lambda b, i: (b, 0, 0, 0))
    kern = functools.partial(_mixer_kernel, tm=tm, n_seg=n_seg, carry=carry, keep=keep,
                             conv_cb=conv_cb, ab=ab)
    n_segs_total = n_batch * n_seg
    taps = jnp.swapaxes(convw.reshape(CONV_WIDTH, n_trips, conv_cb), 0, 1)
    gconv_b = gconv.reshape(n_trips, 1, conv_cb)
    cinit_b = jnp.swapaxes(cinit.reshape(n_segs_total, SUBLANES, n_trips, conv_cb), 1, 2)
    by_parity = jnp.swapaxes(sinks.reshape(N_KV_HEADS, 2, 2), 1, 2)[:, :, :, None]
    sink_tab = jnp.broadcast_to(by_parity, (N_KV_HEADS, 2, 2, ab)).reshape(N_KV_HEADS, 2, 2 * ab)
    sink_tab = jnp.pad(sink_tab, ((0, 0), (0, SUBLANES - 2), (0, 0)))
    mix, k_tail, v_tail, cstate = pl.pallas_call(
        kern,
        grid=(n_batch, nt),
        in_specs=[
            pl.BlockSpec((tm, D_MODEL), row_blk),
            pl.BlockSpec((tm, LANES), tab_blk),
            pl.BlockSpec((tm, LANES), tab_blk),
            pl.BlockSpec((tm, LANES), tab_blk),
            pl.BlockSpec((n_seg, WINDOW, KV_WIDTH), seg_blk),
            pl.BlockSpec((n_seg, WINDOW, KV_WIDTH), seg_blk),
            cst_blk,
            _resident((1, D_MODEL)),
            _resident(win.shape),
            _resident(taps.shape),
            _resident(gconv_b.shape),
            _resident((1, ATTN_WIDTH)),
            _resident((N_KV_HEADS, SUBLANES, 2 * ab)),
        ],
        out_specs=[
            pl.BlockSpec((tm, D_CONV + ATTN_WIDTH), row_blk),
            pl.BlockSpec((n_seg, keep, KV_WIDTH), seg_blk),
            pl.BlockSpec((n_seg, keep, KV_WIDTH), seg_blk),
            cst_blk,
        ],
        out_shape=[
            jax.ShapeDtypeStruct((rows, D_CONV + ATTN_WIDTH), BF16),
            jax.ShapeDtypeStruct((n_segs_total, keep, KV_WIDTH), F32),
            jax.ShapeDtypeStruct((n_segs_total, keep, KV_WIDTH), F32),
            jax.ShapeDtypeStruct((n_segs_total, n_trips, SUBLANES, conv_cb), F32),
        ],
        scratch_shapes=[
            pltpu.VMEM((tm, D_MODEL), BF16),
            pltpu.VMEM((n_trips, tm, conv_cb), F32),
            pltpu.VMEM((tm, ATTN_WIDTH), BF16),
            pltpu.VMEM((n_seg, N_KV_HEADS, WINDOW + seg, LANES), BF16),
            pltpu.VMEM((n_seg, N_KV_HEADS, WINDOW + seg, LANES), BF16),
            pltpu.VMEM((n_seg, N_KV_HEADS, LANES, WINDOW + seg), BF16),
            pltpu.VMEM((n_seg, N_KV_HEADS, LANES, WINDOW + seg), BF16),
            pltpu.VMEM((SUBLANES + seg, conv_cb), F32),
            pltpu.VMEM((n_trips, tm, 1), F32),
        ],
        compiler_params=pltpu.CompilerParams(
            dimension_semantics=("arbitrary", "arbitrary"), vmem_limit_bytes=VMEM_LIMIT_BYTES),
        name="mixer",
    )(x2d, *tabs, kinit, vinit, cinit_b, gpre, win, taps, gconv_b, gattn, sink_tab)
    cstate = jnp.swapaxes(cstate, 1, 2).reshape(n_segs_total, SUBLANES, D_CONV)
    return mix, k_tail, v_tail, cstate


def _mix_out_kernel(x_ref, mix_ref, wout_ref, gpost_ref, out_ref):
    half = x_ref.shape[0] // 2
    for r0 in (0, half):
        y = _dot(mix_ref[r0:r0 + half, :], wout_ref[...])
        out_ref[r0:r0 + half, :] = x_ref[r0:r0 + half, :] + y * _rms_scale(y) * gpost_ref[...]


def _mix_out(x2d, mix, wout, gpost, *, tm):
    rows = x2d.shape[0]
    row_blk = lambda i: (i, 0)
    width = mix.shape[1]
    return pl.pallas_call(
        _mix_out_kernel,
        grid=(rows // tm,),
        in_specs=[
            pl.BlockSpec((tm, D_MODEL), row_blk),
            pl.BlockSpec((tm, width), row_blk),
            _resident((width, D_MODEL)),
            _resident((1, D_MODEL)),
        ],
        out_specs=pl.BlockSpec((tm, D_MODEL), row_blk),
        out_shape=jax.ShapeDtypeStruct((rows, D_MODEL), F32),
        compiler_params=pltpu.CompilerParams(
            dimension_semantics=("arbitrary",), vmem_limit_bytes=VMEM_LIMIT_BYTES),
        name="mix_out",
    )(x2d, mix, wout, gpost)


def _cross_kernel(x_ref, mk_ref, mv_ref, gpre_ref, wq_ref, wo_ref, gpost_ref, gnext_ref,
                  out_ref, hnext_ref, o_ref, *, tm, n_seg):
    seg = tm // n_seg
    x = x_ref[...]
    hb = (x * _rms_scale(x) * gpre_ref[...]).astype(BF16)
    qs = [_dot(hb, wq_ref[:, 0:MEM_HEAD_DIM]).astype(BF16)]
    for hd in range(N_MEM_HEADS):
        c0 = hd * MEM_HEAD_DIM
        if hd + 1 < N_MEM_HEADS:
            qs.append(_dot(hb, wq_ref[:, c0 + MEM_HEAD_DIM:c0 + 2 * MEM_HEAD_DIM]).astype(BF16))
        q = qs[hd]
        for s in range(n_seg):
            r0 = s * seg
            sc = _dot_nt(q[r0:r0 + seg], mk_ref[s, :, c0:c0 + MEM_HEAD_DIM]) * (MEM_HEAD_DIM ** -0.5)
            p = jnp.exp(sc - jnp.max(sc, axis=-1, keepdims=True))
            den = jnp.sum(p, axis=-1, keepdims=True)
            o = _dot(p.astype(BF16), mv_ref[s, :, c0:c0 + MEM_HEAD_DIM]) * (1.0 / den)
            o_ref[r0:r0 + seg, c0:c0 + MEM_HEAD_DIM] = o.astype(BF16)
    half = tm // 2
    for r0 in (0, half):
        y = _dot(o_ref[r0:r0 + half, :], wo_ref[...])
        x_out = x[r0:r0 + half] + y * _rms_scale(y) * gpost_ref[...]
        out_ref[r0:r0 + half, :] = x_out
        hnext_ref[r0:r0 + half, :] = (x_out * _rms_scale(x_out) * gnext_ref[...]).astype(BF16)


def _cross(x2d, mk, mv, gpre, wq, wo, gpost, gnext, *, n_batch, tm, n_seg):
    rows = x2d.shape[0]
    nt = rows // (n_batch * tm)
    row_blk = lambda b, i: (b * nt + i, 0)
    if n_batch == 1:
        mem_blk = _resident((n_seg, N_MEM, D_MODEL))
    else:
        mem_blk = pl.BlockSpec((n_seg, N_MEM, D_MODEL), lambda b, i: (b, 0, 0))
    kern = functools.partial(_cross_kernel, tm=tm, n_seg=n_seg)
    return pl.pallas_call(
        kern,
        grid=(n_batch, nt),
        in_specs=[
            pl.BlockSpec((tm, D_MODEL), row_blk),
            mem_blk, mem_blk,
            _resident((1, D_MODEL)),
            _resident((D_MODEL, D_MODEL)),
            _resident((D_MODEL, D_MODEL)),
            _resident((1, D_MODEL)),
            _resident((1, D_MODEL)),
        ],
        out_specs=[pl.BlockSpec((tm, D_MODEL), row_blk), pl.BlockSpec((tm, D_MODEL), row_blk)],
        out_shape=[jax.ShapeDtypeStruct((rows, D_MODEL), F32),
                   jax.ShapeDtypeStruct((rows, D_MODEL), BF16)],
        scratch_shapes=[pltpu.VMEM((tm, D_MODEL), BF16)],
        compiler_params=pltpu.CompilerParams(
            dimension_semantics=("arbitrary", "arbitrary"), vmem_limit_bytes=VMEM_LIMIT_BYTES),
        name="cross_attn",
    )(x2d, mk, mv, gpre, wq, wo, gpost, gnext)


def _ffn_kernel(h_ref, x_hbm, ginit_ref, wg_ref, wu_ref, cw_ref, cb_ref, wd_ref, gpost_ref,
                out_ref, gstate_ref,
                gc_ref, cs_ref, xbuf_ref, xsem, *, tm, n_seg, fc, nt, nf):
    b = pl.program_id(0)
    i = pl.program_id(1)
    f = pl.program_id(2)
    seg = tm // n_seg

    def residual_copy():
        row0 = pl.multiple_of((b * nt + i) * tm, tm)
        return pltpu.make_async_copy(x_hbm.at[pl.ds(row0, tm)], xbuf_ref, xsem)

    @pl.when(f == 0)
    def _():
        out_ref[...] = jnp.zeros_like(out_ref)

    @pl.when(f == max(nf - 3, 0))
    def _():
        residual_copy().start()

    @pl.when(i == 0)
    def _():
        gc_ref[f] = ginit_ref[...]

    hb = h_ref[...]
    g = _dot(hb, wg_ref[...])
    u = _dot(hb, wu_ref[...])
    acts = []
    for s in range(n_seg):
        r0 = s * seg
        cur = g[r0:r0 + seg]
        a = _causal_conv(cs_ref, gc_ref[f, s], cur, cw_ref[...]) + cb_ref[...]
        acts.append((a * (1.0 / (1.0 + jnp.exp(-a))) * u[r0:r0 + seg]).astype(BF16))
        last = cur[seg - SUBLANES:seg]
        gc_ref[f, s] = last
        gstate_ref[s, f] = last
    act = acts[0] if n_seg == 1 else jnp.concatenate(acts, axis=0)
    out_ref[...] += _dot(act, wd_ref[...])

    @pl.when(f == nf - 1)
    def _():
        residual_copy().wait()
        y = out_ref[...]
        out_ref[...] = xbuf_ref[...] + y * _rms_scale(y) * gpost_ref[...]


def _ffn(h2d, x2d, ginit, wg, wu, cw, cb, wd, gpost, *, n_batch, tm, n_seg, fc):
    rows = x2d.shape[0]
    nt = rows // (n_batch * tm)
    nf = D_FF // fc
    seg = tm // n_seg
    row_blk = lambda b, i, f: (b * nt + i, 0)
    st_blk = pl.BlockSpec((n_seg, SUBLANES, fc), lambda b, i, f: (b, 0, f))
    kern = functools.partial(_ffn_kernel, tm=tm, n_seg=n_seg, fc=fc, nt=nt, nf=nf)
    return pl.pallas_call(
        kern,
        grid=(n_batch, nt, nf),
        in_specs=[
            pl.BlockSpec((tm, D_MODEL), row_blk),
            pl.BlockSpec(memory_space=pl.ANY),
            st_blk,
            pl.BlockSpec((D_MODEL, fc), lambda b, i, f: (0, f)),
            pl.BlockSpec((D_MODEL, fc), lambda b, i, f: (0, f)),
            pl.BlockSpec((CONV_WIDTH, fc), lambda b, i, f: (0, f)),
            pl.BlockSpec((1, fc), lambda b, i, f: (0, f)),
            pl.BlockSpec((fc, D_MODEL), lambda b, i, f: (f, 0)),
            _resident((1, D_MODEL)),
        ],
        out_specs=[
            pl.BlockSpec((tm, D_MODEL), row_blk),
            pl.BlockSpec((n_seg, nf, SUBLANES, fc), lambda b, i, f: (b, 0, 0, 0)),
        ],
        out_shape=[
            jax.ShapeDtypeStruct((rows, D_MODEL), F32),
            jax.ShapeDtypeStruct((n_batch * n_seg, nf, SUBLANES, fc), F32),
        ],
        scratch_shapes=[
            pltpu.VMEM((nf, n_seg, SUBLANES, fc), F32),
            pltpu.VMEM((SUBLANES + seg, fc), F32),
            pltpu.VMEM((tm, D_MODEL), F32),
            pltpu.SemaphoreType.DMA(()),
        ],
        compiler_params=pltpu.CompilerParams(
            dimension_semantics=("arbitrary", "arbitrary", "arbitrary"),
            vmem_limit_bytes=VMEM_LIMIT_BYTES),
        name="conv_ffn",
    )(h2d, x2d, ginit, wg, wu, cw, cb, wd, gpost)


def _rope_tables(pos):
    half = ROT_DIM // 2
    inv = ROPE_THETA ** (-jnp.arange(0, ROT_DIM, 2, dtype=F32) / ROT_DIM)
    ang = pos[:, None] * inv[None, :]
    cos = jnp.tile(jnp.cos(ang), (1, LANES // half))
    sin = jnp.tile(jnp.sin(ang), (1, LANES // half))
    dim = jnp.arange(LANES) % HEAD_DIM
    first, second = (dim < half)[None, :], ((dim >= half) & (dim < ROT_DIM))[None, :]
    return (jnp.where(first | second, cos, 1.0), jnp.where(first, -sin, 0.0),
            jnp.where(second, sin, 0.0))


def _merge_heads_bf16(mem):
    n, m, heads, d = mem.shape
    return mem.reshape(n, m, heads * d).astype(BF16)


def _pad_state(state):
    return jnp.pad(state, ((0, 0), (SUBLANES - (CONV_WIDTH - 1), 0), (0, 0)))


def _layer(x2d, tabs, kinit, vinit, cinit, finit, mk, mv, w, *, mixer_cfg, mix_out_tm, cross_cfg,
           ffn_cfg, carry):
    tile = lambda cfg: dict(n_batch=cfg[0], tm=cfg[1], n_seg=cfg[2])
    mix, k_new, v_new, cstate = _mixer(
        x2d, tabs, kinit, vinit, cinit, w["g_mix_pre"], w["w_mix_in"], w["conv_mix_w"],
        w["g_grp_conv"], w["g_grp_attn"], w["sinks"], carry=carry, **tile(mixer_cfg))
    x1 = _mix_out(x2d, mix, w["w_mix_out"], w["g_mix_post"], tm=mix_out_tm)
    x2, h_ffn = _cross(x1, mk, mv, w["g_x_pre"], w["w_xq"], w["w_xo"], w["g_x_post"],
                       w["g_ffn_pre"], **tile(cross_cfg))
    x3, fstate = _ffn(h_ffn, x2, finit, w["w_gate"], w["w_up"], w["conv_ffn_w"],
                      w["conv_ffn_b"], w["w_down"], w["g_ffn_post"], fc=FFN_CHUNK,
                      **tile(ffn_cfg))
    fstate = jnp.swapaxes(fstate, 1, 2).reshape(fstate.shape[0], SUBLANES, D_FF)
    return x3, k_new, v_new, cstate, fstate


def kernel(x_prompt, x_sample, cache_mem_k, cache_mem_v, cache_swa_k, cache_swa_v,
           state_mix_conv, state_ffn_conv, mem_prompt,
           g_mix_pre, w_mix_in, conv_mix_w, g_grp_conv, g_grp_attn, attn_sinks,
           w_mix_out, g_mix_post, g_mem, w_xk, w_xv, g_x_pre, w_xq, w_xo, g_x_post,
           g_ffn_pre, w_gate, w_up, conv_ffn_w, conv_ffn_b, w_down, g_ffn_post):
    n_p, s_p, _ = x_prompt.shape
    n_s, s_s, _ = x_sample.shape
    depth = w_mix_in.shape[0]
    swa_len = cache_swa_k.shape[2]
    tabs_p = _rope_tables(jnp.arange(s_p, dtype=F32))
    tabs_s = tuple(jnp.tile(a, (n_s // 2, 1))
                   for a in _rope_tables(PAST_LEN + jnp.arange(s_s, dtype=F32)))

    yp = x_prompt.reshape(n_p * s_p, D_MODEL)
    ys = x_sample.reshape(n_s * s_s, D_MODEL)
    outs = [[] for _ in range(10)]
    for l in range(depth):
        row = lambda a: a[l][None, :]
        w = {
            "sinks": attn_sinks[l],
            "g_mix_pre": row(g_mix_pre), "w_mix_in": w_mix_in[l].astype(BF16),
            "conv_mix_w": conv_mix_w[l], "g_grp_conv": row(g_grp_conv),
            "g_grp_attn": row(g_grp_attn),
            "w_mix_out": w_mix_out[l].astype(BF16), "g_mix_post": row(g_mix_post),
            "g_x_pre": row(g_x_pre), "w_xq": w_xq[l].astype(BF16), "w_xo": w_xo[l].astype(BF16),
            "g_x_post": row(g_x_post), "g_ffn_pre": row(g_ffn_pre),
            "w_gate": w_gate[l].astype(BF16), "w_up": w_up[l].astype(BF16),
            "conv_ffn_w": conv_ffn_w[l], "conv_ffn_b": row(conv_ffn_b),
            "w_down": w_down[l].astype(BF16), "g_ffn_post": row(g_ffn_post),
        }
        mk32, mv32, mkb, mvb = _memory_kv(
            mem_prompt.reshape(n_p * N_MEM, D_MODEL), row(g_mem),
            w_xk[l].astype(BF16), w_xv[l].astype(BF16))
        yp, kp, vp, cp, fp = _layer(
            yp, tabs_p,
            jnp.zeros((n_p, WINDOW, KV_WIDTH), F32), jnp.zeros((n_p, WINDOW, KV_WIDTH), F32),
            jnp.zeros((n_p, SUBLANES, D_CONV), F32), jnp.zeros((n_p, SUBLANES, D_FF), F32),
            mkb.reshape(n_p, N_MEM, D_MODEL), mvb.reshape(n_p, N_MEM, D_MODEL), w,
            mixer_cfg=(n_p, 512, 1), mix_out_tm=1024, cross_cfg=(n_p, 512, 1),
            ffn_cfg=(n_p, 1024, 1), carry=True)
        ys, ks, vs, cs, fs = _layer(
            ys, tabs_s,
            cache_swa_k[l].reshape(n_s, swa_len, KV_WIDTH),
            cache_swa_v[l].reshape(n_s, swa_len, KV_WIDTH),
            _pad_state(state_mix_conv[l]), _pad_state(state_ffn_conv[l]),
            _merge_heads_bf16(cache_mem_k[l]), _merge_heads_bf16(cache_mem_v[l]), w,
            mixer_cfg=(2, n_s * s_s // 2, n_s // 2), mix_out_tm=n_s * s_s,
            cross_cfg=(1, n_s * s_s, n_s),
            ffn_cfg=(1, n_s * s_s, n_s), carry=False)
        keep_p = min(WINDOW, s_p)
        tail = CONV_WIDTH - 1
        new_k = ks.reshape(n_s, s_s, N_KV_HEADS, HEAD_DIM)
        new_v = vs.reshape(n_s, s_s, N_KV_HEADS, HEAD_DIM)
        layer_out = (
            mk32.reshape(n_p, N_MEM, N_MEM_HEADS, MEM_HEAD_DIM),
            mv32.reshape(n_p, N_MEM, N_MEM_HEADS, MEM_HEAD_DIM),
            kp.reshape(n_p, keep_p, N_KV_HEADS, HEAD_DIM),
            vp.reshape(n_p, keep_p, N_KV_HEADS, HEAD_DIM),
            cp[:, SUBLANES - tail:], fp[:, SUBLANES - tail:],
            jnp.concatenate([cache_swa_k[l], new_k], axis=1)[:, -swa_len:],
            jnp.concatenate([cache_swa_v[l], new_v], axis=1)[:, -swa_len:],
            cs[:, SUBLANES - tail:], fs[:, SUBLANES - tail:],
        )
        for acc, o in zip(outs, layer_out):
            acc.append(o)

    return (yp.reshape(n_p, s_p, D_MODEL), ys.reshape(n_s, s_s, D_MODEL),
            *[jnp.stack(o) for o in outs])
```
